```python
import math, functools
import jax, jax.numpy as jnp
from jax import lax
import numpy as np


D_MODEL = 1024
BATCH = 2
SEQ = 8192
DEPTH = 2

GRID_W = 64
CTX_LEN = 256
CHUNK = 64
CONV_W = 5
POS_BASE = 10000.0
EPS = 1e-6
N_DIR = 2
ALPHA = (2 * DEPTH) ** 0.25
BETA = (8 * DEPTH) ** -0.25

A_HEADS = 4
A_DK = 128
A_DV = 128
B_HEADS = 8
B_HEADDIM = 64
B_GROUPS = 2
B_STATE = 128
C_HEADS = 4
C_DK = 128
C_DV = 256
PEER_HEADS = 8
PEER_NKEYS = 128
PEER_EXPERTS = PEER_NKEYS * PEER_NKEYS
PEER_DKEY = 256
PEER_TOPK = 16
PEER_BLOCK = 128

A_QKV = A_HEADS * (2 * A_DK + A_DV)
A_Z = A_HEADS * A_DV
A_GATES = N_DIR * A_HEADS
B_DINNER = B_HEADS * B_HEADDIM
B_XBC = B_DINNER + 2 * B_GROUPS * B_STATE
B_DT = N_DIR * B_HEADS
EVEN_IN = A_QKV + A_Z + 2 * A_GATES + B_DINNER + B_XBC + B_DT
EVEN_MIX = A_Z + B_DINNER
C_QK = 2 * C_HEADS * C_DK
C_V = C_HEADS * C_DV
C_GATES = N_DIR * C_HEADS
ODD_IN = C_QK + 2 * C_V + 2 * C_GATES
ODD_MIX = C_V

kernel_name = 'hybrid_deltanet_ssd_mlstm_peer_dit'


def split_cols(a, sizes):
    return jnp.split(a, [int(s) for s in np.cumsum(sizes)[:-1]], axis=-1)


def layer_norm(x, g, b):
    xf = x.astype(jnp.float32)
    mu = jnp.mean(xf, -1, keepdims=True)
    var = jnp.mean(jnp.square(xf - mu), -1, keepdims=True)
    return ((xf - mu) * lax.rsqrt(var + EPS) * g + b).astype(x.dtype)


def rms_norm(x, g):
    xf = x.astype(jnp.float32)
    return (xf * lax.rsqrt(jnp.mean(jnp.square(xf), -1, keepdims=True) + EPS) * g).astype(x.dtype)


def l2_normalize(x):
    xf = x.astype(jnp.float32)
    return xf * lax.rsqrt(jnp.sum(jnp.square(xf), -1, keepdims=True) + EPS)


def modulate(x, shift, scale):
    return x * (1.0 + scale) + shift


def centred_conv(x, w, b=None):
    ch = x.shape[-1]
    y = lax.conv_general_dilated(x, w[:, None, :].astype(x.dtype), window_strides=(1,), padding='SAME',
                                 dimension_numbers=('NWC', 'WIO', 'NWC'), feature_group_count=ch)
    return y if b is None else y + b


def grid_sincos(n_tokens, d):
    t = jnp.arange(n_tokens)
    row = (t // GRID_W).astype(jnp.float32)[:, None]
    col = (t % GRID_W).astype(jnp.float32)[:, None]
    n_freq = d // 4
    freq = jnp.exp(-math.log(POS_BASE) * jnp.arange(n_freq, dtype=jnp.float32) / n_freq)[None, :]
    return jnp.concatenate([jnp.sin(row * freq), jnp.cos(row * freq), jnp.sin(col * freq), jnp.cos(col * freq)], -1)


def to_column_major(x, rows):
    b, _, d = x.shape
    return x.reshape(b, rows, GRID_W, d).transpose(0, 2, 1, 3).reshape(b, rows * GRID_W, d)


def to_row_major(x, rows):
    b, _, d = x.shape
    return x.reshape(b, GRID_W, rows, d).transpose(0, 2, 1, 3).reshape(b, rows * GRID_W, d)


def gated_delta_scan(q, k, v, g, beta, state):
    bsz, seq, heads, _ = q.shape
    dv = v.shape[-1]
    n = seq // CHUNK

    def chunks(a):
        return jnp.moveaxis(a.reshape((bsz, n, CHUNK) + a.shape[2:]), 3, 1)

    q, k, v, g, beta = (chunks(a) for a in (q, k, v, g, beta))
    causal = jnp.tril(jnp.ones((CHUNK, CHUNK), dtype=bool))
    gc = jnp.cumsum(g, axis=-1)
    decay = jnp.exp(jnp.where(causal, gc[..., :, None] - gc[..., None, :], -jnp.inf))
    k_beta = k * beta[..., None]
    l_mat = jnp.einsum('bhnid,bhnjd->bhnij', k_beta, k) * decay
    rhs = jnp.concatenate([v * beta[..., None], k_beta * jnp.exp(gc)[..., None]], axis=-1)
    sol = lax.linalg.triangular_solve(l_mat, rhs, left_side=True, lower=True, unit_diagonal=True)
    u, w = sol[..., :dv], sol[..., dv:]
    qk = jnp.einsum('bhnid,bhnjd->bhnij', q, k) * decay
    q_dec = q * jnp.exp(gc)[..., None]
    k_dec = k * jnp.exp(gc[..., -1:] - gc)[..., None]
    g_end = jnp.exp(gc[..., -1])

    def step(s, inp):
        u_c, w_c, qk_c, q_c, k_c, d_c = inp
        v_new = u_c - jnp.einsum('bhlk,bhkv->bhlv', w_c, s)
        o = jnp.einsum('bhlk,bhkv->bhlv', q_c, s) + jnp.einsum('bhls,bhsv->bhlv', qk_c, v_new)
        s = d_c[..., None, None] * s + jnp.einsum('bhlk,bhlv->bhkv', k_c, v_new)
        return s, o

    xs = tuple(jnp.moveaxis(a, 2, 0) for a in (u, w, qk, q_dec, k_dec, g_end))
    state, o = lax.scan(step, state, xs)
    o = jnp.moveaxis(o, 0, 2)
    return jnp.moveaxis(o, 1, 3).reshape(bsz, seq, heads, dv), state


def ssd_scan(x, dt, bm, cm, state, a):
    bsz, seq, heads, hd = x.shape
    groups, n_state = bm.shape[2], bm.shape[3]
    rep = heads // groups
    n = seq // CHUNK

    def chunks(arr):
        return arr.reshape((bsz, n, CHUNK) + arr.shape[2:])

    x = chunks(x).reshape(bsz, n, CHUNK, groups, rep, hd)
    dt = chunks(dt).reshape(bsz, n, CHUNK, groups, rep)
    bm, cm = chunks(bm), chunks(cm)
    causal = jnp.tril(jnp.ones((CHUNK, CHUNK), dtype=bool))[:, :, None, None]
    acs = jnp.cumsum(dt * a.reshape(groups, rep), axis=2)
    seg = jnp.exp(jnp.where(causal, acs[:, :, :, None] - acs[:, :, None, :], -jnp.inf))
    xdt = x * dt[..., None]
    cb = jnp.einsum('bclgn,bcsgn->bclsg', cm, bm)
    y_diag = jnp.einsum('bclsg,bclsgr,bcsgrp->bclgrp', cb, seg, xdt)
    to_end = jnp.exp(acs[:, :, -1:] - acs)
    chunk_states = jnp.einsum('bclgn,bclgr,bclgrp->bcgrnp', bm, to_end, xdt)
    chunk_decay = jnp.exp(acs[:, :, -1])

    def step(h, inp):
        s_c, d_c = inp
        return d_c[..., None, None] * h + s_c, h

    state, h_prev = lax.scan(step, state.reshape(bsz, groups, rep, n_state, hd),
                             (jnp.moveaxis(chunk_states, 1, 0), jnp.moveaxis(chunk_decay, 1, 0)))
    h_prev = jnp.moveaxis(h_prev, 0, 1)
    y_off = jnp.einsum('bclgn,bcgrnp,bclgr->bclgrp', cm, h_prev, jnp.exp(acs))
    y = (y_diag + y_off).reshape(bsz, seq, heads, hd)
    return y, state.reshape(bsz, heads, n_state, hd)


def mlstm_scan(q, k, v, log_i, log_f, state):
    bsz, seq, heads, _ = q.shape
    n = seq // CHUNK
    causal = jnp.tril(jnp.ones((CHUNK, CHUNK), dtype=bool))

    def chunks(a):
        a = a.reshape((bsz, n, CHUNK) + a.shape[2:])
        return jnp.moveaxis(jnp.moveaxis(a, 1, 0), 3, 2)

    def step(carry, inp):
        c_mem, n_mem, m = carry
        qc, kc, vc, ic, fc = inp
        b = jnp.cumsum(fc, axis=-1)
        logw_intra = jnp.where(causal, b[..., :, None] - b[..., None, :] + ic[..., None, :], -jnp.inf)
        logw_inter = b + m[..., None]
        m_t = jnp.maximum(logw_inter, jnp.max(logw_intra, -1))
        s = jnp.einsum('bhld,bhsd->bhls', qc, kc) * jnp.exp(logw_intra - m_t[..., None])
        w_inter = jnp.exp(logw_inter - m_t)
        num = jnp.einsum('bhls,bhsv->bhlv', s, vc) + w_inter[..., None] * jnp.einsum('bhld,bhdv->bhlv', qc, c_mem)
        den = jnp.sum(s, -1) + w_inter * jnp.einsum('bhld,bhd->bhl', qc, n_mem)
        h = num / jnp.maximum(jnp.abs(den), jnp.exp(-m_t))[..., None]
        b_last = b[..., -1]
        logw_end = b_last[..., None] - b + ic
        m_new = jnp.maximum(b_last + m, jnp.max(logw_end, -1))
        w_end = jnp.exp(logw_end - m_new[..., None])
        keep = jnp.exp(b_last + m - m_new)
        c_mem = keep[..., None, None] * c_mem + jnp.einsum('bhl,bhld,bhlv->bhdv', w_end, kc, vc)
        n_mem = keep[..., None] * n_mem + jnp.einsum('bhl,bhld->bhd', w_end, kc)
        return (c_mem, n_mem, m_new), h

    state, h = lax.scan(step, state, tuple(chunks(a) for a in (q, k, v, log_i, log_f)))
    h = jnp.moveaxis(jnp.moveaxis(h, 2, 3), 0, 1)
    return h.reshape(bsz, seq, heads, -1), state


def bidirectional_two_stream(scans, ctx_args, lat_args, state0):
    out_ctx, out_lat = 0.0, 0.0
    for d in range(N_DIR):
        order = (lambda a: a) if d == 0 else (lambda a: jnp.flip(a, axis=1))
        o_c, s_c = scans[d](*[order(a) for a in ctx_args[d]], state0)
        o_l, _ = scans[d](*[order(a) for a in lat_args[d]], s_c)
        out_ctx = out_ctx + order(o_c)
        out_lat = out_lat + order(o_l)
    return out_ctx, out_lat


def mixer_even(h_ctx, h_lat, need_ctx, *, w_in, conv_a, a_log_a, dt_bias_a, norm_a,
               conv_b, conv_b_bias, a_log_b, dt_bias_b, d_skip_b, norm_b, w_out):
    f32 = jnp.float32

    def project(h):
        bsz, seq, _ = h.shape
        qkv, z_a, beta_raw, decay_raw, z_b, xbc, dt_raw = split_cols(
            h @ w_in, (A_QKV, A_Z, A_GATES, A_GATES, B_DINNER, B_XBC, B_DT))
        qkv = jax.nn.silu(centred_conv(qkv, conv_a).astype(f32))
        q, k, v = split_cols(qkv, (A_HEADS * A_DK, A_HEADS * A_DK, A_HEADS * A_DV))
        q = l2_normalize(q.reshape(bsz, seq, A_HEADS, A_DK)) * A_DK ** -0.5
        k = l2_normalize(k.reshape(bsz, seq, A_HEADS, A_DK))
        v = v.reshape(bsz, seq, A_HEADS, A_DV)
        beta = jax.nn.sigmoid(beta_raw.astype(f32)).reshape(bsz, seq, N_DIR, A_HEADS)
        g = -jnp.exp(a_log_a.astype(f32)) * jax.nn.softplus(
            decay_raw.astype(f32).reshape(bsz, seq, N_DIR, A_HEADS) + dt_bias_a)
        a_args = tuple((q, k, v, g[:, :, d], beta[:, :, d]) for d in range(N_DIR))
        xbc = jax.nn.silu(centred_conv(xbc, conv_b, conv_b_bias).astype(f32))
        xs, bm, cm = split_cols(xbc, (B_DINNER, B_GROUPS * B_STATE, B_GROUPS * B_STATE))
        xs = xs.reshape(bsz, seq, B_HEADS, B_HEADDIM)
        bm = bm.reshape(bsz, seq, B_GROUPS, B_STATE)
        cm = cm.reshape(bsz, seq, B_GROUPS, B_STATE)
        dt = jax.nn.softplus(dt_raw.astype(f32).reshape(bsz, seq, N_DIR, B_HEADS) + dt_bias_b)
        b_args = tuple((xs, dt[:, :, d], bm, cm) for d in range(N_DIR))
        return (z_a, z_b, xs), a_args, b_args

    gates_c, a_args_c, b_args_c = project(h_ctx)
    gates_l, a_args_l, b_args_l = project(h_lat)
    bsz = h_lat.shape[0]
    a_state0 = jnp.zeros((bsz, A_HEADS, A_DK, A_DV), f32)
    b_state0 = jnp.zeros((bsz, B_HEADS, B_STATE, B_HEADDIM), f32)
    a_out_c, a_out_l = bidirectional_two_stream((gated_delta_scan, gated_delta_scan), a_args_c, a_args_l, a_state0)
    ssd_dirs = tuple(functools.partial(ssd_scan, a=-jnp.exp(a_log_b[d].astype(f32))) for d in range(N_DIR))
    b_out_c, b_out_l = bidirectional_two_stream(ssd_dirs, b_args_c, b_args_l, b_state0)

    def merge(a_out, b_out, gates):
        z_a, z_b, xs = gates
        bsz, seq = a_out.shape[:2]
        y_a = rms_norm(a_out, norm_a) * jax.nn.silu(z_a.astype(f32)).reshape(bsz, seq, A_HEADS, A_DV)
        y_b = (b_out + d_skip_b[:, None] * xs).reshape(bsz, seq, B_DINNER) * jax.nn.silu(z_b.astype(f32))
        y_b = rms_norm(y_b.reshape(bsz, seq, B_GROUPS, -1), norm_b.reshape(B_GROUPS, -1))
        y = jnp.concatenate([y_a.reshape(bsz, seq, -1), y_b.reshape(bsz, seq, -1)], -1)
        return y.astype(w_out.dtype) @ w_out

    y_lat = merge(a_out_l, b_out_l, gates_l)
    y_ctx = merge(a_out_c, b_out_c, gates_c) if need_ctx else None
    return y_ctx, y_lat


def mixer_odd(h_ctx, h_lat, need_ctx, *, rows, w_in, conv_c, i_bias, f_bias, norm_c, w_out):
    f32 = jnp.float32

    def project(h):
        bsz, seq, _ = h.shape
        qk, v, o, i_raw, f_raw = split_cols(h @ w_in, (C_QK, C_V, C_V, C_GATES, C_GATES))
        qk = jax.nn.silu(centred_conv(qk, conv_c).astype(f32))
        q, k = split_cols(qk, (C_HEADS * C_DK, C_HEADS * C_DK))
        q = q.reshape(bsz, seq, C_HEADS, C_DK) * C_DK ** -0.5
        k = k.reshape(bsz, seq, C_HEADS, C_DK)
        v = v.astype(f32).reshape(bsz, seq, C_HEADS, C_DV)
        log_i = i_raw.astype(f32).reshape(bsz, seq, N_DIR, C_HEADS) + i_bias
        log_f = jax.nn.log_sigmoid(f_raw.astype(f32).reshape(bsz, seq, N_DIR, C_HEADS) + f_bias)
        args = tuple((q, k, v, log_i[:, :, d], log_f[:, :, d]) for d in range(N_DIR))
        return o, args

    o_c, args_c = project(h_ctx)
    o_l, args_l = project(to_column_major(h_lat, rows))
    bsz = h_lat.shape[0]
    state0 = (jnp.zeros((bsz, C_HEADS, C_DK, C_DV), f32), jnp.zeros((bsz, C_HEADS, C_DK), f32),
              jnp.zeros((bsz, C_HEADS), f32))
    h_c, h_l = bidirectional_two_stream((mlstm_scan, mlstm_scan), args_c, args_l, state0)

    def merge(hh, o):
        bsz, seq = hh.shape[:2]
        y = rms_norm(hh, norm_c) * jax.nn.sigmoid(o.astype(f32)).reshape(bsz, seq, C_HEADS, C_DV)
        return y.reshape(bsz, seq, ODD_MIX).astype(w_out.dtype) @ w_out

    y_lat = to_row_major(merge(h_l, o_l), rows)
    y_ctx = merge(h_c, o_c) if need_ctx else None
    return y_ctx, y_lat


def peer(h, w_query, sub_keys, u_tab, v_tab):
    bsz, seq, d = h.shape
    tok = h.reshape(-1, PEER_BLOCK, d)

    def block(xb):
        q = (xb @ w_query).reshape(PEER_BLOCK, PEER_HEADS, 2, PEER_DKEY // 2)
        s = jnp.einsum('mhcd,hckd->mhck', q, sub_keys).astype(jnp.float32)
        s1, i1 = lax.top_k(s[:, :, 0], PEER_TOPK)
        s2, i2 = lax.top_k(s[:, :, 1], PEER_TOPK)
        cand = (s1[..., :, None] + s2[..., None, :]).reshape(PEER_BLOCK, PEER_HEADS, PEER_TOPK * PEER_TOPK)
        top_s, top_i = lax.top_k(cand, PEER_TOPK)
        expert = (jnp.take_along_axis(i1, top_i // PEER_TOPK, -1) * PEER_NKEYS
                  + jnp.take_along_axis(i2, top_i % PEER_TOPK, -1))
        gate = jax.nn.softmax(top_s, axis=-1)
        act = jax.nn.gelu(jnp.einsum('mhkd,md->mhk', u_tab[expert], xb).astype(jnp.float32), approximate=False)
        return jnp.einsum('mhk,mhkd->md', (gate * act).astype(xb.dtype), v_tab[expert])

    return lax.map(block, tok).reshape(bsz, seq, d)


def trunk_layer(x_ctx, x_lat, cond_ctx, cond_lat, mixer, ada_w, ada_b, ln1_g, ln1_b, ln2_g, ln2_b,
                peer_q, peer_keys, peer_u, peer_v, need_ctx):
    sh1_l, sc1_l, g1_l, sh2_l, sc2_l, g2_l = jnp.split((jax.nn.silu(cond_lat) @ ada_w + ada_b)[:, None, :], 6, axis=-1)
    sh1_c, sc1_c, g1_c, sh2_c, sc2_c, g2_c = jnp.split(jax.nn.silu(cond_ctx) @ ada_w + ada_b, 6, axis=-1)
    y_ctx, y_lat = mixer(modulate(x_ctx, sh1_c, sc1_c), modulate(x_lat, sh1_l, sc1_l), need_ctx)
    x_lat = layer_norm(ALPHA * x_lat + g1_l * y_lat, ln1_g, ln1_b)
    x_lat = layer_norm(ALPHA * x_lat + g2_l * peer(modulate(x_lat, sh2_l, sc2_l), peer_q, peer_keys, peer_u, peer_v),
                       ln2_g, ln2_b)
    if need_ctx:
        x_ctx = layer_norm(ALPHA * x_ctx + g1_c * y_ctx, ln1_g, ln1_b)
        x_ctx = layer_norm(ALPHA * x_ctx + g2_c * peer(modulate(x_ctx, sh2_c, sc2_c), peer_q, peer_keys, peer_u, peer_v),
                           ln2_g, ln2_b)
    return x_ctx, x_lat


def setup_inputs(seed: int = 0) -> dict:
    key = jax.random.key(seed)
    ks = jax.random.split(key, 64)
    counter = [0]
    f32 = jnp.float32

    def nxt():
        counter[0] += 1
        return ks[counter[0] - 1]

    def normal(shape, scale):
        return jax.random.normal(nxt(), shape, f32) * scale

    def gain(n):
        return 1.0 + normal((n,), 0.02)

    def a_log(shape):
        return jnp.log(jax.random.uniform(nxt(), shape, f32, 1.0, 16.0))

    def dt_bias(shape):
        dt = jnp.exp(jax.random.uniform(nxt(), shape, f32, math.log(1e-3), math.log(1e-1)))
        return dt + jnp.log(-jnp.expm1(-dt))

    d = D_MODEL
    fan = d ** -0.5
    inp = {}
    inp['x'] = normal((BATCH, SEQ, d), 1.0)
    inp['c'] = normal((BATCH, d), 1.0)
    inp['ctx'] = normal((BATCH, CTX_LEN, d), 1.0)
    inp['c_ctx'] = normal((d,), 1.0)
    inp['l0_ada_w'] = normal((d, 6 * d), fan)
    inp['l0_ada_b'] = normal((6 * d,), 0.02)
    inp['l0_ln1_g'] = gain(d)
    inp['l0_ln1_b'] = normal((d,), 0.02)
    inp['l0_w_in'] = normal((d, EVEN_IN), fan)
    inp['l0_conv_a'] = normal((CONV_W, A_QKV), CONV_W ** -0.5)
    inp['l0_a_log_a'] = a_log((N_DIR, A_HEADS))
    inp['l0_dt_bias_a'] = dt_bias((N_DIR, A_HEADS))
    inp['l0_norm_a'] = gain(A_DV)
    inp['l0_conv_b'] = normal((CONV_W, B_XBC), CONV_W ** -0.5)
    inp['l0_conv_b_bias'] = normal((B_XBC,), 0.02)
    inp['l0_a_log_b'] = a_log((N_DIR, B_HEADS))
    inp['l0_dt_bias_b'] = dt_bias((N_DIR, B_HEADS))
    inp['l0_d_skip_b'] = gain(B_HEADS)
    inp['l0_norm_b'] = gain(B_DINNER)
    inp['l0_w_out'] = normal((EVEN_MIX, d), BETA * EVEN_MIX ** -0.5)
    inp['l0_ln2_g'] = gain(d)
    inp['l0_ln2_b'] = normal((d,), 0.02)
    inp['l0_peer_q'] = normal((d, PEER_HEADS * PEER_DKEY), fan)
    inp['l0_peer_keys'] = normal((PEER_HEADS, 2, PEER_NKEYS, PEER_DKEY // 2), (PEER_DKEY // 2) ** -0.5)
    inp['l0_peer_u'] = normal((PEER_EXPERTS, d), fan)
    inp['l0_peer_v'] = normal((PEER_EXPERTS, d), BETA)
    inp['l1_ada_w'] = normal((d, 6 * d), fan)
    inp['l1_ada_b'] = normal((6 * d,), 0.02)
    inp['l1_ln1_g'] = gain(d)
    inp['l1_ln1_b'] = normal((d,), 0.02)
    inp['l1_w_in'] = normal((d, ODD_IN), fan)
    inp['l1_conv_c'] = normal((CONV_W, C_QK), CONV_W ** -0.5)
    inp['l1_i_bias'] = normal((N_DIR, C_HEADS), 0.1)
    inp['l1_f_bias'] = 3.0 + normal((N_DIR, C_HEADS), 0.5)
    inp['l1_norm_c'] = gain(C_DV)
    inp['l1_w_out'] = normal((ODD_MIX, d), BETA * ODD_MIX ** -0.5)
    inp['l1_ln2_g'] = gain(d)
    inp['l1_ln2_b'] = normal((d,), 0.02)
    inp['l1_peer_q'] = normal((d, PEER_HEADS * PEER_DKEY), fan)
    inp['l1_peer_keys'] = normal((PEER_HEADS, 2, PEER_NKEYS, PEER_DKEY // 2), (PEER_DKEY // 2) ** -0.5)
    inp['l1_peer_u'] = normal((PEER_EXPERTS, d), fan)
    inp['l1_peer_v'] = normal((PEER_EXPERTS, d), BETA)
    return inp


def reference(x, c, ctx, c_ctx,
              l0_ada_w, l0_ada_b, l0_ln1_g, l0_ln1_b, l0_w_in, l0_conv_a, l0_a_log_a, l0_dt_bias_a, l0_norm_a,
              l0_conv_b, l0_conv_b_bias, l0_a_log_b, l0_dt_bias_b, l0_d_skip_b, l0_norm_b, l0_w_out,
              l0_ln2_g, l0_ln2_b, l0_peer_q, l0_peer_keys, l0_peer_u, l0_peer_v,
              l1_ada_w, l1_ada_b, l1_ln1_g, l1_ln1_b, l1_w_in, l1_conv_c, l1_i_bias, l1_f_bias, l1_norm_c, l1_w_out,
              l1_ln2_g, l1_ln2_b, l1_peer_q, l1_peer_keys, l1_peer_u, l1_peer_v):
    n_tokens = x.shape[1]
    rows = n_tokens // GRID_W
    x_lat = x + grid_sincos(n_tokens, x.shape[2]).astype(x.dtype)
    x_ctx = ctx
    mixers = (
        functools.partial(mixer_even, w_in=l0_w_in, conv_a=l0_conv_a, a_log_a=l0_a_log_a, dt_bias_a=l0_dt_bias_a,
                          norm_a=l0_norm_a, conv_b=l0_conv_b, conv_b_bias=l0_conv_b_bias, a_log_b=l0_a_log_b,
                          dt_bias_b=l0_dt_bias_b, d_skip_b=l0_d_skip_b, norm_b=l0_norm_b, w_out=l0_w_out),
        functools.partial(mixer_odd, rows=rows, w_in=l1_w_in, conv_c=l1_conv_c, i_bias=l1_i_bias, f_bias=l1_f_bias,
                          norm_c=l1_norm_c, w_out=l1_w_out),
    )
    layer_params = (
        (l0_ada_w, l0_ada_b, l0_ln1_g, l0_ln1_b, l0_ln2_g, l0_ln2_b, l0_peer_q, l0_peer_keys, l0_peer_u, l0_peer_v),
        (l1_ada_w, l1_ada_b, l1_ln1_g, l1_ln1_b, l1_ln2_g, l1_ln2_b, l1_peer_q, l1_peer_keys, l1_peer_u, l1_peer_v),
    )
    for layer in range(DEPTH):
        x_ctx, x_lat = trunk_layer(x_ctx, x_lat, c_ctx, c, mixers[layer], *layer_params[layer],
                                   need_ctx=layer < DEPTH - 1)
    return x_lat
```

```python
import functools
import math

import numpy as np
import jax
import jax.numpy as jnp
from jax import lax
from jax.experimental import pallas as pl
from jax.experimental.pallas import tpu as pltpu

f32 = jnp.float32
bf16 = jnp.bfloat16

D_MODEL = 1024
GRID_W = 64
CHUNK = 64
CONV_W = 5
POS_BASE = 10000.0
EPS = 1e-6
N_DIR = 2
DEPTH = 2
ALPHA = (2 * DEPTH) ** 0.25

A_HEADS, A_DK, A_DV = 4, 128, 128
B_HEADS, B_HEADDIM, B_GROUPS, B_STATE = 8, 64, 2, 128
C_HEADS, C_DK, C_DV = 4, 128, 256
PEER_HEADS, PEER_NKEYS, PEER_DKEY, PEER_TOPK = 8, 128, 256, 16

LANE = 128
ROW_TILE = 256
EXPERT_TOK = 512
EXPERT_BLK = 1024
VMEM_LIMIT = 56 * 1024 * 1024
N_CAND_ROWS = -(-sum(PEER_TOPK // (p + 1) for p in range(PEER_TOPK)) // 8) * 8

NN = (((1,), (0,)), ((), ()))
NT = (((1,), (1,)), ((), ()))


def _params(*sem):
    return pltpu.CompilerParams(dimension_semantics=sem, vmem_limit_bytes=VMEM_LIMIT)


def _dot(a, b, dims=NN):
    return lax.dot_general(a, b, dims, preferred_element_type=f32)


def _split(a):
    hi = a.astype(bf16)
    return hi, (a - hi.astype(f32)).astype(bf16)


def _dot3(a, b, dims=NN):
    ah, al = _split(a)
    bh, bl = _split(b)
    return _dot(ah, bh, dims) + (_dot(ah, bl, dims) + _dot(al, bh, dims))


def _sigmoid(x):
    return 1.0 / (1.0 + jnp.exp(-x))


def _silu(x):
    return x * _sigmoid(x)


def _softplus(x):
    return jnp.maximum(x, 0.0) + jnp.log1p(jnp.exp(-jnp.abs(x)))


def _layer_norm(z, g, b):
    mu = jnp.mean(z, -1, keepdims=True)
    zc = z - mu
    var = jnp.mean(zc * zc, -1, keepdims=True)
    return zc * lax.rsqrt(var + EPS) * g + b


def _rms(x, g):
    return x * lax.rsqrt(jnp.mean(x * x, -1, keepdims=True) + EPS) * g


def _chunk_masks(d):
    row = lax.broadcasted_iota(jnp.int32, (CHUNK, CHUNK), 0)
    col = lax.broadcasted_iota(jnp.int32, (CHUNK, CHUNK), 1)
    diff = jnp.where(d == 0, row - col, col - row)
    return diff >= 0, diff > 0, diff <= 0, row == col


def _cum(mask_incl, mask_incl_t, x_col, x_row):
    c_col = jnp.sum(jnp.where(mask_incl, x_row, 0.0), axis=1, keepdims=True)
    c_row = jnp.sum(jnp.where(mask_incl_t, x_col, 0.0), axis=0, keepdims=True)
    tot = jnp.sum(x_col, axis=0, keepdims=True)
    return c_col, c_row, tot


def _ada_kernel(c_ref, w_ref, b_ref, o_ref):
    o_ref[...] = _dot3(_silu(c_ref[...]), w_ref[...]) + b_ref[...]


def _ada(cond, w, b):
    rows, d = cond.shape
    n = w.shape[1]
    return pl.pallas_call(
        _ada_kernel,
        out_shape=jax.ShapeDtypeStruct((rows, n), f32),
        grid=(n // d,),
        in_specs=[pl.BlockSpec((rows, d), lambda j: (0, 0)),
                  pl.BlockSpec((d, d), lambda j: (0, j)),
                  pl.BlockSpec((1, d), lambda j: (0, j))],
        out_specs=pl.BlockSpec((rows, d), lambda j: (0, j)),
        compiler_params=_params("parallel"),
    )(cond, w, b.reshape(1, n))


def _proj_kernel(x_ref, sh_ref, sc_ref, w_ref, o_ref):
    xm = x_ref[...] * (1.0 + sc_ref[0]) + sh_ref[0]
    o_ref[...] = _dot(xm.astype(bf16), w_ref[...])


def _proj(x, shift_t, scale_t, w):
    rows, d = x.shape
    n = w.shape[1]
    return pl.pallas_call(
        _proj_kernel,
        out_shape=jax.ShapeDtypeStruct((rows, n), f32),
        grid=(rows // ROW_TILE,),
        in_specs=[pl.BlockSpec((ROW_TILE, d), lambda i: (i, 0)),
                  pl.BlockSpec((1, 1, d), lambda i: (i, 0, 0)),
                  pl.BlockSpec((1, 1, d), lambda i: (i, 0, 0)),
                  pl.BlockSpec((d, n), lambda i: (0, 0))],
        out_specs=pl.BlockSpec((ROW_TILE, n), lambda i: (i, 0)),
        compiler_params=_params("parallel"),
    )(x, shift_t, scale_t, w)


def _conv_kernel(x_ref, w_ref, b_ref, o_ref, pad_ref, *, seq, seg0, n_norm, n_qscale, qscale):
    j = pl.program_id(1)
    halo = 8
    pad_ref[0:halo, :] = jnp.zeros((halo, LANE), f32)
    pad_ref[halo + seq:2 * halo + seq, :] = jnp.zeros((halo, LANE), f32)
    pad_ref[halo:halo + seq, :] = x_ref[0]
    w = w_ref[...]
    bias = b_ref[...]
    scale = jnp.where(j < n_qscale, qscale, 1.0).astype(f32)
    norm_on = jnp.where(j < n_norm, 1.0, 0.0).astype(f32)
    rows = ROW_TILE

    def body(c, carry):
        r0 = pl.multiple_of(c * rows, rows)
        win = pad_ref[pl.ds(r0, rows + 2 * halo), :]
        t = r0 + lax.broadcasted_iota(jnp.int32, (rows, LANE), 0)
        acc = bias + w[2:3] * win[halo:halo + rows]
        for k in (0, 1, 3, 4):
            off = k - CONV_W // 2
            tap = win[halo + off:halo + off + rows]
            same_segment = ((t + off) >= seg0) == (t >= seg0)
            acc = acc + w[k:k + 1] * jnp.where(same_segment, tap, 0.0)
        y = _silu(acc)
        yn = y * lax.rsqrt(jnp.sum(y * y, -1, keepdims=True) + EPS)
        y = (norm_on * yn + (1.0 - norm_on) * y) * scale
        o_ref[0, pl.ds(r0, rows), :] = y
        return carry

    lax.fori_loop(0, seq // rows, body, 0)


def _conv_act(x, w, b, col_blocks, *, seg0, n_norm, n_qscale, qscale):
    bsz, seq, _ = x.shape
    n_out = len(col_blocks)
    first_gap = next((i for i, cb in enumerate(col_blocks) if cb != i), n_out)
    gap = col_blocks[first_gap] - first_gap if first_gap < n_out else 0
    in_map = lambda bi, j: (bi, 0, jnp.where(j < first_gap, j, j + gap))
    kern = functools.partial(_conv_kernel, seq=seq, seg0=seg0, n_norm=n_norm, n_qscale=n_qscale, qscale=qscale)
    return pl.pallas_call(
        kern,
        out_shape=jax.ShapeDtypeStruct((bsz, seq, n_out * LANE), f32),
        grid=(bsz, n_out),
        in_specs=[pl.BlockSpec((1, seq, LANE), in_map),
                  pl.BlockSpec((CONV_W, LANE), lambda bi, j: (0, j)),
                  pl.BlockSpec((1, LANE), lambda bi, j: (0, j))],
        out_specs=pl.BlockSpec((1, seq, LANE), lambda bi, j: (bi, 0, j)),
        scratch_shapes=[pltpu.VMEM((seq + 16, LANE), f32)],
        compiler_params=_params("parallel", "parallel"),
    )(x, w, b)


def _chunk_index(d, n, n_ctx, n_all):
    rev = jnp.where(n < n_ctx, n_ctx - 1 - n, n_all + n_ctx - 1 - n)
    return jnp.where(d == 0, n, rev)


def _gate_layouts(raw, heads):
    bsz, seq, _ = raw.shape
    r = raw.reshape(bsz, seq // CHUNK, CHUNK, N_DIR, heads)
    return r.transpose(3, 0, 1, 2, 4), r.transpose(3, 0, 1, 4, 2)


def _param_layouts(*ps):
    return jnp.stack(ps, axis=1).astype(f32), jnp.stack(ps, axis=2).astype(f32)


def _gdn_kernel(q_ref, k_ref, v_ref, bcol_ref, brow_ref, dcol_ref, drow_ref, pcol_ref, prow_ref,
                o_ref, st_ref, *, bsz):
    d = pl.program_id(0)
    n = pl.program_id(1)

    @pl.when(n == 0)
    def _():
        st_ref[...] = jnp.zeros(st_ref.shape, f32)

    m_incl, m_strict, m_incl_t, m_eye = _chunk_masks(d)
    eye = m_eye.astype(f32)
    row = lax.broadcasted_iota(jnp.int32, (CHUNK, CHUNK), 0)
    col = lax.broadcasted_iota(jnp.int32, (CHUNK, CHUNK), 1)
    blocks = [(row >> sh) == (col >> sh) for sh in (3, 4, 5, 6)]
    pc = pcol_ref[0]
    pr = prow_ref[0]
    for b in range(bsz):
        beta_col = _sigmoid(bcol_ref[0, b, 0])
        g_col = -jnp.exp(pc[0:1, :]) * _softplus(dcol_ref[0, b, 0] + pc[1:2, :])
        g_row = -jnp.exp(pr[:, 0:1]) * _softplus(drow_ref[0, b, 0] + pr[:, 1:2])
        for h in range(A_HEADS):
            sl = slice(h * A_DK, (h + 1) * A_DK)
            q = q_ref[b, :, sl]
            k = k_ref[b, :, sl]
            v = v_ref[b, :, sl]
            gc_col, gc_row, g_tot = _cum(m_incl, m_incl_t, g_col[:, h:h + 1], g_row[h:h + 1, :])
            decay = jnp.exp(jnp.where(m_incl, gc_col - gc_row, -jnp.inf))
            b_col = beta_col[:, h:h + 1]
            kb = k * b_col
            l_mat = _dot3(kb, k, NT) * jnp.where(m_strict, decay, 0.0)
            nil = -jnp.where(blocks[0], l_mat, 0.0)
            t_inv = eye + nil
            for _ in range(2):
                nil = _dot3(nil, nil)
                t_inv = t_inv + _dot3(t_inv, nil)
            for lvl in range(3):
                off = jnp.where(blocks[lvl + 1] & ~blocks[lvl], l_mat, 0.0)
                t_inv = t_inv - _dot3(_dot3(t_inv, off), t_inv)
            e_col = jnp.exp(gc_col)
            uw = _dot3(t_inv, jnp.concatenate([v * b_col, kb * e_col], axis=1))
            u, w = uw[:, :A_DV], uw[:, A_DV:]
            qk = _dot(q, k, NT) * decay
            k_dec = k * jnp.exp(g_tot - gc_col)
            s = st_ref[b * A_HEADS + h]
            wq = _dot(jnp.concatenate([w, q * e_col], axis=0), s)
            v_new = u - wq[:CHUNK]
            o_ref[0, b, :, sl] = wq[CHUNK:] + _dot(qk, v_new)
            st_ref[b * A_HEADS + h] = jnp.exp(g_tot) * s + _dot(k_dec.T, v_new)


def _gdn(act, beta_raw, decay_raw, a_log, dt_bias, n_ctx):
    bsz, seq, _ = act.shape
    n_all = seq // CHUNK
    hw = A_HEADS * A_DK
    bcol, brow = _gate_layouts(beta_raw, A_HEADS)
    dcol, drow = _gate_layouts(decay_raw, A_HEADS)
    pcol, prow = _param_layouts(a_log, dt_bias)
    ci = functools.partial(_chunk_index, n_ctx=n_ctx, n_all=n_all)
    tok = lambda blk: pl.BlockSpec((bsz, CHUNK, hw), lambda d, n: (0, ci(d, n), blk))
    gcol = pl.BlockSpec((1, bsz, 1, CHUNK, A_HEADS), lambda d, n: (d, 0, ci(d, n), 0, 0))
    grow = pl.BlockSpec((1, bsz, 1, A_HEADS, CHUNK), lambda d, n: (d, 0, ci(d, n), 0, 0))
    return pl.pallas_call(
        functools.partial(_gdn_kernel, bsz=bsz),
        out_shape=jax.ShapeDtypeStruct((N_DIR, bsz, seq, hw), f32),
        grid=(N_DIR, n_all),
        in_specs=[tok(0), tok(1), tok(2), gcol, grow, gcol, grow,
                  pl.BlockSpec((1, 2, A_HEADS), lambda d, n: (d, 0, 0)),
                  pl.BlockSpec((1, A_HEADS, 2), lambda d, n: (d, 0, 0))],
        out_specs=pl.BlockSpec((1, bsz, CHUNK, hw), lambda d, n: (d, 0, ci(d, n), 0)),
        scratch_shapes=[pltpu.VMEM((bsz * A_HEADS, A_DK, A_DV), f32)],
        compiler_params=_params("arbitrary", "arbitrary"),
    )(act, act, act, bcol, brow, dcol, drow, pcol, prow)


def _ssd_kernel(x_ref, bm_ref, cm_ref, tcol_ref, trow_ref, pcol_ref, prow_ref, o_ref, st_ref, *, bsz):
    d = pl.program_id(0)
    n = pl.program_id(1)

    @pl.when(n == 0)
    def _():
        st_ref[...] = jnp.zeros(st_ref.shape, f32)

    m_incl, _, m_incl_t, _ = _chunk_masks(d)
    pc = pcol_ref[0]
    pr = prow_ref[0]
    rep = B_HEADS // B_GROUPS
    for b in range(bsz):
        dt_col = _softplus(tcol_ref[0, b, 0] + pc[1:2, :])
        dt_row = _softplus(trow_ref[0, b, 0] + pr[:, 1:2])
        da_col = dt_col * (-jnp.exp(pc[0:1, :]))
        da_row = dt_row * (-jnp.exp(pr[:, 0:1]))
        for g in range(B_GROUPS):
            bm = bm_ref[b, :, g * B_STATE:(g + 1) * B_STATE]
            cm = cm_ref[b, :, g * B_STATE:(g + 1) * B_STATE]
            cb = _dot(cm, bm, NT)
            hs = st_ref[b * B_GROUPS + g]
            y_off = _dot(cm, hs)
            xdt_end, keep = [], []
            for r in range(rep):
                h = g * rep + r
                hsl = slice(h * B_HEADDIM, (h + 1) * B_HEADDIM)
                a_col, a_row, a_tot = _cum(m_incl, m_incl_t, da_col[:, h:h + 1], da_row[h:h + 1, :])
                seg = jnp.exp(jnp.where(m_incl, a_col - a_row, -jnp.inf))
                xdt = x_ref[b, :, hsl] * dt_col[:, h:h + 1]
                o_ref[0, b, :, hsl] = (_dot(cb * seg, xdt)
                                       + jnp.exp(a_col) * y_off[:, r * B_HEADDIM:(r + 1) * B_HEADDIM])
                xdt_end.append(xdt * jnp.exp(a_tot - a_col))
                keep.append(jnp.broadcast_to(jnp.exp(a_tot), (1, B_HEADDIM)))
            st_ref[b * B_GROUPS + g] = (jnp.concatenate(keep, axis=1) * hs
                                        + _dot(bm.T, jnp.concatenate(xdt_end, axis=1)))


def _ssd(act, dt_raw, a_log, dt_bias, n_ctx):
    bsz, seq, _ = act.shape
    n_all = seq // CHUNK
    dinner = B_HEADS * B_HEADDIM
    gw = B_GROUPS * B_STATE
    tcol, trow = _gate_layouts(dt_raw, B_HEADS)
    pcol, prow = _param_layouts(a_log, dt_bias)
    ci = functools.partial(_chunk_index, n_ctx=n_ctx, n_all=n_all)
    x_off = (A_HEADS * (2 * A_DK + A_DV)) // dinner
    bm_off = (A_HEADS * (2 * A_DK + A_DV) + dinner) // gw
    return pl.pallas_call(
        functools.partial(_ssd_kernel, bsz=bsz),
        out_shape=jax.ShapeDtypeStruct((N_DIR, bsz, seq, dinner), f32),
        grid=(N_DIR, n_all),
        in_specs=[pl.BlockSpec((bsz, CHUNK, dinner), lambda d, n: (0, ci(d, n), x_off)),
                  pl.BlockSpec((bsz, CHUNK, gw), lambda d, n: (0, ci(d, n), bm_off)),
                  pl.BlockSpec((bsz, CHUNK, gw), lambda d, n: (0, ci(d, n), bm_off + 1)),
                  pl.BlockSpec((1, bsz, 1, CHUNK, B_HEADS), lambda d, n: (d, 0, ci(d, n), 0, 0)),
                  pl.BlockSpec((1, bsz, 1, B_HEADS, CHUNK), lambda d, n: (d, 0, ci(d, n), 0, 0)),
                  pl.BlockSpec((1, 2, B_HEADS), lambda d, n: (d, 0, 0)),
                  pl.BlockSpec((1, B_HEADS, 2), lambda d, n: (d, 0, 0))],
        out_specs=pl.BlockSpec((1, bsz, CHUNK, dinner), lambda d, n: (d, 0, ci(d, n), 0)),
        scratch_shapes=[pltpu.VMEM((bsz * B_GROUPS, B_STATE, (B_HEADS // B_GROUPS) * B_HEADDIM), f32)],
        compiler_params=_params("arbitrary", "arbitrary"),
    )(act, act, act, tcol, trow, pcol, prow)


def _mlstm_kernel(q_ref, k_ref, v_ref, icol_ref, irow_ref, fcol_ref, frow_ref, pcol_ref, prow_ref,
                  o_ref, c_ref, n_ref, m_ref, *, bsz):
    d = pl.program_id(0)
    n = pl.program_id(1)

    @pl.when(n == 0)
    def _():
        c_ref[...] = jnp.zeros(c_ref.shape, f32)
        n_ref[...] = jnp.zeros(n_ref.shape, f32)
        m_ref[...] = jnp.zeros(m_ref.shape, f32)

    m_incl, _, m_incl_t, _ = _chunk_masks(d)
    pc = pcol_ref[0]
    pr = prow_ref[0]
    for b in range(bsz):
        li_col = icol_ref[0, b, 0] + pc[0:1, :]
        li_row = irow_ref[0, b, 0] + pr[:, 0:1]
        lf_col = -_softplus(-(fcol_ref[0, b, 0] + pc[1:2, :]))
        lf_row = -_softplus(-(frow_ref[0, b, 0] + pr[:, 1:2]))
        for h in range(C_HEADS):
            ch = b * C_HEADS + h
            q = q_ref[b, :, h * C_DK:(h + 1) * C_DK]
            k = k_ref[b, :, h * C_DK:(h + 1) * C_DK]
            v = v_ref[b, :, h * C_DV:(h + 1) * C_DV]
            b_col, b_row, b_tot = _cum(m_incl, m_incl_t, lf_col[:, h:h + 1], lf_row[h:h + 1, :])
            i_col = li_col[:, h:h + 1]
            logw_intra = jnp.where(m_incl, b_col - b_row + li_row[h:h + 1, :], -jnp.inf)
            m_prev = m_ref[ch]
            logw_inter = b_col + m_prev
            m_t = jnp.maximum(logw_inter, jnp.max(logw_intra, axis=1, keepdims=True))
            s = _dot(q, k, NT) * jnp.exp(logw_intra - m_t)
            w_inter = jnp.exp(logw_inter - m_t)
            c_mem = c_ref[ch]
            n_mem = n_ref[ch]
            num = _dot(s, v) + w_inter * _dot(q, c_mem)
            den = jnp.sum(s, axis=1, keepdims=True) + w_inter * jnp.sum(q * n_mem, axis=1, keepdims=True)
            o_ref[0, b, :, h * C_DV:(h + 1) * C_DV] = num / jnp.maximum(jnp.abs(den), jnp.exp(-m_t))
            logw_end = b_tot - b_col + i_col
            m_new = jnp.maximum(b_tot + m_prev, jnp.max(logw_end, axis=0, keepdims=True))
            kw = k * jnp.exp(logw_end - m_new)
            keep = jnp.exp(b_tot + m_prev - m_new)
            c_ref[ch] = keep * c_mem + _dot(kw.T, v)
            n_ref[ch] = keep * n_mem + jnp.sum(kw, axis=0, keepdims=True)
            m_ref[ch] = m_new


def _mlstm(act, proj, i_raw, f_raw, i_bias, f_bias, n_ctx):
    bsz, seq, _ = act.shape
    n_all = seq // CHUNK
    qw = C_HEADS * C_DK
    vw = C_HEADS * C_DV
    icol, irow = _gate_layouts(i_raw, C_HEADS)
    fcol, frow = _gate_layouts(f_raw, C_HEADS)
    pcol, prow = _param_layouts(i_bias, f_bias)
    ci = functools.partial(_chunk_index, n_ctx=n_ctx, n_all=n_all)
    gcol = pl.BlockSpec((1, bsz, 1, CHUNK, C_HEADS), lambda d, n: (d, 0, ci(d, n), 0, 0))
    grow = pl.BlockSpec((1, bsz, 1, C_HEADS, CHUNK), lambda d, n: (d, 0, ci(d, n), 0, 0))
    return pl.pallas_call(
        functools.partial(_mlstm_kernel, bsz=bsz),
        out_shape=jax.ShapeDtypeStruct((N_DIR, bsz, seq, vw), f32),
        grid=(N_DIR, n_all),
        in_specs=[pl.BlockSpec((bsz, CHUNK, qw), lambda d, n: (0, ci(d, n), 0)),
                  pl.BlockSpec((bsz, CHUNK, qw), lambda d, n: (0, ci(d, n), 1)),
                  pl.BlockSpec((bsz, CHUNK, vw), lambda d, n: (0, ci(d, n), 2 * qw // vw)),
                  gcol, grow, gcol, grow,
                  pl.BlockSpec((1, 2, C_HEADS), lambda d, n: (d, 0, 0)),
                  pl.BlockSpec((1, C_HEADS, 2), lambda d, n: (d, 0, 0))],
        out_specs=pl.BlockSpec((1, bsz, CHUNK, vw), lambda d, n: (d, 0, ci(d, n), 0)),
        scratch_shapes=[pltpu.VMEM((bsz * C_HEADS, C_DK, C_DV), f32),
                        pltpu.VMEM((bsz * C_HEADS, 1, C_DK), f32),
                        pltpu.VMEM((bsz * C_HEADS, 1, 1), f32)],
        compiler_params=_params("arbitrary", "arbitrary"),
    )(act, act, proj, icol, irow, fcol, frow, pcol, prow)


def _merge_even_kernel(x_ref, a_ref, b_ref, za_ref, zb_ref, xs_ref, na_ref, nb_ref, dsk_ref, w_ref,
                       g_ref, lg_ref, lb_ref, o_ref):
    a = a_ref[0] + a_ref[1]
    za = za_ref[...]
    na = na_ref[...]
    parts = []
    for h in range(A_HEADS):
        sl = slice(h * A_DV, (h + 1) * A_DV)
        parts.append(_rms(a[:, sl], na) * _silu(za[:, sl]))
    yb = (b_ref[0] + b_ref[1] + dsk_ref[...] * xs_ref[...]) * _silu(zb_ref[...])
    gw = (B_HEADS * B_HEADDIM) // B_GROUPS
    nb = nb_ref[...]
    for g in range(B_GROUPS):
        sl = slice(g * gw, (g + 1) * gw)
        parts.append(_rms(yb[:, sl], nb[:, sl]))
    y = jnp.concatenate(parts, axis=1).astype(bf16)
    z = ALPHA * x_ref[...] + g_ref[0] * _dot(y, w_ref[...])
    o_ref[...] = _layer_norm(z, lg_ref[...], lb_ref[...])


def _merge_even(x, o_a, o_b, proj, act, norm_a, norm_b, d_skip, w_out, gate_t, ln_g, ln_b):
    rows, d = x.shape
    aw = A_HEADS * A_DV
    bw = B_HEADS * B_HEADDIM
    row = lambda w_, blk: pl.BlockSpec((ROW_TILE, w_), lambda i: (i, blk))
    vec = lambda w_: pl.BlockSpec((1, w_), lambda i: (0, 0))
    qkv = A_HEADS * (2 * A_DK + A_DV)
    return pl.pallas_call(
        _merge_even_kernel,
        out_shape=jax.ShapeDtypeStruct((rows, d), f32),
        grid=(rows // ROW_TILE,),
        in_specs=[row(d, 0),
                  pl.BlockSpec((N_DIR, ROW_TILE, aw), lambda i: (0, i, 0)),
                  pl.BlockSpec((N_DIR, ROW_TILE, bw), lambda i: (0, i, 0)),
                  row(aw, qkv // aw), row(bw, (qkv + aw) // bw), row(bw, qkv // bw),
                  vec(A_DV), vec(bw), vec(bw),
                  pl.BlockSpec((aw + bw, d), lambda i: (0, 0)),
                  pl.BlockSpec((1, 1, d), lambda i: (i, 0, 0)),
                  vec(d), vec(d)],
        out_specs=row(d, 0),
        compiler_params=_params("parallel"),
    )(x, o_a, o_b, proj, proj, act, norm_a.reshape(1, -1), norm_b.reshape(1, -1),
      jnp.repeat(d_skip, B_HEADDIM).reshape(1, -1), w_out, gate_t, ln_g.reshape(1, -1), ln_b.reshape(1, -1))


def _merge_odd_kernel(h_ref, o_ref_in, nc_ref, w_ref, out_ref):
    hsum = h_ref[0] + h_ref[1]
    o = o_ref_in[...]
    nc = nc_ref[...]
    parts = []
    for h in range(C_HEADS):
        sl = slice(h * C_DV, (h + 1) * C_DV)
        parts.append(_rms(hsum[:, sl], nc) * _sigmoid(o[:, sl]))
    y = jnp.concatenate(parts, axis=1).astype(bf16)
    out_ref[...] = _dot(y, w_ref[...])


def _merge_odd(h, proj, norm_c, w_out, tiles_per_batch, ctx_tiles, bsz):
    vw = C_HEADS * C_DV
    lat_tiles = tiles_per_batch - ctx_tiles
    src = lambda i: i + ctx_tiles * (i // lat_tiles + 1)
    d = w_out.shape[1]
    return pl.pallas_call(
        _merge_odd_kernel,
        out_shape=jax.ShapeDtypeStruct((bsz * lat_tiles * ROW_TILE, d), f32),
        grid=(bsz * lat_tiles,),
        in_specs=[pl.BlockSpec((N_DIR, ROW_TILE, vw), lambda i: (0, src(i), 0)),
                  pl.BlockSpec((ROW_TILE, vw), lambda i: (src(i), 2 * C_HEADS * C_DK // vw + 1)),
                  pl.BlockSpec((1, C_DV), lambda i: (0, 0)),
                  pl.BlockSpec((vw, d), lambda i: (0, 0))],
        out_specs=pl.BlockSpec((ROW_TILE, d), lambda i: (i, 0)),
        compiler_params=_params("parallel"),
    )(h, proj, norm_c.reshape(1, -1), w_out)


def _resid_ln_kernel(x_ref, y_ref, g_ref, lg_ref, lb_ref, o_ref, *, transposed):
    y = y_ref[...]
    if transposed:
        y = y.T
    o_ref[...] = _layer_norm(ALPHA * x_ref[...] + g_ref[0] * y, lg_ref[...], lb_ref[...])


def _resid_ln(x, y, gate_t, ln_g, ln_b, *, transposed):
    rows, d = x.shape
    y_spec = (pl.BlockSpec((d, ROW_TILE), lambda i: (0, i)) if transposed
              else pl.BlockSpec((ROW_TILE, d), lambda i: (i, 0)))
    return pl.pallas_call(
        functools.partial(_resid_ln_kernel, transposed=transposed),
        out_shape=jax.ShapeDtypeStruct((rows, d), f32),
        grid=(rows // ROW_TILE,),
        in_specs=[pl.BlockSpec((ROW_TILE, d), lambda i: (i, 0)), y_spec,
                  pl.BlockSpec((1, 1, d), lambda i: (i, 0, 0)),
                  pl.BlockSpec((1, d), lambda i: (0, 0)),
                  pl.BlockSpec((1, d), lambda i: (0, 0))],
        out_specs=pl.BlockSpec((ROW_TILE, d), lambda i: (i, 0)),
        compiler_params=_params("parallel"),
    )(x, y, gate_t, ln_g.reshape(1, -1), ln_b.reshape(1, -1))


def _top_values(s, count):
    rows = float(s.shape[0])
    idx = lax.broadcasted_iota(jnp.int32, s.shape, 0).astype(f32)
    vals = []
    for _ in range(count):
        m = jnp.max(s, axis=0, keepdims=True)
        first = jnp.min(jnp.where(s == m, idx, rows), axis=0, keepdims=True)
        vals.append(m)
        s = jnp.where(idx == first, -jnp.inf, s)
    return vals


def _route_kernel(x_ref, sh_ref, sc_ref, wh_ref, wl_ref, kh_ref, kl_ref,
                  xm_ref, s1_ref, s2_ref, e1_ref, e2_ref, tau_ref, cand_ref):
    xm = x_ref[...] * (1.0 + sc_ref[0]) + sh_ref[0]
    xm_ref[...] = xm.astype(bf16)
    xh, xl = _split(xm)
    half = PEER_DKEY // 2
    for h in range(PEER_HEADS):
        wsl = slice(h * PEER_DKEY, (h + 1) * PEER_DKEY)
        q = _dot(xh, wh_ref[:, wsl]) + (_dot(xh, wl_ref[:, wsl]) + _dot(xl, wh_ref[:, wsl]))
        tops = []
        for c, s_ref in ((0, s1_ref), (1, s2_ref)):
            qh, ql = _split(q[:, c * half:(c + 1) * half])
            kh = kh_ref[h, c]
            st = _dot(kh, qh, NT) + (_dot(kl_ref[h, c], qh, NT) + _dot(kh, ql, NT))
            s_ref[h] = st
            tops.append(_top_values(st, PEER_TOPK))
        t1, t2 = tops
        pairs = [(p, r) for p in range(PEER_TOPK) for r in range(PEER_TOPK) if (p + 1) * (r + 1) <= PEER_TOPK]
        cand_ref[...] = jnp.full(cand_ref.shape, -jnp.inf, f32)
        for i, (p, r) in enumerate(pairs):
            cand_ref[i:i + 1, :] = t1[p] + t2[r]
        best = _top_values(cand_ref[...], PEER_TOPK)
        z = sum(jnp.exp(bv - best[0]) for bv in best)
        tau_ref[h:h + 1, :] = best[-1]
        e1_ref[h] = jnp.exp(s1_ref[h] - t1[0]) / z
        e2_ref[h] = jnp.exp(s2_ref[h] - t2[0])


def _route(x, shift_t, scale_t, wq_hi, wq_lo, keys_hi, keys_lo):
    rows, d = x.shape
    nq = wq_hi.shape[1]
    half = PEER_DKEY // 2
    score = jax.ShapeDtypeStruct((PEER_HEADS, PEER_NKEYS, rows), f32)
    score_spec = pl.BlockSpec((PEER_HEADS, PEER_NKEYS, ROW_TILE), lambda i: (0, 0, i))
    key_spec = pl.BlockSpec((PEER_HEADS, 2, PEER_NKEYS, half), lambda i: (0, 0, 0, 0))
    return pl.pallas_call(
        _route_kernel,
        out_shape=(jax.ShapeDtypeStruct((rows, d), bf16), score, score, score, score,
                   jax.ShapeDtypeStruct((PEER_HEADS, rows), f32)),
        grid=(rows // ROW_TILE,),
        in_specs=[pl.BlockSpec((ROW_TILE, d), lambda i: (i, 0)),
                  pl.BlockSpec((1, 1, d), lambda i: (i, 0, 0)),
                  pl.BlockSpec((1, 1, d), lambda i: (i, 0, 0)),
                  pl.BlockSpec((d, nq), lambda i: (0, 0)),
                  pl.BlockSpec((d, nq), lambda i: (0, 0)),
                  key_spec, key_spec],
        out_specs=(pl.BlockSpec((ROW_TILE, d), lambda i: (i, 0)),
                   score_spec, score_spec, score_spec, score_spec,
                   pl.BlockSpec((PEER_HEADS, ROW_TILE), lambda i: (0, i))),
        scratch_shapes=[pltpu.VMEM((N_CAND_ROWS, ROW_TILE), f32)],
        compiler_params=_params("parallel"),
    )(x, shift_t, scale_t, wq_hi, wq_lo, keys_hi, keys_lo)


def _expert_kernel(xm_ref, u_ref, vt_ref, s1_ref, e1_ref, s2_ref, e2_ref, tau_ref, o_ref, a_scr, w_scr):
    j = pl.program_id(1)

    @pl.when(j == 0)
    def _():
        o_ref[...] = jnp.zeros(o_ref.shape, f32)

    a_scr[...] = _dot(u_ref[...], xm_ref[...], NT)
    sqrt_half = math.sqrt(0.5)
    for il in range(EXPERT_BLK // PEER_NKEYS):
        rows = slice(il * PEER_NKEYS, (il + 1) * PEER_NKEYS)
        for lb in range(EXPERT_TOK // LANE):
            lanes = slice(lb * LANE, (lb + 1) * LANE)
            gate = jnp.zeros((PEER_NKEYS, LANE), f32)
            for h in range(PEER_HEADS):
                t = s1_ref[h, il:il + 1, lanes] + s2_ref[h, :, lanes]
                w = e1_ref[h, il:il + 1, lanes] * e2_ref[h, :, lanes]
                gate = gate + jnp.where(t >= tau_ref[h:h + 1, lanes], w, 0.0)
            a = a_scr[rows, lanes]
            act = 0.5 * a * (1.0 + lax.erf(a * sqrt_half))
            w_scr[rows, lanes] = (gate * act).astype(bf16)
    o_ref[...] += _dot(vt_ref[...], w_scr[...])


def _experts(xm, u_bf, vt_bf, s1, e1, s2, e2, tau):
    rows, d = xm.shape
    n_exp = u_bf.shape[0]
    i1_per_blk = EXPERT_BLK // PEER_NKEYS
    sel = pl.BlockSpec((PEER_HEADS, i1_per_blk, EXPERT_TOK), lambda i, j: (0, j, i))
    full = pl.BlockSpec((PEER_HEADS, PEER_NKEYS, EXPERT_TOK), lambda i, j: (0, 0, i))
    return pl.pallas_call(
        _expert_kernel,
        out_shape=jax.ShapeDtypeStruct((d, rows), f32),
        grid=(rows // EXPERT_TOK, n_exp // EXPERT_BLK),
        in_specs=[pl.BlockSpec((EXPERT_TOK, d), lambda i, j: (i, 0)),
                  pl.BlockSpec((EXPERT_BLK, d), lambda i, j: (j, 0)),
                  pl.BlockSpec((d, EXPERT_BLK), lambda i, j: (0, j)),
                  sel, sel, full, full,
                  pl.BlockSpec((PEER_HEADS, EXPERT_TOK), lambda i, j: (0, i))],
        out_specs=pl.BlockSpec((d, EXPERT_TOK), lambda i, j: (0, i)),
        scratch_shapes=[pltpu.VMEM((EXPERT_BLK, EXPERT_TOK), f32),
                        pltpu.VMEM((EXPERT_BLK, EXPERT_TOK), bf16)],
        compiler_params=_params("parallel", "arbitrary"),
    )(xm, u_bf, vt_bf, s1, e1, s2, e2, tau)


def _peer_residual(x, shift_t, scale_t, gate_t, peer_q, peer_keys, peer_u, peer_v, ln_g, ln_b):
    wq_hi, wq_lo = _split(peer_q)
    keys_hi, keys_lo = _split(peer_keys)
    xm, s1, s2, e1, e2, tau = _route(x, shift_t, scale_t, wq_hi, wq_lo, keys_hi, keys_lo)
    y_t = _experts(xm, peer_u.astype(bf16), peer_v.T.astype(bf16), s1, e1, s2, e2, tau)
    return _resid_ln(x, y_t, gate_t, ln_g, ln_b, transposed=True)


def _grid_sincos(n_tokens, d):
    t = jnp.arange(n_tokens)
    row = (t // GRID_W).astype(f32)[:, None]
    col = (t % GRID_W).astype(f32)[:, None]
    n_freq = d // 4
    freq = jnp.exp(-math.log(POS_BASE) * jnp.arange(n_freq, dtype=f32) / n_freq)[None, :]
    return jnp.concatenate([jnp.sin(row * freq), jnp.cos(row * freq), jnp.sin(col * freq), jnp.cos(col * freq)], -1)


def _tile_rows(mod, src):
    d = mod.shape[1] // 6
    return [mod[src, i * d:(i + 1) * d][:, None, :] for i in range(6)]


def _pad_cols(w, n):
    return jnp.pad(w, ((0, 0), (0, n - w.shape[1])))


def kernel(x, c, ctx, c_ctx, l0_ada_w, l0_ada_b, l0_ln1_g, l0_ln1_b, l0_w_in, l0_conv_a, l0_a_log_a, l0_dt_bias_a, l0_norm_a, l0_conv_b, l0_conv_b_bias, l0_a_log_b, l0_dt_bias_b, l0_d_skip_b, l0_norm_b, l0_w_out, l0_ln2_g, l0_ln2_b, l0_peer_q, l0_peer_keys, l0_peer_u, l0_peer_v, l1_ada_w, l1_ada_b, l1_ln1_g, l1_ln1_b, l1_w_in, l1_conv_c, l1_i_bias, l1_f_bias, l1_norm_c, l1_w_out, l1_ln2_g, l1_ln2_b, l1_peer_q, l1_peer_keys, l1_peer_u, l1_peer_v):
    bsz, n_lat, d = x.shape
    n_ctx_tok = ctx.shape[1]
    rows_grid = n_lat // GRID_W
    seq = n_ctx_tok + n_lat
    n_ctx = n_ctx_tok // CHUNK
    tiles_b = seq // ROW_TILE
    ctx_tiles = n_ctx_tok // ROW_TILE
    lat_tiles = n_lat // ROW_TILE

    cond = jnp.zeros((8, d), f32).at[:bsz].set(c).at[bsz].set(c_ctx)
    src_all = np.concatenate([[bsz] * ctx_tiles + [b] * lat_tiles for b in range(bsz)])

    x_lat = x + _grid_sincos(n_lat, d).astype(x.dtype)
    xs_all = jnp.concatenate([ctx, x_lat], axis=1).reshape(bsz * seq, d)

    sh1, sc1, g1, sh2, sc2, g2 = _tile_rows(_ada(cond, l0_ada_w, l0_ada_b), src_all)
    qkv_w = A_HEADS * (2 * A_DK + A_DV)
    az = A_HEADS * A_DV
    ag = N_DIR * A_HEADS
    bd = B_HEADS * B_HEADDIM
    bx = bd + 2 * B_GROUPS * B_STATE
    bt = N_DIR * B_HEADS
    o0 = np.cumsum([0, qkv_w, az, ag, ag, bd, bx, bt])
    w_qkv, w_za, w_beta, w_dec, w_zb, w_xbc, w_dt = (l0_w_in[:, o0[i]:o0[i + 1]] for i in range(7))
    main_w = qkv_w + az + bd + bx
    w0 = _pad_cols(jnp.concatenate([w_qkv, w_za, w_zb, w_xbc, w_beta, w_dec, w_dt], axis=1), main_w + LANE)
    proj = _proj(xs_all, sh1, sc1, w0.astype(bf16)).reshape(bsz, seq, main_w + LANE)
    conv_w = jnp.concatenate([l0_conv_a, l0_conv_b], axis=1)
    conv_b = jnp.concatenate([jnp.zeros((qkv_w,), f32), l0_conv_b_bias]).reshape(1, -1)
    xbc_blk = (qkv_w + az + bd) // LANE
    col_blocks = list(range(qkv_w // LANE)) + list(range(xbc_blk, xbc_blk + bx // LANE))
    act = _conv_act(proj, conv_w, conv_b, col_blocks, seg0=n_ctx_tok,
                    n_norm=2 * A_HEADS, n_qscale=A_HEADS, qscale=A_DK ** -0.5)
    gates = proj[:, :, main_w:]
    o_a = _gdn(act, gates[..., :ag], gates[..., ag:2 * ag], l0_a_log_a, l0_dt_bias_a, n_ctx)
    o_b = _ssd(act, gates[..., 2 * ag:2 * ag + bt], l0_a_log_b, l0_dt_bias_b, n_ctx)
    x1 = _merge_even(xs_all, o_a.reshape(N_DIR, bsz * seq, az), o_b.reshape(N_DIR, bsz * seq, bd),
                     proj.reshape(bsz * seq, -1), act.reshape(bsz * seq, -1),
                     l0_norm_a, l0_norm_b, l0_d_skip_b, l0_w_out.astype(bf16), g1, l0_ln1_g, l0_ln1_b)
    x2 = _peer_residual(x1, sh2, sc2, g2, l0_peer_q, l0_peer_keys, l0_peer_u, l0_peer_v, l0_ln2_g, l0_ln2_b)

    sh1, sc1, g1, sh2, sc2, g2 = _tile_rows(_ada(cond, l1_ada_w, l1_ada_b), src_all)
    x2 = x2.reshape(bsz, seq, d)
    x2_ctx, x2_lat = x2[:, :n_ctx_tok], x2[:, n_ctx_tok:]
    lat_cm = x2_lat.reshape(bsz, rows_grid, GRID_W, d).transpose(0, 2, 1, 3).reshape(bsz, n_lat, d)
    xs1 = jnp.concatenate([x2_ctx, lat_cm], axis=1).reshape(bsz * seq, d)
    qk_w = 2 * C_HEADS * C_DK
    vw = C_HEADS * C_DV
    w1 = _pad_cols(l1_w_in, qk_w + 2 * vw + LANE)
    proj1 = _proj(xs1, sh1, sc1, w1.astype(bf16)).reshape(bsz, seq, -1)
    act1 = _conv_act(proj1, l1_conv_c, jnp.zeros((1, qk_w), f32), list(range(qk_w // LANE)), seg0=n_ctx_tok,
                     n_norm=0, n_qscale=C_HEADS, qscale=C_DK ** -0.5)
    gates1 = proj1[:, :, qk_w + 2 * vw:]
    cg = N_DIR * C_HEADS
    h1 = _mlstm(act1, proj1, gates1[..., :cg], gates1[..., cg:2 * cg], l1_i_bias, l1_f_bias, n_ctx)
    y_cm = _merge_odd(h1.reshape(N_DIR, bsz * seq, vw), proj1.reshape(bsz * seq, -1), l1_norm_c,
                      l1_w_out.astype(bf16), tiles_b, ctx_tiles, bsz)
    y_lat = y_cm.reshape(bsz, GRID_W, rows_grid, d).transpose(0, 2, 1, 3).reshape(bsz * n_lat, d)
    lat_sel = lambda t: t.reshape(bsz, tiles_b, 1, d)[:, ctx_tiles:].reshape(bsz * lat_tiles, 1, d)
    x3 = _resid_ln(x2_lat.reshape(bsz * n_lat, d), y_lat, lat_sel(g1), l1_ln1_g, l1_ln1_b, transposed=False)
    x4 = _peer_residual(x3, lat_sel(sh2), lat_sel(sc2), lat_sel(g2), l1_peer_q, l1_peer_keys, l1_peer_u,
                        l1_peer_v, l1_ln2_g, l1_ln2_b)
    return x4.reshape(bsz, n_lat, d)
```

```python
import functools
import math

import numpy as np
import jax
import jax.numpy as jnp
from jax import lax
from jax.experimental import pallas as pl
from jax.experimental.pallas import tpu as pltpu

f32 = jnp.float32
bf16 = jnp.bfloat16

D_MODEL = 1024
GRID_W = 64
CHUNK = 64
CONV_W = 5
POS_BASE = 10000.0
EPS = 1e-6
N_DIR = 2
DEPTH = 2
ALPHA = (2 * DEPTH) ** 0.25

A_HEADS, A_DK, A_DV = 4, 128, 128
B_HEADS, B_HEADDIM, B_GROUPS, B_STATE = 8, 64, 2, 128
C_HEADS, C_DK, C_DV = 4, 128, 256
PEER_HEADS, PEER_NKEYS, PEER_DKEY, PEER_TOPK = 8, 128, 256, 16

LANE = 128
ROW_TILE = 256
EXPERT_TOK = 512
EXPERT_BLK = 1024
GATE_ROWS = 32
VMEM_LIMIT = 56 * 1024 * 1024
N_CAND_ROWS = -(-sum((PEER_TOPK + 1) // (p + 1) for p in range(PEER_TOPK + 1)) // 8) * 8

NN = (((1,), (0,)), ((), ()))
NT = (((1,), (1,)), ((), ()))


def _params(*sem):
    return pltpu.CompilerParams(dimension_semantics=sem, vmem_limit_bytes=VMEM_LIMIT)


def _dot(a, b, dims=NN):
    return lax.dot_general(a, b, dims, preferred_element_type=f32)


def _split(a):
    hi = a.astype(bf16)
    return hi, (a - hi.astype(f32)).astype(bf16)


def _dot3(a, b, dims=NN):
    ah, al = _split(a)
    bh, bl = _split(b)
    return _dot(ah, bh, dims) + (_dot(ah, bl, dims) + _dot(al, bh, dims))


def _sigmoid(x):
    return 1.0 / (1.0 + jnp.exp(-x))


def _silu(x):
    return x * _sigmoid(x)


def _softplus(x):
    return jnp.maximum(x, 0.0) + jnp.log1p(jnp.exp(-jnp.abs(x)))


def _layer_norm(z, g, b):
    mu = jnp.mean(z, -1, keepdims=True)
    zc = z - mu
    var = jnp.mean(zc * zc, -1, keepdims=True)
    return zc * lax.rsqrt(var + EPS) * g + b


def _rms(x, g):
    return x * lax.rsqrt(jnp.mean(x * x, -1, keepdims=True) + EPS) * g


def _chunk_masks(d):
    row = lax.broadcasted_iota(jnp.int32, (CHUNK, CHUNK), 0)
    col = lax.broadcasted_iota(jnp.int32, (CHUNK, CHUNK), 1)
    diff = jnp.where(d == 0, row - col, col - row)
    return diff >= 0, diff > 0, diff <= 0, row == col


def _cum(mask_incl, mask_incl_t, x_col, x_row):
    c_col = jnp.sum(jnp.where(mask_incl, x_row, 0.0), axis=1, keepdims=True)
    c_row = jnp.sum(jnp.where(mask_incl_t, x_col, 0.0), axis=0, keepdims=True)
    tot = jnp.sum(x_col, axis=0, keepdims=True)
    return c_col, c_row, tot


def _ada_kernel(c_ref, w_ref, b_ref, o_ref):
    o_ref[...] = _dot3(_silu(c_ref[...]), w_ref[...]) + b_ref[...]


def _ada(cond, w, b):
    rows, d = cond.shape
    n = w.shape[1]
    return pl.pallas_call(
        _ada_kernel,
        out_shape=jax.ShapeDtypeStruct((rows, n), f32),
        grid=(n // d,),
        in_specs=[pl.BlockSpec((rows, d), lambda j: (0, 0)),
                  pl.BlockSpec((d, d), lambda j: (0, j)),
                  pl.BlockSpec((1, d), lambda j: (0, j))],
        out_specs=pl.BlockSpec((rows, d), lambda j: (0, j)),
        compiler_params=_params("parallel"),
    )(cond, w, b.reshape(1, n))


def _proj_kernel(x_ref, sh_ref, sc_ref, w_ref, o_ref):
    xm = x_ref[...] * (1.0 + sc_ref[0]) + sh_ref[0]
    o_ref[...] = _dot(xm.astype(bf16), w_ref[...])


def _proj(x, shift_t, scale_t, w):
    rows, d = x.shape
    n = w.shape[1]
    return pl.pallas_call(
        _proj_kernel,
        out_shape=jax.ShapeDtypeStruct((rows, n), f32),
        grid=(rows // ROW_TILE,),
        in_specs=[pl.BlockSpec((ROW_TILE, d), lambda i: (i, 0)),
                  pl.BlockSpec((1, 1, d), lambda i: (i, 0, 0)),
                  pl.BlockSpec((1, 1, d), lambda i: (i, 0, 0)),
                  pl.BlockSpec((d, n), lambda i: (0, 0))],
        out_specs=pl.BlockSpec((ROW_TILE, n), lambda i: (i, 0)),
        compiler_params=_params("parallel"),
    )(x, shift_t, scale_t, w)


def _conv_kernel(x_ref, w_ref, b_ref, o_ref, pad_ref, *, seq, seg0, n_norm, n_qscale, qscale):
    j = pl.program_id(1)
    halo = 8
    pad_ref[0:halo, :] = jnp.zeros((halo, LANE), f32)
    pad_ref[halo + seq:2 * halo + seq, :] = jnp.zeros((halo, LANE), f32)
    pad_ref[halo:halo + seq, :] = x_ref[0]
    w = w_ref[...]
    bias = b_ref[...]
    scale = jnp.where(j < n_qscale, qscale, 1.0).astype(f32)
    norm_on = jnp.where(j < n_norm, 1.0, 0.0).astype(f32)
    rows = ROW_TILE

    def body(c, carry):
        r0 = pl.multiple_of(c * rows, rows)
        win = pad_ref[pl.ds(r0, rows + 2 * halo), :]
        t = r0 + lax.broadcasted_iota(jnp.int32, (rows, LANE), 0)
        acc = bias + w[2:3] * win[halo:halo + rows]
        for k in (0, 1, 3, 4):
            off = k - CONV_W // 2
            tap = win[halo + off:halo + off + rows]
            same_segment = ((t + off) >= seg0) == (t >= seg0)
            acc = acc + w[k:k + 1] * jnp.where(same_segment, tap, 0.0)
        y = _silu(acc)
        yn = y * lax.rsqrt(jnp.sum(y * y, -1, keepdims=True) + EPS)
        y = (norm_on * yn + (1.0 - norm_on) * y) * scale
        o_ref[0, pl.ds(r0, rows), :] = y
        return carry

    lax.fori_loop(0, seq // rows, body, 0)


def _conv_act(x, w, b, col_blocks, *, seg0, n_norm, n_qscale, qscale):
    bsz, seq, _ = x.shape
    n_out = len(col_blocks)
    first_gap = next((i for i, cb in enumerate(col_blocks) if cb != i), n_out)
    gap = col_blocks[first_gap] - first_gap if first_gap < n_out else 0
    in_map = lambda bi, j: (bi, 0, jnp.where(j < first_gap, j, j + gap))
    kern = functools.partial(_conv_kernel, seq=seq, seg0=seg0, n_norm=n_norm, n_qscale=n_qscale, qscale=qscale)
    return pl.pallas_call(
        kern,
        out_shape=jax.ShapeDtypeStruct((bsz, seq, n_out * LANE), f32),
        grid=(bsz, n_out),
        in_specs=[pl.BlockSpec((1, seq, LANE), in_map),
                  pl.BlockSpec((CONV_W, LANE), lambda bi, j: (0, j)),
                  pl.BlockSpec((1, LANE), lambda bi, j: (0, j))],
        out_specs=pl.BlockSpec((1, seq, LANE), lambda bi, j: (bi, 0, j)),
        scratch_shapes=[pltpu.VMEM((seq + 16, LANE), f32)],
        compiler_params=_params("parallel", "parallel"),
    )(x, w, b)


def _chunk_index(d, n, n_ctx, n_all):
    rev = jnp.where(n < n_ctx, n_ctx - 1 - n, n_all + n_ctx - 1 - n)
    return jnp.where(d == 0, n, rev)


def _gate_layouts(raw, heads):
    bsz, seq, _ = raw.shape
    r = raw.reshape(bsz, seq // CHUNK, CHUNK, N_DIR, heads)
    return r.transpose(3, 0, 1, 2, 4), r.transpose(3, 0, 1, 4, 2)


def _param_layouts(*ps):
    return jnp.stack(ps, axis=1).astype(f32), jnp.stack(ps, axis=2).astype(f32)


def _gdn_kernel(q_ref, k_ref, v_ref, bcol_ref, brow_ref, dcol_ref, drow_ref, pcol_ref, prow_ref,
                o_ref, st_ref, *, bsz):
    d = pl.program_id(0)
    n = pl.program_id(1)

    @pl.when(n == 0)
    def _():
        st_ref[...] = jnp.zeros(st_ref.shape, f32)

    m_incl, m_strict, m_incl_t, m_eye = _chunk_masks(d)
    eye = m_eye.astype(f32)
    row = lax.broadcasted_iota(jnp.int32, (CHUNK, CHUNK), 0)
    col = lax.broadcasted_iota(jnp.int32, (CHUNK, CHUNK), 1)
    blocks = [(row >> sh) == (col >> sh) for sh in (3, 4, 5, 6)]
    off_masks = [blocks[lvl + 1] & ~blocks[lvl] for lvl in range(3)]
    pc = pcol_ref[0]
    pr = prow_ref[0]
    chains = [(b, h) for b in range(bsz) for h in range(A_HEADS)]
    idx = range(len(chains))
    beta_col = [_sigmoid(bcol_ref[0, b, 0]) for b in range(bsz)]
    g_col = [-jnp.exp(pc[0:1, :]) * _softplus(dcol_ref[0, b, 0] + pc[1:2, :]) for b in range(bsz)]
    g_row = [-jnp.exp(pr[:, 0:1]) * _softplus(drow_ref[0, b, 0] + pr[:, 1:2]) for b in range(bsz)]
    sl = [slice(h * A_DK, (h + 1) * A_DK) for _, h in chains]
    q = [q_ref[b, :, sl[c]] for c, (b, _) in enumerate(chains)]
    k = [k_ref[b, :, sl[c]] for c, (b, _) in enumerate(chains)]
    v = [v_ref[b, :, sl[c]] for c, (b, _) in enumerate(chains)]
    cums = [_cum(m_incl, m_incl_t, g_col[b][:, h:h + 1], g_row[b][h:h + 1, :]) for b, h in chains]
    decay = [jnp.exp(jnp.where(m_incl, gc - gr, -jnp.inf)) for gc, gr, _ in cums]
    b_col = [beta_col[b][:, h:h + 1] for b, h in chains]
    kb = [k[c] * b_col[c] for c in idx]
    l_mat = [_dot3(kb[c], k[c], NT) * jnp.where(m_strict, decay[c], 0.0) for c in idx]
    nil = [-jnp.where(blocks[0], l_mat[c], 0.0) for c in idx]
    t_inv = [eye + nil[c] for c in idx]
    for _ in range(2):
        nil = [_dot3(nil[c], nil[c]) for c in idx]
        t_inv = [t_inv[c] + _dot3(t_inv[c], nil[c]) for c in idx]
    for lvl in range(3):
        left = [_dot3(t_inv[c], jnp.where(off_masks[lvl], l_mat[c], 0.0)) for c in idx]
        t_inv = [t_inv[c] - _dot3(left[c], t_inv[c]) for c in idx]
    e_col = [jnp.exp(gc) for gc, _, _ in cums]
    uw = [_dot3(t_inv[c], jnp.concatenate([v[c] * b_col[c], kb[c] * e_col[c]], axis=1)) for c in idx]
    qk = [_dot(q[c], k[c], NT) * decay[c] for c in idx]
    k_dec_t = [(k[c] * jnp.exp(cums[c][2] - cums[c][0])).T for c in idx]
    s = [st_ref[c] for c in idx]
    wq = [_dot(jnp.concatenate([uw[c][:, A_DV:], q[c] * e_col[c]], axis=0), s[c]) for c in idx]
    v_new = [uw[c][:, :A_DV] - wq[c][:CHUNK] for c in idx]
    out = [wq[c][CHUNK:] + _dot(qk[c], v_new[c]) for c in idx]
    s_new = [jnp.exp(cums[c][2]) * s[c] + _dot(k_dec_t[c], v_new[c]) for c in idx]
    for c, (b, _) in enumerate(chains):
        o_ref[0, b, :, sl[c]] = out[c]
        st_ref[c] = s_new[c]


def _gdn(act, beta_raw, decay_raw, a_log, dt_bias, n_ctx):
    bsz, seq, _ = act.shape
    n_all = seq // CHUNK
    hw = A_HEADS * A_DK
    bcol, brow = _gate_layouts(beta_raw, A_HEADS)
    dcol, drow = _gate_layouts(decay_raw, A_HEADS)
    pcol, prow = _param_layouts(a_log, dt_bias)
    ci = functools.partial(_chunk_index, n_ctx=n_ctx, n_all=n_all)
    tok = lambda blk: pl.BlockSpec((bsz, CHUNK, hw), lambda d, n: (0, ci(d, n), blk))
    gcol = pl.BlockSpec((1, bsz, 1, CHUNK, A_HEADS), lambda d, n: (d, 0, ci(d, n), 0, 0))
    grow = pl.BlockSpec((1, bsz, 1, A_HEADS, CHUNK), lambda d, n: (d, 0, ci(d, n), 0, 0))
    return pl.pallas_call(
        functools.partial(_gdn_kernel, bsz=bsz),
        out_shape=jax.ShapeDtypeStruct((N_DIR, bsz, seq, hw), f32),
        grid=(N_DIR, n_all),
        in_specs=[tok(0), tok(1), tok(2), gcol, grow, gcol, grow,
                  pl.BlockSpec((1, 2, A_HEADS), lambda d, n: (d, 0, 0)),
                  pl.BlockSpec((1, A_HEADS, 2), lambda d, n: (d, 0, 0))],
        out_specs=pl.BlockSpec((1, bsz, CHUNK, hw), lambda d, n: (d, 0, ci(d, n), 0)),
        scratch_shapes=[pltpu.VMEM((bsz * A_HEADS, A_DK, A_DV), f32)],
        compiler_params=_params("arbitrary", "arbitrary"),
    )(act, act, act, bcol, brow, dcol, drow, pcol, prow)


def _ssd_kernel(x_ref, bm_ref, cm_ref, tcol_ref, trow_ref, pcol_ref, prow_ref, o_ref, st_ref, *, bsz):
    d = pl.program_id(0)
    n = pl.program_id(1)

    @pl.when(n == 0)
    def _():
        st_ref[...] = jnp.zeros(st_ref.shape, f32)

    m_incl, _, m_incl_t, _ = _chunk_masks(d)
    pc = pcol_ref[0]
    pr = prow_ref[0]
    rep = B_HEADS // B_GROUPS
    groups = [(b, g) for b in range(bsz) for g in range(B_GROUPS)]
    heads = [(b, h) for b in range(bsz) for h in range(B_HEADS)]
    gsl = [slice(g * B_STATE, (g + 1) * B_STATE) for _, g in groups]
    hsl = [slice(h * B_HEADDIM, (h + 1) * B_HEADDIM) for _, h in heads]
    dt_col = [_softplus(tcol_ref[0, b, 0] + pc[1:2, :]) for b in range(bsz)]
    dt_row = [_softplus(trow_ref[0, b, 0] + pr[:, 1:2]) for b in range(bsz)]
    da_col = [dt_col[b] * (-jnp.exp(pc[0:1, :])) for b in range(bsz)]
    da_row = [dt_row[b] * (-jnp.exp(pr[:, 0:1])) for b in range(bsz)]
    bm = [bm_ref[b, :, gsl[i]] for i, (b, _) in enumerate(groups)]
    cm = [cm_ref[b, :, gsl[i]] for i, (b, _) in enumerate(groups)]
    hs = [st_ref[i] for i in range(len(groups))]
    cb = [_dot(cm[i], bm[i], NT) for i in range(len(groups))]
    y_off = [_dot(cm[i], hs[i]) for i in range(len(groups))]
    bm_t = [bm[i].T for i in range(len(groups))]
    cums = [_cum(m_incl, m_incl_t, da_col[b][:, h:h + 1], da_row[b][h:h + 1, :]) for b, h in heads]
    seg = [jnp.exp(jnp.where(m_incl, ac - ar, -jnp.inf)) for ac, ar, _ in cums]
    xdt = [x_ref[b, :, hsl[j]] * dt_col[b][:, h:h + 1] for j, (b, h) in enumerate(heads)]
    y_diag = [_dot(cb[j // rep] * seg[j], xdt[j]) for j in range(len(heads))]
    for j, (b, h) in enumerate(heads):
        r = h % rep
        o_ref[0, b, :, hsl[j]] = y_diag[j] + jnp.exp(cums[j][0]) * y_off[j // rep][:, r * B_HEADDIM:(r + 1) * B_HEADDIM]
    xdt_end = [xdt[j] * jnp.exp(cums[j][2] - cums[j][0]) for j in range(len(heads))]
    keep = [jnp.broadcast_to(jnp.exp(cums[j][2]), (1, B_HEADDIM)) for j in range(len(heads))]
    upd = [_dot(bm_t[i], jnp.concatenate(xdt_end[i * rep:(i + 1) * rep], axis=1)) for i in range(len(groups))]
    for i in range(len(groups)):
        st_ref[i] = jnp.concatenate(keep[i * rep:(i + 1) * rep], axis=1) * hs[i] + upd[i]


def _ssd(act, dt_raw, a_log, dt_bias, n_ctx):
    bsz, seq, _ = act.shape
    n_all = seq // CHUNK
    dinner = B_HEADS * B_HEADDIM
    gw = B_GROUPS * B_STATE
    tcol, trow = _gate_layouts(dt_raw, B_HEADS)
    pcol, prow = _param_layouts(a_log, dt_bias)
    ci = functools.partial(_chunk_index, n_ctx=n_ctx, n_all=n_all)
    x_off = (A_HEADS * (2 * A_DK + A_DV)) // dinner
    bm_off = (A_HEADS * (2 * A_DK + A_DV) + dinner) // gw
    return pl.pallas_call(
        functools.partial(_ssd_kernel, bsz=bsz),
        out_shape=jax.ShapeDtypeStruct((N_DIR, bsz, seq, dinner), f32),
        grid=(N_DIR, n_all),
        in_specs=[pl.BlockSpec((bsz, CHUNK, dinner), lambda d, n: (0, ci(d, n), x_off)),
                  pl.BlockSpec((bsz, CHUNK, gw), lambda d, n: (0, ci(d, n), bm_off)),
                  pl.BlockSpec((bsz, CHUNK, gw), lambda d, n: (0, ci(d, n), bm_off + 1)),
                  pl.BlockSpec((1, bsz, 1, CHUNK, B_HEADS), lambda d, n: (d, 0, ci(d, n), 0, 0)),
                  pl.BlockSpec((1, bsz, 1, B_HEADS, CHUNK), lambda d, n: (d, 0, ci(d, n), 0, 0)),
                  pl.BlockSpec((1, 2, B_HEADS), lambda d, n: (d, 0, 0)),
                  pl.BlockSpec((1, B_HEADS, 2), lambda d, n: (d, 0, 0))],
        out_specs=pl.BlockSpec((1, bsz, CHUNK, dinner), lambda d, n: (d, 0, ci(d, n), 0)),
        scratch_shapes=[pltpu.VMEM((bsz * B_GROUPS, B_STATE, (B_HEADS // B_GROUPS) * B_HEADDIM), f32)],
        compiler_params=_params("arbitrary", "arbitrary"),
    )(act, act, act, tcol, trow, pcol, prow)


def _mlstm_kernel(q_ref, k_ref, v_ref, icol_ref, irow_ref, fcol_ref, frow_ref, pcol_ref, prow_ref,
                  o_ref, c_ref, n_ref, m_ref, *, bsz):
    d = pl.program_id(0)
    n = pl.program_id(1)

    @pl.when(n == 0)
    def _():
        c_ref[...] = jnp.zeros(c_ref.shape, f32)
        n_ref[...] = jnp.zeros(n_ref.shape, f32)
        m_ref[...] = jnp.zeros(m_ref.shape, f32)

    m_incl, _, m_incl_t, _ = _chunk_masks(d)
    pc = pcol_ref[0]
    pr = prow_ref[0]
    chains = [(b, h) for b in range(bsz) for h in range(C_HEADS)]
    idx = range(len(chains))
    li_col = [icol_ref[0, b, 0] + pc[0:1, :] for b in range(bsz)]
    li_row = [irow_ref[0, b, 0] + pr[:, 0:1] for b in range(bsz)]
    lf_col = [-_softplus(-(fcol_ref[0, b, 0] + pc[1:2, :])) for b in range(bsz)]
    lf_row = [-_softplus(-(frow_ref[0, b, 0] + pr[:, 1:2])) for b in range(bsz)]
    q = [q_ref[b, :, h * C_DK:(h + 1) * C_DK] for b, h in chains]
    k = [k_ref[b, :, h * C_DK:(h + 1) * C_DK] for b, h in chains]
    v = [v_ref[b, :, h * C_DV:(h + 1) * C_DV] for b, h in chains]
    cums = [_cum(m_incl, m_incl_t, lf_col[b][:, h:h + 1], lf_row[b][h:h + 1, :]) for b, h in chains]
    logw_intra = [jnp.where(m_incl, cums[c][0] - cums[c][1] + li_row[b][h:h + 1, :], -jnp.inf)
                  for c, (b, h) in enumerate(chains)]
    m_prev = [m_ref[c] for c in idx]
    logw_inter = [cums[c][0] + m_prev[c] for c in idx]
    m_t = [jnp.maximum(logw_inter[c], jnp.max(logw_intra[c], axis=1, keepdims=True)) for c in idx]
    qk = [_dot(q[c], k[c], NT) for c in idx]
    c_mem = [c_ref[c] for c in idx]
    n_mem = [n_ref[c] for c in idx]
    qc = [_dot(q[c], c_mem[c]) for c in idx]
    s = [qk[c] * jnp.exp(logw_intra[c] - m_t[c]) for c in idx]
    w_inter = [jnp.exp(logw_inter[c] - m_t[c]) for c in idx]
    num = [_dot(s[c], v[c]) + w_inter[c] * qc[c] for c in idx]
    den = [jnp.sum(s[c], axis=1, keepdims=True) + w_inter[c] * jnp.sum(q[c] * n_mem[c], axis=1, keepdims=True)
           for c in idx]
    for c, (b, h) in enumerate(chains):
        o_ref[0, b, :, h * C_DV:(h + 1) * C_DV] = num[c] / jnp.maximum(jnp.abs(den[c]), jnp.exp(-m_t[c]))
    logw_end = [cums[c][2] - cums[c][0] + li_col[b][:, h:h + 1] for c, (b, h) in enumerate(chains)]
    m_new = [jnp.maximum(cums[c][2] + m_prev[c], jnp.max(logw_end[c], axis=0, keepdims=True)) for c in idx]
    kw = [k[c] * jnp.exp(logw_end[c] - m_new[c]) for c in idx]
    keep = [jnp.exp(cums[c][2] + m_prev[c] - m_new[c]) for c in idx]
    kv = [_dot(kw[c].T, v[c]) for c in idx]
    for c in idx:
        c_ref[c] = keep[c] * c_mem[c] + kv[c]
        n_ref[c] = keep[c] * n_mem[c] + jnp.sum(kw[c], axis=0, keepdims=True)
        m_ref[c] = m_new[c]


def _mlstm(act, proj, i_raw, f_raw, i_bias, f_bias, n_ctx):
    bsz, seq, _ = act.shape
    n_all = seq // CHUNK
    qw = C_HEADS * C_DK
    vw = C_HEADS * C_DV
    icol, irow = _gate_layouts(i_raw, C_HEADS)
    fcol, frow = _gate_layouts(f_raw, C_HEADS)
    pcol, prow = _param_layouts(i_bias, f_bias)
    ci = functools.partial(_chunk_index, n_ctx=n_ctx, n_all=n_all)
    gcol = pl.BlockSpec((1, bsz, 1, CHUNK, C_HEADS), lambda d, n: (d, 0, ci(d, n), 0, 0))
    grow = pl.BlockSpec((1, bsz, 1, C_HEADS, CHUNK), lambda d, n: (d, 0, ci(d, n), 0, 0))
    return pl.pallas_call(
        functools.partial(_mlstm_kernel, bsz=bsz),
        out_shape=jax.ShapeDtypeStruct((N_DIR, bsz, seq, vw), f32),
        grid=(N_DIR, n_all),
        in_specs=[pl.BlockSpec((bsz, CHUNK, qw), lambda d, n: (0, ci(d, n), 0)),
                  pl.BlockSpec((bsz, CHUNK, qw), lambda d, n: (0, ci(d, n), 1)),
                  pl.BlockSpec((bsz, CHUNK, vw), lambda d, n: (0, ci(d, n), 2 * qw // vw)),
                  gcol, grow, gcol, grow,
                  pl.BlockSpec((1, 2, C_HEADS), lambda d, n: (d, 0, 0)),
                  pl.BlockSpec((1, C_HEADS, 2), lambda d, n: (d, 0, 0))],
        out_specs=pl.BlockSpec((1, bsz, CHUNK, vw), lambda d, n: (d, 0, ci(d, n), 0)),
        scratch_shapes=[pltpu.VMEM((bsz * C_HEADS, C_DK, C_DV), f32),
                        pltpu.VMEM((bsz * C_HEADS, 1, C_DK), f32),
                        pltpu.VMEM((bsz * C_HEADS, 1, 1), f32)],
        compiler_params=_params("arbitrary", "arbitrary"),
    )(act, act, proj, icol, irow, fcol, frow, pcol, prow)


def _merge_even_kernel(x_ref, a_ref, b_ref, za_ref, zb_ref, xs_ref, na_ref, nb_ref, dsk_ref, w_ref,
                       g_ref, lg_ref, lb_ref, o_ref):
    a = a_ref[0] + a_ref[1]
    za = za_ref[...]
    na = na_ref[...]
    parts = []
    for h in range(A_HEADS):
        sl = slice(h * A_DV, (h + 1) * A_DV)
        parts.append(_rms(a[:, sl], na) * _silu(za[:, sl]))
    yb = (b_ref[0] + b_ref[1] + dsk_ref[...] * xs_ref[...]) * _silu(zb_ref[...])
    gw = (B_HEADS * B_HEADDIM) // B_GROUPS
    nb = nb_ref[...]
    for g in range(B_GROUPS):
        sl = slice(g * gw, (g + 1) * gw)
        parts.append(_rms(yb[:, sl], nb[:, sl]))
    y = jnp.concatenate(parts, axis=1).astype(bf16)
    z = ALPHA * x_ref[...] + g_ref[0] * _dot(y, w_ref[...])
    o_ref[...] = _layer_norm(z, lg_ref[...], lb_ref[...])


def _merge_even(x, o_a, o_b, proj, act, norm_a, norm_b, d_skip, w_out, gate_t, ln_g, ln_b):
    rows, d = x.shape
    aw = A_HEADS * A_DV
    bw = B_HEADS * B_HEADDIM
    row = lambda w_, blk: pl.BlockSpec((ROW_TILE, w_), lambda i: (i, blk))
    vec = lambda w_: pl.BlockSpec((1, w_), lambda i: (0, 0))
    qkv = A_HEADS * (2 * A_DK + A_DV)
    return pl.pallas_call(
        _merge_even_kernel,
        out_shape=jax.ShapeDtypeStruct((rows, d), f32),
        grid=(rows // ROW_TILE,),
        in_specs=[row(d, 0),
                  pl.BlockSpec((N_DIR, ROW_TILE, aw), lambda i: (0, i, 0)),
                  pl.BlockSpec((N_DIR, ROW_TILE, bw), lambda i: (0, i, 0)),
                  row(aw, qkv // aw), row(bw, (qkv + aw) // bw), row(bw, qkv // bw),
                  vec(A_DV), vec(bw), vec(bw),
                  pl.BlockSpec((aw + bw, d), lambda i: (0, 0)),
                  pl.BlockSpec((1, 1, d), lambda i: (i, 0, 0)),
                  vec(d), vec(d)],
        out_specs=row(d, 0),
        compiler_params=_params("parallel"),
    )(x, o_a, o_b, proj, proj, act, norm_a.reshape(1, -1), norm_b.reshape(1, -1),
      jnp.repeat(d_skip, B_HEADDIM).reshape(1, -1), w_out, gate_t, ln_g.reshape(1, -1), ln_b.reshape(1, -1))


def _merge_odd_kernel(h_ref, o_ref_in, nc_ref, w_ref, out_ref):
    hsum = h_ref[0] + h_ref[1]
    o = o_ref_in[...]
    nc = nc_ref[...]
    parts = []
    for h in range(C_HEADS):
        sl = slice(h * C_DV, (h + 1) * C_DV)
        parts.append(_rms(hsum[:, sl], nc) * _sigmoid(o[:, sl]))
    y = jnp.concatenate(parts, axis=1).astype(bf16)
    out_ref[...] = _dot(y, w_ref[...])


def _merge_odd(h, proj, norm_c, w_out, tiles_per_batch, ctx_tiles, bsz):
    vw = C_HEADS * C_DV
    lat_tiles = tiles_per_batch - ctx_tiles
    src = lambda i: i + ctx_tiles * (i // lat_tiles + 1)
    d = w_out.shape[1]
    return pl.pallas_call(
        _merge_odd_kernel,
        out_shape=jax.ShapeDtypeStruct((bsz * lat_tiles * ROW_TILE, d), f32),
        grid=(bsz * lat_tiles,),
        in_specs=[pl.BlockSpec((N_DIR, ROW_TILE, vw), lambda i: (0, src(i), 0)),
                  pl.BlockSpec((ROW_TILE, vw), lambda i: (src(i), 2 * C_HEADS * C_DK // vw + 1)),
                  pl.BlockSpec((1, C_DV), lambda i: (0, 0)),
                  pl.BlockSpec((vw, d), lambda i: (0, 0))],
        out_specs=pl.BlockSpec((ROW_TILE, d), lambda i: (i, 0)),
        compiler_params=_params("parallel"),
    )(h, proj, norm_c.reshape(1, -1), w_out)


def _resid_ln_kernel(x_ref, y_ref, g_ref, lg_ref, lb_ref, o_ref, *, transposed):
    y = y_ref[...]
    if transposed:
        y = y.T
    o_ref[...] = _layer_norm(ALPHA * x_ref[...] + g_ref[0] * y, lg_ref[...], lb_ref[...])


def _resid_ln(x, y, gate_t, ln_g, ln_b, *, transposed):
    rows, d = x.shape
    y_spec = (pl.BlockSpec((d, ROW_TILE), lambda i: (0, i)) if transposed
              else pl.BlockSpec((ROW_TILE, d), lambda i: (i, 0)))
    return pl.pallas_call(
        functools.partial(_resid_ln_kernel, transposed=transposed),
        out_shape=jax.ShapeDtypeStruct((rows, d), f32),
        grid=(rows // ROW_TILE,),
        in_specs=[pl.BlockSpec((ROW_TILE, d), lambda i: (i, 0)), y_spec,
                  pl.BlockSpec((1, 1, d), lambda i: (i, 0, 0)),
                  pl.BlockSpec((1, d), lambda i: (0, 0)),
                  pl.BlockSpec((1, d), lambda i: (0, 0))],
        out_specs=pl.BlockSpec((ROW_TILE, d), lambda i: (i, 0)),
        compiler_params=_params("parallel"),
    )(x, y, gate_t, ln_g.reshape(1, -1), ln_b.reshape(1, -1))


def _top_values(s, count):
    rows = float(s.shape[0])
    idx = lax.broadcasted_iota(jnp.int32, s.shape, 0).astype(f32)
    vals = []
    for _ in range(count):
        m = jnp.max(s, axis=0, keepdims=True)
        first = jnp.min(jnp.where(s == m, idx, rows), axis=0, keepdims=True)
        vals.append(m)
        s = jnp.where(idx == first, -jnp.inf, s)
    return vals


def _route_kernel(x_ref, sh_ref, sc_ref, wh_ref, wl_ref, kh_ref, kl_ref,
                  xm_ref, thr_ref, s2_ref, e1_ref, e2_ref, cand_ref):
    xm = x_ref[...] * (1.0 + sc_ref[0]) + sh_ref[0]
    xm_ref[...] = xm.astype(bf16)
    xh, xl = _split(xm)
    half = PEER_DKEY // 2
    n_top = PEER_TOPK + 1
    for h in range(PEER_HEADS):
        wsl = slice(h * PEER_DKEY, (h + 1) * PEER_DKEY)
        q = _dot(xh, wh_ref[:, wsl]) + (_dot(xh, wl_ref[:, wsl]) + _dot(xl, wh_ref[:, wsl]))
        scores, tops = [], []
        for c in range(2):
            qh, ql = _split(q[:, c * half:(c + 1) * half])
            kh = kh_ref[h, c]
            st = _dot(kh, qh, NT) + (_dot(kl_ref[h, c], qh, NT) + _dot(kh, ql, NT))
            scores.append(st)
            tops.append(_top_values(st, n_top))
        t1, t2 = tops
        pairs = [(p, r) for p in range(n_top) for r in range(n_top) if (p + 1) * (r + 1) <= n_top]
        cand_ref[...] = jnp.full(cand_ref.shape, -jnp.inf, f32)
        for i, (p, r) in enumerate(pairs):
            cand_ref[i:i + 1, :] = t1[p] + t2[r]
        best = _top_values(cand_ref[...], n_top)
        z = sum(jnp.exp(bv - best[0]) for bv in best[:PEER_TOPK])
        tau = 0.5 * (best[PEER_TOPK - 1] + best[PEER_TOPK])
        thr_ref[h] = tau - scores[0]
        s2_ref[h] = scores[1]
        e1_ref[h] = jnp.exp(scores[0] - t1[0]) / z
        e2_ref[h] = jnp.exp(scores[1] - t2[0])


def _route(x, shift_t, scale_t, wq_hi, wq_lo, keys_hi, keys_lo):
    rows, d = x.shape
    nq = wq_hi.shape[1]
    half = PEER_DKEY // 2
    score = jax.ShapeDtypeStruct((PEER_HEADS, PEER_NKEYS, rows), f32)
    score_spec = pl.BlockSpec((PEER_HEADS, PEER_NKEYS, ROW_TILE), lambda i: (0, 0, i))
    key_spec = pl.BlockSpec((PEER_HEADS, 2, PEER_NKEYS, half), lambda i: (0, 0, 0, 0))
    return pl.pallas_call(
        _route_kernel,
        out_shape=(jax.ShapeDtypeStruct((rows, d), bf16), score, score, score, score),
        grid=(rows // ROW_TILE,),
        in_specs=[pl.BlockSpec((ROW_TILE, d), lambda i: (i, 0)),
                  pl.BlockSpec((1, 1, d), lambda i: (i, 0, 0)),
                  pl.BlockSpec((1, 1, d), lambda i: (i, 0, 0)),
                  pl.BlockSpec((d, nq), lambda i: (0, 0)),
                  pl.BlockSpec((d, nq), lambda i: (0, 0)),
                  key_spec, key_spec],
        out_specs=(pl.BlockSpec((ROW_TILE, d), lambda i: (i, 0)),
                   score_spec, score_spec, score_spec, score_spec),
        scratch_shapes=[pltpu.VMEM((N_CAND_ROWS, ROW_TILE), f32)],
        compiler_params=_params("parallel"),
    )(x, shift_t, scale_t, wq_hi, wq_lo, keys_hi, keys_lo)


def _expert_kernel(xm_ref, u_ref, vt_ref, thr_ref, e1_ref, s2_ref, e2_ref, o_ref, a_scr, w_scr):
    j = pl.program_id(1)

    @pl.when(j == 0)
    def _():
        o_ref[...] = jnp.zeros(o_ref.shape, f32)

    a_scr[...] = _dot(u_ref[...], xm_ref[...], NT)
    sqrt_half = math.sqrt(0.5)
    n_i1 = EXPERT_BLK // PEER_NKEYS
    row_chunks = PEER_NKEYS // GATE_ROWS

    def tile(t, carry):
        lanes = pl.ds(pl.multiple_of((t // row_chunks) * LANE, LANE), LANE)
        r0 = pl.multiple_of((t % row_chunks) * GATE_ROWS, GATE_ROWS)
        i2 = pl.ds(r0, GATE_ROWS)
        gate = [jnp.zeros((GATE_ROWS, LANE), f32) for _ in range(n_i1)]
        for h in range(PEER_HEADS):
            s2 = s2_ref[h, i2, lanes]
            e2 = e2_ref[h, i2, lanes]
            for il in range(n_i1):
                w = e1_ref[h, il:il + 1, lanes] * e2
                gate[il] = gate[il] + jnp.where(s2 >= thr_ref[h, il:il + 1, lanes], w, 0.0)
        for il in range(n_i1):
            rows = pl.ds(il * PEER_NKEYS + r0, GATE_ROWS)
            a = a_scr[rows, lanes]
            act = 0.5 * a * (1.0 + lax.erf(a * sqrt_half))
            w_scr[rows, lanes] = (gate[il] * act).astype(bf16)
        return carry

    lax.fori_loop(0, (EXPERT_TOK // LANE) * row_chunks, tile, 0)
    o_ref[...] += _dot(vt_ref[...], w_scr[...])


def _experts(xm, u_bf, vt_bf, thr, e1, s2, e2):
    rows, d = xm.shape
    n_exp = u_bf.shape[0]
    i1_per_blk = EXPERT_BLK // PEER_NKEYS
    sel = pl.BlockSpec((PEER_HEADS, i1_per_blk, EXPERT_TOK), lambda i, j: (0, j, i))
    full = pl.BlockSpec((PEER_HEADS, PEER_NKEYS, EXPERT_TOK), lambda i, j: (0, 0, i))
    return pl.pallas_call(
        _expert_kernel,
        out_shape=jax.ShapeDtypeStruct((d, rows), f32),
        grid=(rows // EXPERT_TOK, n_exp // EXPERT_BLK),
        in_specs=[pl.BlockSpec((EXPERT_TOK, d), lambda i, j: (i, 0)),
                  pl.BlockSpec((EXPERT_BLK, d), lambda i, j: (j, 0)),
                  pl.BlockSpec((d, EXPERT_BLK), lambda i, j: (0, j)),
                  sel, sel, full, full],
        out_specs=pl.BlockSpec((d, EXPERT_TOK), lambda i, j: (0, i)),
        scratch_shapes=[pltpu.VMEM((EXPERT_BLK, EXPERT_TOK), f32),
                        pltpu.VMEM((EXPERT_BLK, EXPERT_TOK), bf16)],
        compiler_params=_params("parallel", "arbitrary"),
    )(xm, u_bf, vt_bf, thr, e1, s2, e2)


def _peer_residual(x, shift_t, scale_t, gate_t, peer_q, peer_keys, peer_u, peer_v, ln_g, ln_b):
    wq_hi, wq_lo = _split(peer_q)
    keys_hi, keys_lo = _split(peer_keys)
    xm, thr, s2, e1, e2 = _route(x, shift_t, scale_t, wq_hi, wq_lo, keys_hi, keys_lo)
    y_t = _experts(xm, peer_u.astype(bf16), peer_v.T.astype(bf16), thr, e1, s2, e2)
    return _resid_ln(x, y_t, gate_t, ln_g, ln_b, transposed=True)


def _grid_sincos(n_tokens, d):
    t = jnp.arange(n_tokens)
    row = (t // GRID_W).astype(f32)[:, None]
    col = (t % GRID_W).astype(f32)[:, None]
    n_freq = d // 4
    freq = jnp.exp(-math.log(POS_BASE) * jnp.arange(n_freq, dtype=f32) / n_freq)[None, :]
    return jnp.concatenate([jnp.sin(row * freq), jnp.cos(row * freq), jnp.sin(col * freq), jnp.cos(col * freq)], -1)


def _tile_rows(mod, src):
    d = mod.shape[1] // 6
    rows = jnp.concatenate([jnp.broadcast_to(mod[r:r + 1], (n, mod.shape[1])) for r, n in src], axis=0)
    return [rows[:, None, i * d:(i + 1) * d] for i in range(6)]


def _pad_cols(w, n):
    return jnp.pad(w, ((0, 0), (0, n - w.shape[1])))


def kernel(x, c, ctx, c_ctx, l0_ada_w, l0_ada_b, l0_ln1_g, l0_ln1_b, l0_w_in, l0_conv_a, l0_a_log_a, l0_dt_bias_a, l0_norm_a, l0_conv_b, l0_conv_b_bias, l0_a_log_b, l0_dt_bias_b, l0_d_skip_b, l0_norm_b, l0_w_out, l0_ln2_g, l0_ln2_b, l0_peer_q, l0_peer_keys, l0_peer_u, l0_peer_v, l1_ada_w, l1_ada_b, l1_ln1_g, l1_ln1_b, l1_w_in, l1_conv_c, l1_i_bias, l1_f_bias, l1_norm_c, l1_w_out, l1_ln2_g, l1_ln2_b, l1_peer_q, l1_peer_keys, l1_peer_u, l1_peer_v):
    bsz, n_lat, d = x.shape
    n_ctx_tok = ctx.shape[1]
    rows_grid = n_lat // GRID_W
    seq = n_ctx_tok + n_lat
    n_ctx = n_ctx_tok // CHUNK
    tiles_b = seq // ROW_TILE
    ctx_tiles = n_ctx_tok // ROW_TILE
    lat_tiles = n_lat // ROW_TILE

    cond = jnp.concatenate([c, c_ctx[None, :], jnp.zeros((8 - bsz - 1, d), f32)], axis=0)
    src_all = [run for b in range(bsz) for run in ((bsz, ctx_tiles), (b, lat_tiles))]

    x_lat = x + _grid_sincos(n_lat, d).astype(x.dtype)
    xs_all = jnp.concatenate([ctx, x_lat], axis=1).reshape(bsz * seq, d)

    sh1, sc1, g1, sh2, sc2, g2 = _tile_rows(_ada(cond, l0_ada_w, l0_ada_b), src_all)
    qkv_w = A_HEADS * (2 * A_DK + A_DV)
    az = A_HEADS * A_DV
    ag = N_DIR * A_HEADS
    bd = B_HEADS * B_HEADDIM
    bx = bd + 2 * B_GROUPS * B_STATE
    bt = N_DIR * B_HEADS
    o0 = np.cumsum([0, qkv_w, az, ag, ag, bd, bx, bt])
    w_qkv, w_za, w_beta, w_dec, w_zb, w_xbc, w_dt = (l0_w_in[:, o0[i]:o0[i + 1]] for i in range(7))
    main_w = qkv_w + az + bd + bx
    w0 = _pad_cols(jnp.concatenate([w_qkv, w_za, w_zb, w_xbc, w_beta, w_dec, w_dt], axis=1), main_w + LANE)
    proj = _proj(xs_all, sh1, sc1, w0.astype(bf16)).reshape(bsz, seq, main_w + LANE)
    conv_w = jnp.concatenate([l0_conv_a, l0_conv_b], axis=1)
    conv_b = jnp.concatenate([jnp.zeros((qkv_w,), f32), l0_conv_b_bias]).reshape(1, -1)
    xbc_blk = (qkv_w + az + bd) // LANE
    col_blocks = list(range(qkv_w // LANE)) + list(range(xbc_blk, xbc_blk + bx // LANE))
    act = _conv_act(proj, conv_w, conv_b, col_blocks, seg0=n_ctx_tok,
                    n_norm=2 * A_HEADS, n_qscale=A_HEADS, qscale=A_DK ** -0.5)
    gates = proj[:, :, main_w:]
    o_a = _gdn(act, gates[..., :ag], gates[..., ag:2 * ag], l0_a_log_a, l0_dt_bias_a, n_ctx)
    o_b = _ssd(act, gates[..., 2 * ag:2 * ag + bt], l0_a_log_b, l0_dt_bias_b, n_ctx)
    x1 = _merge_even(xs_all, o_a.reshape(N_DIR, bsz * seq, az), o_b.reshape(N_DIR, bsz * seq, bd),
                     proj.reshape(bsz * seq, -1), act.reshape(bsz * seq, -1),
                     l0_norm_a, l0_norm_b, l0_d_skip_b, l0_w_out.astype(bf16), g1, l0_ln1_g, l0_ln1_b)
    x2 = _peer_residual(x1, sh2, sc2, g2, l0_peer_q, l0_peer_keys, l0_peer_u, l0_peer_v, l0_ln2_g, l0_ln2_b)

    sh1, sc1, g1, sh2, sc2, g2 = _tile_rows(_ada(cond, l1_ada_w, l1_ada_b), src_all)
    x2 = x2.reshape(bsz, seq, d)
    x2_ctx, x2_lat = x2[:, :n_ctx_tok], x2[:, n_ctx_tok:]
    lat_cm = x2_lat.reshape(bsz, rows_grid, GRID_W, d).transpose(0, 2, 1, 3).reshape(bsz, n_lat, d)
    xs1 = jnp.concatenate([x2_ctx, lat_cm], axis=1).reshape(bsz * seq, d)
    qk_w = 2 * C_HEADS * C_DK
    vw = C_HEADS * C_DV
    w1 = _pad_cols(l1_w_in, qk_w + 2 * vw + LANE)
    proj1 = _proj(xs1, sh1, sc1, w1.astype(bf16)).reshape(bsz, seq, -1)
    act1 = _conv_act(proj1, l1_conv_c, jnp.zeros((1, qk_w), f32), list(range(qk_w // LANE)), seg0=n_ctx_tok,
                     n_norm=0, n_qscale=C_HEADS, qscale=C_DK ** -0.5)
    gates1 = proj1[:, :, qk_w + 2 * vw:]
    cg = N_DIR * C_HEADS
    h1 = _mlstm(act1, proj1, gates1[..., :cg], gates1[..., cg:2 * cg], l1_i_bias, l1_f_bias, n_ctx)
    y_cm = _merge_odd(h1.reshape(N_DIR, bsz * seq, vw), proj1.reshape(bsz * seq, -1), l1_norm_c,
                      l1_w_out.astype(bf16), tiles_b, ctx_tiles, bsz)
    y_lat = y_cm.reshape(bsz, GRID_W, rows_grid, d).transpose(0, 2, 1, 3).reshape(bsz * n_lat, d)
    lat_sel = lambda t: t.reshape(bsz, tiles_b, 1, d)[:, ctx_tiles:].reshape(bsz * lat_tiles, 1, d)
    x3 = _resid_ln(x2_lat.reshape(bsz * n_lat, d), y_lat, lat_sel(g1), l1_ln1_g, l1_ln1_b, transposed=False)
    x4 = _peer_residual(x3, lat_sel(sh2), lat_sel(sc2), lat_sel(g2), l1_peer_q, l1_peer_keys, l1_peer_u,
                        l1_peer_v, l1_ln2_g, l1_ln2_b)
    return x4.reshape(bsz, n_lat, d)
```

```python
import functools
import math

import numpy as np
import jax
import jax.numpy as jnp
from jax import lax
from jax.experimental import pallas as pl
from jax.experimental.pallas import tpu as pltpu

f32 = jnp.float32
bf16 = jnp.bfloat16

D_MODEL = 1024
GRID_W = 64
CHUNK = 64
CONV_W = 5
POS_BASE = 10000.0
EPS = 1e-6
N_DIR = 2
DEPTH = 2
ALPHA = (2 * DEPTH) ** 0.25

A_HEADS, A_DK, A_DV = 4, 128, 128
B_HEADS, B_HEADDIM, B_GROUPS, B_STATE = 8, 64, 2, 128
C_HEADS, C_DK, C_DV = 4, 128, 256
PEER_HEADS, PEER_NKEYS, PEER_DKEY, PEER_TOPK = 8, 128, 256, 16

LANE = 128
ROW_TILE = 256
EXPERT_TOK = 512
EXPERT_BLK = 1024
GATE_ROWS = 32
VMEM_LIMIT = 56 * 1024 * 1024
N_CAND_ROWS = -(-sum((PEER_TOPK + 1) // (p + 1) for p in range(PEER_TOPK + 1)) // 8) * 8

NN = (((1,), (0,)), ((), ()))
NT = (((1,), (1,)), ((), ()))


def _params(*sem):
    return pltpu.CompilerParams(dimension_semantics=sem, vmem_limit_bytes=VMEM_LIMIT)


def _dot(a, b, dims=NN):
    return lax.dot_general(a, b, dims, preferred_element_type=f32)


def _split(a):
    hi = a.astype(bf16)
    return hi, (a - hi.astype(f32)).astype(bf16)


def _dot3(a, b, dims=NN):
    ah, al = _split(a)
    bh, bl = _split(b)
    return _dot(ah, bh, dims) + (_dot(ah, bl, dims) + _dot(al, bh, dims))


def _sigmoid(x):
    return 1.0 / (1.0 + jnp.exp(-x))


def _silu(x):
    return x * _sigmoid(x)


def _softplus(x):
    return jnp.maximum(x, 0.0) + jnp.log1p(jnp.exp(-jnp.abs(x)))


def _layer_norm(z, g, b):
    mu = jnp.mean(z, -1, keepdims=True)
    zc = z - mu
    var = jnp.mean(zc * zc, -1, keepdims=True)
    return zc * lax.rsqrt(var + EPS) * g + b


def _rms(x, g):
    return x * lax.rsqrt(jnp.mean(x * x, -1, keepdims=True) + EPS) * g


def _chunk_masks(d):
    row = lax.broadcasted_iota(jnp.int32, (CHUNK, CHUNK), 0)
    col = lax.broadcasted_iota(jnp.int32, (CHUNK, CHUNK), 1)
    diff = jnp.where(d == 0, row - col, col - row)
    return diff >= 0, diff > 0, diff <= 0, row == col


def _cum(mask_incl, mask_incl_t, x_col, x_row):
    c_col = jnp.sum(jnp.where(mask_incl, x_row, 0.0), axis=1, keepdims=True)
    c_row = jnp.sum(jnp.where(mask_incl_t, x_col, 0.0), axis=0, keepdims=True)
    tot = jnp.sum(x_col, axis=0, keepdims=True)
    return c_col, c_row, tot


def _ada_kernel(c_ref, w_ref, b_ref, o_ref):
    o_ref[...] = _dot3(_silu(c_ref[...]), w_ref[...]) + b_ref[...]


def _ada(cond, w, b):
    rows, d = cond.shape
    n = w.shape[1]
    return pl.pallas_call(
        _ada_kernel,
        out_shape=jax.ShapeDtypeStruct((rows, n), f32),
        grid=(n // d,),
        in_specs=[pl.BlockSpec((rows, d), lambda j: (0, 0)),
                  pl.BlockSpec((d, d), lambda j: (0, j)),
                  pl.BlockSpec((1, d), lambda j: (0, j))],
        out_specs=pl.BlockSpec((rows, d), lambda j: (0, j)),
        compiler_params=_params("parallel"),
    )(cond, w, b.reshape(1, n))


def _proj_kernel(x_ref, sh_ref, sc_ref, w_ref, o_ref):
    xm = x_ref[...] * (1.0 + sc_ref[0]) + sh_ref[0]
    o_ref[...] = _dot(xm.astype(bf16), w_ref[...])


def _proj(x, shift_t, scale_t, w):
    rows, d = x.shape
    n = w.shape[1]
    return pl.pallas_call(
        _proj_kernel,
        out_shape=jax.ShapeDtypeStruct((rows, n), f32),
        grid=(rows // ROW_TILE,),
        in_specs=[pl.BlockSpec((ROW_TILE, d), lambda i: (i, 0)),
                  pl.BlockSpec((1, 1, d), lambda i: (i, 0, 0)),
                  pl.BlockSpec((1, 1, d), lambda i: (i, 0, 0)),
                  pl.BlockSpec((d, n), lambda i: (0, 0))],
        out_specs=pl.BlockSpec((ROW_TILE, n), lambda i: (i, 0)),
        compiler_params=_params("parallel"),
    )(x, shift_t, scale_t, w)


def _conv_kernel(x_ref, w_ref, b_ref, o_ref, pad_ref, *, seq, seg0, n_norm, n_qscale, qscale):
    j = pl.program_id(1)
    halo = 8
    pad_ref[0:halo, :] = jnp.zeros((halo, LANE), f32)
    pad_ref[halo + seq:2 * halo + seq, :] = jnp.zeros((halo, LANE), f32)
    pad_ref[halo:halo + seq, :] = x_ref[0]
    w = w_ref[...]
    bias = b_ref[...]
    scale = jnp.where(j < n_qscale, qscale, 1.0).astype(f32)
    norm_on = jnp.where(j < n_norm, 1.0, 0.0).astype(f32)
    rows = ROW_TILE

    def body(c, carry):
        r0 = pl.multiple_of(c * rows, rows)
        win = pad_ref[pl.ds(r0, rows + 2 * halo), :]
        t = r0 + lax.broadcasted_iota(jnp.int32, (rows, LANE), 0)
        acc = bias + w[2:3] * win[halo:halo + rows]
        for k in (0, 1, 3, 4):
            off = k - CONV_W // 2
            tap = win[halo + off:halo + off + rows]
            same_segment = ((t + off) >= seg0) == (t >= seg0)
            acc = acc + w[k:k + 1] * jnp.where(same_segment, tap, 0.0)
        y = _silu(acc)
        yn = y * lax.rsqrt(jnp.sum(y * y, -1, keepdims=True) + EPS)
        y = (norm_on * yn + (1.0 - norm_on) * y) * scale
        o_ref[0, pl.ds(r0, rows), :] = y
        return carry

    lax.fori_loop(0, seq // rows, body, 0)


def _conv_act(x, w, b, col_blocks, *, seg0, n_norm, n_qscale, qscale):
    bsz, seq, _ = x.shape
    n_out = len(col_blocks)
    first_gap = next((i for i, cb in enumerate(col_blocks) if cb != i), n_out)
    gap = col_blocks[first_gap] - first_gap if first_gap < n_out else 0
    in_map = lambda bi, j: (bi, 0, jnp.where(j < first_gap, j, j + gap))
    kern = functools.partial(_conv_kernel, seq=seq, seg0=seg0, n_norm=n_norm, n_qscale=n_qscale, qscale=qscale)
    return pl.pallas_call(
        kern,
        out_shape=jax.ShapeDtypeStruct((bsz, seq, n_out * LANE), f32),
        grid=(bsz, n_out),
        in_specs=[pl.BlockSpec((1, seq, LANE), in_map),
                  pl.BlockSpec((CONV_W, LANE), lambda bi, j: (0, j)),
                  pl.BlockSpec((1, LANE), lambda bi, j: (0, j))],
        out_specs=pl.BlockSpec((1, seq, LANE), lambda bi, j: (bi, 0, j)),
        scratch_shapes=[pltpu.VMEM((seq + 16, LANE), f32)],
        compiler_params=_params("parallel", "parallel"),
    )(x, w, b)


def _chunk_index(d, n, n_ctx, n_all):
    rev = jnp.where(n < n_ctx, n_ctx - 1 - n, n_all + n_ctx - 1 - n)
    return jnp.where(d == 0, n, rev)


def _gate_layouts(raw, heads):
    bsz, seq, _ = raw.shape
    r = raw.reshape(bsz, seq // CHUNK, CHUNK, N_DIR, heads)
    return r.transpose(3, 0, 1, 2, 4), r.transpose(3, 0, 1, 4, 2)


def _param_layouts(*ps):
    return jnp.stack(ps, axis=1).astype(f32), jnp.stack(ps, axis=2).astype(f32)


def _gdn_kernel(q_ref, k_ref, v_ref, bcol_ref, brow_ref, dcol_ref, drow_ref, pcol_ref, prow_ref,
                o_ref, st_ref, *, bsz):
    d = pl.program_id(0)
    n = pl.program_id(1)

    @pl.when(n == 0)
    def _():
        st_ref[...] = jnp.zeros(st_ref.shape, f32)

    m_incl, m_strict, m_incl_t, m_eye = _chunk_masks(d)
    eye = m_eye.astype(f32)
    row = lax.broadcasted_iota(jnp.int32, (CHUNK, CHUNK), 0)
    col = lax.broadcasted_iota(jnp.int32, (CHUNK, CHUNK), 1)
    blocks = [(row >> sh) == (col >> sh) for sh in (3, 4, 5, 6)]
    off_masks = [blocks[lvl + 1] & ~blocks[lvl] for lvl in range(3)]
    pc = pcol_ref[0]
    pr = prow_ref[0]
    chains = [(b, h) for b in range(bsz) for h in range(A_HEADS)]
    idx = range(len(chains))
    beta_col = [_sigmoid(bcol_ref[0, b, 0]) for b in range(bsz)]
    g_col = [-jnp.exp(pc[0:1, :]) * _softplus(dcol_ref[0, b, 0] + pc[1:2, :]) for b in range(bsz)]
    g_row = [-jnp.exp(pr[:, 0:1]) * _softplus(drow_ref[0, b, 0] + pr[:, 1:2]) for b in range(bsz)]
    sl = [slice(h * A_DK, (h + 1) * A_DK) for _, h in chains]
    q = [q_ref[b, :, sl[c]] for c, (b, _) in enumerate(chains)]
    k = [k_ref[b, :, sl[c]] for c, (b, _) in enumerate(chains)]
    v = [v_ref[b, :, sl[c]] for c, (b, _) in enumerate(chains)]
    cums = [_cum(m_incl, m_incl_t, g_col[b][:, h:h + 1], g_row[b][h:h + 1, :]) for b, h in chains]
    decay = [jnp.exp(jnp.where(m_incl, gc - gr, -jnp.inf)) for gc, gr, _ in cums]
    b_col = [beta_col[b][:, h:h + 1] for b, h in chains]
    kb = [k[c] * b_col[c] for c in idx]
    l_mat = [_dot3(kb[c], k[c], NT) * jnp.where(m_strict, decay[c], 0.0) for c in idx]
    nil = [-jnp.where(blocks[0], l_mat[c], 0.0) for c in idx]
    t_inv = [eye + nil[c] for c in idx]
    for _ in range(2):
        nil = [_dot3(nil[c], nil[c]) for c in idx]
        t_inv = [t_inv[c] + _dot3(t_inv[c], nil[c]) for c in idx]
    for lvl in range(3):
        left = [_dot3(t_inv[c], jnp.where(off_masks[lvl], l_mat[c], 0.0)) for c in idx]
        t_inv = [t_inv[c] - _dot3(left[c], t_inv[c]) for c in idx]
    e_col = [jnp.exp(gc) for gc, _, _ in cums]
    uw = [_dot3(t_inv[c], jnp.concatenate([v[c] * b_col[c], kb[c] * e_col[c]], axis=1)) for c in idx]
    qk = [_dot(q[c], k[c], NT) * decay[c] for c in idx]
    k_dec_t = [(k[c] * jnp.exp(cums[c][2] - cums[c][0])).T for c in idx]
    s = [st_ref[c] for c in idx]
    wq = [_dot(jnp.concatenate([uw[c][:, A_DV:], q[c] * e_col[c]], axis=0), s[c]) for c in idx]
    v_new = [uw[c][:, :A_DV] - wq[c][:CHUNK] for c in idx]
    out = [wq[c][CHUNK:] + _dot(qk[c], v_new[c]) for c in idx]
    s_new = [jnp.exp(cums[c][2]) * s[c] + _dot(k_dec_t[c], v_new[c]) for c in idx]
    for c, (b, _) in enumerate(chains):
        o_ref[0, b, :, sl[c]] = out[c]
        st_ref[c] = s_new[c]


def _gdn(act, beta_raw, decay_raw, a_log, dt_bias, n_ctx):
    bsz, seq, _ = act.shape
    n_all = seq // CHUNK
    hw = A_HEADS * A_DK
    bcol, brow = _gate_layouts(beta_raw, A_HEADS)
    dcol, drow = _gate_layouts(decay_raw, A_HEADS)
    pcol, prow = _param_layouts(a_log, dt_bias)
    ci = functools.partial(_chunk_index, n_ctx=n_ctx, n_all=n_all)
    tok = lambda blk: pl.BlockSpec((bsz, CHUNK, hw), lambda d, n: (0, ci(d, n), blk))
    gcol = pl.BlockSpec((1, bsz, 1, CHUNK, A_HEADS), lambda d, n: (d, 0, ci(d, n), 0, 0))
    grow = pl.BlockSpec((1, bsz, 1, A_HEADS, CHUNK), lambda d, n: (d, 0, ci(d, n), 0, 0))
    return pl.pallas_call(
        functools.partial(_gdn_kernel, bsz=bsz),
        out_shape=jax.ShapeDtypeStruct((N_DIR, bsz, seq, hw), f32),
        grid=(N_DIR, n_all),
        in_specs=[tok(0), tok(1), tok(2), gcol, grow, gcol, grow,
                  pl.BlockSpec((1, 2, A_HEADS), lambda d, n: (d, 0, 0)),
                  pl.BlockSpec((1, A_HEADS, 2), lambda d, n: (d, 0, 0))],
        out_specs=pl.BlockSpec((1, bsz, CHUNK, hw), lambda d, n: (d, 0, ci(d, n), 0)),
        scratch_shapes=[pltpu.VMEM((bsz * A_HEADS, A_DK, A_DV), f32)],
        compiler_params=_params("arbitrary", "arbitrary"),
    )(act, act, act, bcol, brow, dcol, drow, pcol, prow)


def _ssd_kernel(x_ref, bm_ref, cm_ref, tcol_ref, trow_ref, pcol_ref, prow_ref, o_ref, st_ref, *, bsz):
    d = pl.program_id(0)
    n = pl.program_id(1)

    @pl.when(n == 0)
    def _():
        st_ref[...] = jnp.zeros(st_ref.shape, f32)

    m_incl, _, m_incl_t, _ = _chunk_masks(d)
    pc = pcol_ref[0]
    pr = prow_ref[0]
    rep = B_HEADS // B_GROUPS
    groups = [(b, g) for b in range(bsz) for g in range(B_GROUPS)]
    heads = [(b, h) for b in range(bsz) for h in range(B_HEADS)]
    gsl = [slice(g * B_STATE, (g + 1) * B_STATE) for _, g in groups]
    hsl = [slice(h * B_HEADDIM, (h + 1) * B_HEADDIM) for _, h in heads]
    dt_col = [_softplus(tcol_ref[0, b, 0] + pc[1:2, :]) for b in range(bsz)]
    dt_row = [_softplus(trow_ref[0, b, 0] + pr[:, 1:2]) for b in range(bsz)]
    da_col = [dt_col[b] * (-jnp.exp(pc[0:1, :])) for b in range(bsz)]
    da_row = [dt_row[b] * (-jnp.exp(pr[:, 0:1])) for b in range(bsz)]
    bm = [bm_ref[b, :, gsl[i]] for i, (b, _) in enumerate(groups)]
    cm = [cm_ref[b, :, gsl[i]] for i, (b, _) in enumerate(groups)]
    hs = [st_ref[i] for i in range(len(groups))]
    cb = [_dot(cm[i], bm[i], NT) for i in range(len(groups))]
    y_off = [_dot(cm[i], hs[i]) for i in range(len(groups))]
    bm_t = [bm[i].T for i in range(len(groups))]
    cums = [_cum(m_incl, m_incl_t, da_col[b][:, h:h + 1], da_row[b][h:h + 1, :]) for b, h in heads]
    seg = [jnp.exp(jnp.where(m_incl, ac - ar, -jnp.inf)) for ac, ar, _ in cums]
    xdt = [x_ref[b, :, hsl[j]] * dt_col[b][:, h:h + 1] for j, (b, h) in enumerate(heads)]
    y_diag = [_dot(cb[j // rep] * seg[j], xdt[j]) for j in range(len(heads))]
    for j, (b, h) in enumerate(heads):
        r = h % rep
        o_ref[0, b, :, hsl[j]] = y_diag[j] + jnp.exp(cums[j][0]) * y_off[j // rep][:, r * B_HEADDIM:(r + 1) * B_HEADDIM]
    xdt_end = [xdt[j] * jnp.exp(cums[j][2] - cums[j][0]) for j in range(len(heads))]
    keep = [jnp.broadcast_to(jnp.exp(cums[j][2]), (1, B_HEADDIM)) for j in range(len(heads))]
    upd = [_dot(bm_t[i], jnp.concatenate(xdt_end[i * rep:(i + 1) * rep], axis=1)) for i in range(len(groups))]
    for i in range(len(groups)):
        st_ref[i] = jnp.concatenate(keep[i * rep:(i + 1) * rep], axis=1) * hs[i] + upd[i]


def _ssd(act, dt_raw, a_log, dt_bias, n_ctx):
    bsz, seq, _ = act.shape
    n_all = seq // CHUNK
    dinner = B_HEADS * B_HEADDIM
    gw = B_GROUPS * B_STATE
    tcol, trow = _gate_layouts(dt_raw, B_HEADS)
    pcol, prow = _param_layouts(a_log, dt_bias)
    ci = functools.partial(_chunk_index, n_ctx=n_ctx, n_all=n_all)
    x_off = (A_HEADS * (2 * A_DK + A_DV)) // dinner
    bm_off = (A_HEADS * (2 * A_DK + A_DV) + dinner) // gw
    return pl.pallas_call(
        functools.partial(_ssd_kernel, bsz=bsz),
        out_shape=jax.ShapeDtypeStruct((N_DIR, bsz, seq, dinner), f32),
        grid=(N_DIR, n_all),
        in_specs=[pl.BlockSpec((bsz, CHUNK, dinner), lambda d, n: (0, ci(d, n), x_off)),
                  pl.BlockSpec((bsz, CHUNK, gw), lambda d, n: (0, ci(d, n), bm_off)),
                  pl.BlockSpec((bsz, CHUNK, gw), lambda d, n: (0, ci(d, n), bm_off + 1)),
                  pl.BlockSpec((1, bsz, 1, CHUNK, B_HEADS), lambda d, n: (d, 0, ci(d, n), 0, 0)),
                  pl.BlockSpec((1, bsz, 1, B_HEADS, CHUNK), lambda d, n: (d, 0, ci(d, n), 0, 0)),
                  pl.BlockSpec((1, 2, B_HEADS), lambda d, n: (d, 0, 0)),
                  pl.BlockSpec((1, B_HEADS, 2), lambda d, n: (d, 0, 0))],
        out_specs=pl.BlockSpec((1, bsz, CHUNK, dinner), lambda d, n: (d, 0, ci(d, n), 0)),
        scratch_shapes=[pltpu.VMEM((bsz * B_GROUPS, B_STATE, (B_HEADS // B_GROUPS) * B_HEADDIM), f32)],
        compiler_params=_params("arbitrary", "arbitrary"),
    )(act, act, act, tcol, trow, pcol, prow)


def _mlstm_kernel(q_ref, k_ref, v_ref, icol_ref, irow_ref, fcol_ref, frow_ref, pcol_ref, prow_ref,
                  o_ref, c_ref, n_ref, m_ref, *, bsz):
    d = pl.program_id(0)
    n = pl.program_id(1)

    @pl.when(n == 0)
    def _():
        c_ref[...] = jnp.zeros(c_ref.shape, f32)
        n_ref[...] = jnp.zeros(n_ref.shape, f32)
        m_ref[...] = jnp.zeros(m_ref.shape, f32)

    m_incl, _, m_incl_t, _ = _chunk_masks(d)
    pc = pcol_ref[0]
    pr = prow_ref[0]
    chains = [(b, h) for b in range(bsz) for h in range(C_HEADS)]
    idx = range(len(chains))
    li_col = [icol_ref[0, b, 0] + pc[0:1, :] for b in range(bsz)]
    li_row = [irow_ref[0, b, 0] + pr[:, 0:1] for b in range(bsz)]
    lf_col = [-_softplus(-(fcol_ref[0, b, 0] + pc[1:2, :])) for b in range(bsz)]
    lf_row = [-_softplus(-(frow_ref[0, b, 0] + pr[:, 1:2])) for b in range(bsz)]
    q = [q_ref[b, :, h * C_DK:(h + 1) * C_DK] for b, h in chains]
    k = [k_ref[b, :, h * C_DK:(h + 1) * C_DK] for b, h in chains]
    v = [v_ref[b, :, h * C_DV:(h + 1) * C_DV] for b, h in chains]
    cums = [_cum(m_incl, m_incl_t, lf_col[b][:, h:h + 1], lf_row[b][h:h + 1, :]) for b, h in chains]
    logw_intra = [jnp.where(m_incl, cums[c][0] - cums[c][1] + li_row[b][h:h + 1, :], -jnp.inf)
                  for c, (b, h) in enumerate(chains)]
    m_prev = [m_ref[c] for c in idx]
    logw_inter = [cums[c][0] + m_prev[c] for c in idx]
    m_t = [jnp.maximum(logw_inter[c], jnp.max(logw_intra[c], axis=1, keepdims=True)) for c in idx]
    qk = [_dot(q[c], k[c], NT) for c in idx]
    c_mem = [c_ref[c] for c in idx]
    n_mem = [n_ref[c] for c in idx]
    qc = [_dot(q[c], c_mem[c]) for c in idx]
    s = [qk[c] * jnp.exp(logw_intra[c] - m_t[c]) for c in idx]
    w_inter = [jnp.exp(logw_inter[c] - m_t[c]) for c in idx]
    num = [_dot(s[c], v[c]) + w_inter[c] * qc[c] for c in idx]
    den = [jnp.sum(s[c], axis=1, keepdims=True) + w_inter[c] * jnp.sum(q[c] * n_mem[c], axis=1, keepdims=True)
           for c in idx]
    for c, (b, h) in enumerate(chains):
        o_ref[0, b, :, h * C_DV:(h + 1) * C_DV] = num[c] / jnp.maximum(jnp.abs(den[c]), jnp.exp(-m_t[c]))
    logw_end = [cums[c][2] - cums[c][0] + li_col[b][:, h:h + 1] for c, (b, h) in enumerate(chains)]
    m_new = [jnp.maximum(cums[c][2] + m_prev[c], jnp.max(logw_end[c], axis=0, keepdims=True)) for c in idx]
    kw = [k[c] * jnp.exp(logw_end[c] - m_new[c]) for c in idx]
    keep = [jnp.exp(cums[c][2] + m_prev[c] - m_new[c]) for c in idx]
    kv = [_dot(kw[c].T, v[c]) for c in idx]
    for c in idx:
        c_ref[c] = keep[c] * c_mem[c] + kv[c]
        n_ref[c] = keep[c] * n_mem[c] + jnp.sum(kw[c], axis=0, keepdims=True)
        m_ref[c] = m_new[c]


def _mlstm(act, proj, i_raw, f_raw, i_bias, f_bias, n_ctx):
    bsz, seq, _ = act.shape
    n_all = seq // CHUNK
    qw = C_HEADS * C_DK
    vw = C_HEADS * C_DV
    icol, irow = _gate_layouts(i_raw, C_HEADS)
    fcol, frow = _gate_layouts(f_raw, C_HEADS)
    pcol, prow = _param_layouts(i_bias, f_bias)
    ci = functools.partial(_chunk_index, n_ctx=n_ctx, n_all=n_all)
    gcol = pl.BlockSpec((1, bsz, 1, CHUNK, C_HEADS), lambda d, n: (d, 0, ci(d, n), 0, 0))
    grow = pl.BlockSpec((1, bsz, 1, C_HEADS, CHUNK), lambda d, n: (d, 0, ci(d, n), 0, 0))
    return pl.pallas_call(
        functools.partial(_mlstm_kernel, bsz=bsz),
        out_shape=jax.ShapeDtypeStruct((N_DIR, bsz, seq, vw), f32),
        grid=(N_DIR, n_all),
        in_specs=[pl.BlockSpec((bsz, CHUNK, qw), lambda d, n: (0, ci(d, n), 0)),
                  pl.BlockSpec((bsz, CHUNK, qw), lambda d, n: (0, ci(d, n), 1)),
                  pl.BlockSpec((bsz, CHUNK, vw), lambda d, n: (0, ci(d, n), 2 * qw // vw)),
                  gcol, grow, gcol, grow,
                  pl.BlockSpec((1, 2, C_HEADS), lambda d, n: (d, 0, 0)),
                  pl.BlockSpec((1, C_HEADS, 2), lambda d, n: (d, 0, 0))],
        out_specs=pl.BlockSpec((1, bsz, CHUNK, vw), lambda d, n: (d, 0, ci(d, n), 0)),
        scratch_shapes=[pltpu.VMEM((bsz * C_HEADS, C_DK, C_DV), f32),
                        pltpu.VMEM((bsz * C_HEADS, 1, C_DK), f32),
                        pltpu.VMEM((bsz * C_HEADS, 1, 1), f32)],
        compiler_params=_params("arbitrary", "arbitrary"),
    )(act, act, proj, icol, irow, fcol, frow, pcol, prow)


def _merge_even_kernel(x_ref, a_ref, b_ref, za_ref, zb_ref, xs_ref, na_ref, nb_ref, dsk_ref, w_ref,
                       g_ref, lg_ref, lb_ref, o_ref):
    a = a_ref[0] + a_ref[1]
    za = za_ref[...]
    na = na_ref[...]
    parts = []
    for h in range(A_HEADS):
        sl = slice(h * A_DV, (h + 1) * A_DV)
        parts.append(_rms(a[:, sl], na) * _silu(za[:, sl]))
    yb = (b_ref[0] + b_ref[1] + dsk_ref[...] * xs_ref[...]) * _silu(zb_ref[...])
    gw = (B_HEADS * B_HEADDIM) // B_GROUPS
    nb = nb_ref[...]
    for g in range(B_GROUPS):
        sl = slice(g * gw, (g + 1) * gw)
        parts.append(_rms(yb[:, sl], nb[:, sl]))
    y = jnp.concatenate(parts, axis=1).astype(bf16)
    z = ALPHA * x_ref[...] + g_ref[0] * _dot(y, w_ref[...])
    o_ref[...] = _layer_norm(z, lg_ref[...], lb_ref[...])


def _merge_even(x, o_a, o_b, proj, act, norm_a, norm_b, d_skip, w_out, gate_t, ln_g, ln_b):
    rows, d = x.shape
    aw = A_HEADS * A_DV
    bw = B_HEADS * B_HEADDIM
    row = lambda w_, blk: pl.BlockSpec((ROW_TILE, w_), lambda i: (i, blk))
    vec = lambda w_: pl.BlockSpec((1, w_), lambda i: (0, 0))
    qkv = A_HEADS * (2 * A_DK + A_DV)
    return pl.pallas_call(
        _merge_even_kernel,
        out_shape=jax.ShapeDtypeStruct((rows, d), f32),
        grid=(rows // ROW_TILE,),
        in_specs=[row(d, 0),
                  pl.BlockSpec((N_DIR, ROW_TILE, aw), lambda i: (0, i, 0)),
                  pl.BlockSpec((N_DIR, ROW_TILE, bw), lambda i: (0, i, 0)),
                  row(aw, qkv // aw), row(bw, (qkv + aw) // bw), row(bw, qkv // bw),
                  vec(A_DV), vec(bw), vec(bw),
                  pl.BlockSpec((aw + bw, d), lambda i: (0, 0)),
                  pl.BlockSpec((1, 1, d), lambda i: (i, 0, 0)),
                  vec(d), vec(d)],
        out_specs=row(d, 0),
        compiler_params=_params("parallel"),
    )(x, o_a, o_b, proj, proj, act, norm_a.reshape(1, -1), norm_b.reshape(1, -1),
      jnp.repeat(d_skip, B_HEADDIM).reshape(1, -1), w_out, gate_t, ln_g.reshape(1, -1), ln_b.reshape(1, -1))


def _merge_odd_kernel(h_ref, o_ref_in, nc_ref, w_ref, out_ref):
    hsum = h_ref[0] + h_ref[1]
    o = o_ref_in[...]
    nc = nc_ref[...]
    parts = []
    for h in range(C_HEADS):
        sl = slice(h * C_DV, (h + 1) * C_DV)
        parts.append(_rms(hsum[:, sl], nc) * _sigmoid(o[:, sl]))
    y = jnp.concatenate(parts, axis=1).astype(bf16)
    out_ref[...] = _dot(y, w_ref[...])


def _merge_odd(h, proj, norm_c, w_out, tiles_per_batch, ctx_tiles, bsz):
    vw = C_HEADS * C_DV
    lat_tiles = tiles_per_batch - ctx_tiles
    src = lambda i: i + ctx_tiles * (i // lat_tiles + 1)
    d = w_out.shape[1]
    return pl.pallas_call(
        _merge_odd_kernel,
        out_shape=jax.ShapeDtypeStruct((bsz * lat_tiles * ROW_TILE, d), f32),
        grid=(bsz * lat_tiles,),
        in_specs=[pl.BlockSpec((N_DIR, ROW_TILE, vw), lambda i: (0, src(i), 0)),
                  pl.BlockSpec((ROW_TILE, vw), lambda i: (src(i), 2 * C_HEADS * C_DK // vw + 1)),
                  pl.BlockSpec((1, C_DV), lambda i: (0, 0)),
                  pl.BlockSpec((vw, d), lambda i: (0, 0))],
        out_specs=pl.BlockSpec((ROW_TILE, d), lambda i: (i, 0)),
        compiler_params=_params("parallel"),
    )(h, proj, norm_c.reshape(1, -1), w_out)


def _resid_ln_kernel(x_ref, y_ref, g_ref, lg_ref, lb_ref, o_ref, *, transposed):
    y = y_ref[...]
    if transposed:
        y = y.T
    o_ref[...] = _layer_norm(ALPHA * x_ref[...] + g_ref[0] * y, lg_ref[...], lb_ref[...])


def _resid_ln(x, y, gate_t, ln_g, ln_b, *, transposed):
    rows, d = x.shape
    y_spec = (pl.BlockSpec((d, ROW_TILE), lambda i: (0, i)) if transposed
              else pl.BlockSpec((ROW_TILE, d), lambda i: (i, 0)))
    return pl.pallas_call(
        functools.partial(_resid_ln_kernel, transposed=transposed),
        out_shape=jax.ShapeDtypeStruct((rows, d), f32),
        grid=(rows // ROW_TILE,),
        in_specs=[pl.BlockSpec((ROW_TILE, d), lambda i: (i, 0)), y_spec,
                  pl.BlockSpec((1, 1, d), lambda i: (i, 0, 0)),
                  pl.BlockSpec((1, d), lambda i: (0, 0)),
                  pl.BlockSpec((1, d), lambda i: (0, 0))],
        out_specs=pl.BlockSpec((ROW_TILE, d), lambda i: (i, 0)),
        compiler_params=_params("parallel"),
    )(x, y, gate_t, ln_g.reshape(1, -1), ln_b.reshape(1, -1))


def _oddeven_merge(lo, hi, r):
    step = r * 2
    if step < hi - lo:
        yield from _oddeven_merge(lo, hi, step)
        yield from _oddeven_merge(lo + r, hi, step)
        yield from [(i, i + r) for i in range(lo + r, hi - r, step)]
    else:
        yield (lo, lo + r)


def _oddeven_sort(lo, hi):
    if hi - lo >= 1:
        mid = lo + (hi - lo) // 2
        yield from _oddeven_sort(lo, mid)
        yield from _oddeven_sort(mid + 1, hi)
        yield from _oddeven_merge(lo, hi, 1)


def _exchange(a, i, j):
    a[i], a[j] = jnp.maximum(a[i], a[j]), jnp.minimum(a[i], a[j])


def _top_sorted(slabs):
    a = list(slabs)
    for i, j in _oddeven_sort(0, len(a) - 1):
        _exchange(a, i, j)
    for shift in (4, 2, 1):
        other = [pltpu.roll(x, shift, axis=0) for x in a]
        if len(a) < PEER_TOPK:
            a = a + other[::-1]
        else:
            a = [jnp.maximum(a[k], other[PEER_TOPK - 1 - k]) for k in range(PEER_TOPK)]
        dist = PEER_TOPK // 2
        while dist >= 1:
            for i in range(PEER_TOPK):
                if i & dist == 0:
                    _exchange(a, i, i + dist)
            dist //= 2
    return a


def _next_largest(slabs, kth):
    count = sum(jnp.where(s >= kth, 1.0, 0.0) for s in slabs)
    below = functools.reduce(jnp.maximum, [jnp.where(s < kth, s, -jnp.inf) for s in slabs])
    count = jnp.sum(count, axis=0, keepdims=True)
    below = jnp.max(below, axis=0, keepdims=True)
    return jnp.where(count > PEER_TOPK, kth, below)


def _top_values(s):
    slabs = [s[i:i + 8] for i in range(0, s.shape[0], 8)]
    if len(slabs) < 8:
        slabs = slabs + [jnp.full_like(slabs[0], -jnp.inf)] * (8 - len(slabs))
    top = [t[0:1] for t in _top_sorted(slabs)]
    return top + [_next_largest(slabs, top[-1])]


def _route_kernel(x_ref, sh_ref, sc_ref, wh_ref, wl_ref, kh_ref, kl_ref,
                  xm_ref, thr_ref, s2_ref, e1_ref, e2_ref, xh_ref, xl_ref, sc_scr, cand_ref):
    xm = x_ref[...] * (1.0 + sc_ref[0]) + sh_ref[0]
    xm_ref[...] = xm.astype(bf16)
    xh_ref[...], xl_ref[...] = _split(xm)
    half = PEER_DKEY // 2
    n_top = PEER_TOPK + 1
    pairs = [(p, r) for p in range(n_top) for r in range(n_top) if (p + 1) * (r + 1) <= n_top]
    cand_ref[...] = jnp.full(cand_ref.shape, -jnp.inf, f32)

    def head(h, carry):
        wsl = pl.ds(pl.multiple_of(h * PEER_DKEY, PEER_DKEY), PEER_DKEY)
        xh = xh_ref[...]
        q = _dot(xh, wh_ref[:, wsl]) + (_dot(xh, wl_ref[:, wsl]) + _dot(xl_ref[...], wh_ref[:, wsl]))
        for c in range(2):
            qh, ql = _split(q[:, c * half:(c + 1) * half])
            kh = kh_ref[h, c]
            sc_scr[c] = _dot(kh, qh, NT) + (_dot(kl_ref[h, c], qh, NT) + _dot(kh, ql, NT))
        for lb in range(ROW_TILE // LANE):
            lanes = slice(lb * LANE, (lb + 1) * LANE)
            s1 = sc_scr[0, :, lanes]
            s2 = sc_scr[1, :, lanes]
            t1 = _top_values(s1)
            t2 = _top_values(s2)
            for i, (p, r) in enumerate(pairs):
                cand_ref[i:i + 1, lanes] = t1[p] + t2[r]
            best = _top_values(cand_ref[:, lanes])
            z = sum(jnp.exp(bv - best[0]) for bv in best[:PEER_TOPK])
            tau = 0.5 * (best[PEER_TOPK - 1] + best[PEER_TOPK])
            thr_ref[h, :, lanes] = tau - s1
            s2_ref[h, :, lanes] = s2
            e1_ref[h, :, lanes] = jnp.exp(s1 - t1[0]) / z
            e2_ref[h, :, lanes] = jnp.exp(s2 - t2[0])
        return carry

    lax.fori_loop(0, PEER_HEADS, head, 0)


def _route(x, shift_t, scale_t, wq_hi, wq_lo, keys_hi, keys_lo):
    rows, d = x.shape
    nq = wq_hi.shape[1]
    half = PEER_DKEY // 2
    score = jax.ShapeDtypeStruct((PEER_HEADS, PEER_NKEYS, rows), f32)
    score_spec = pl.BlockSpec((PEER_HEADS, PEER_NKEYS, ROW_TILE), lambda i: (0, 0, i))
    key_spec = pl.BlockSpec((PEER_HEADS, 2, PEER_NKEYS, half), lambda i: (0, 0, 0, 0))
    return pl.pallas_call(
        _route_kernel,
        out_shape=(jax.ShapeDtypeStruct((rows, d), bf16), score, score, score, score),
        grid=(rows // ROW_TILE,),
        in_specs=[pl.BlockSpec((ROW_TILE, d), lambda i: (i, 0)),
                  pl.BlockSpec((1, 1, d), lambda i: (i, 0, 0)),
                  pl.BlockSpec((1, 1, d), lambda i: (i, 0, 0)),
                  pl.BlockSpec((d, nq), lambda i: (0, 0)),
                  pl.BlockSpec((d, nq), lambda i: (0, 0)),
                  key_spec, key_spec],
        out_specs=(pl.BlockSpec((ROW_TILE, d), lambda i: (i, 0)),
                   score_spec, score_spec, score_spec, score_spec),
        scratch_shapes=[pltpu.VMEM((ROW_TILE, d), bf16), pltpu.VMEM((ROW_TILE, d), bf16),
                        pltpu.VMEM((2, PEER_NKEYS, ROW_TILE), f32),
                        pltpu.VMEM((N_CAND_ROWS, ROW_TILE), f32)],
        compiler_params=_params("parallel"),
    )(x, shift_t, scale_t, wq_hi, wq_lo, keys_hi, keys_lo)


def _expert_kernel(xm_ref, u_ref, vt_ref, thr_ref, e1_ref, s2_ref, e2_ref, o_ref, a_scr, w_scr):
    j = pl.program_id(1)

    @pl.when(j == 0)
    def _():
        o_ref[...] = jnp.zeros(o_ref.shape, f32)

    a_scr[...] = _dot(u_ref[...], xm_ref[...], NT)
    sqrt_half = math.sqrt(0.5)
    n_i1 = EXPERT_BLK // PEER_NKEYS
    row_chunks = PEER_NKEYS // GATE_ROWS

    def tile(t, carry):
        lanes = pl.ds(pl.multiple_of((t // row_chunks) * LANE, LANE), LANE)
        r0 = pl.multiple_of((t % row_chunks) * GATE_ROWS, GATE_ROWS)
        i2 = pl.ds(r0, GATE_ROWS)
        gate = [jnp.zeros((GATE_ROWS, LANE), f32) for _ in range(n_i1)]
        for h in range(PEER_HEADS):
            s2 = s2_ref[h, i2, lanes]
            e2 = e2_ref[h, i2, lanes]
            for il in range(n_i1):
                w = e1_ref[h, il:il + 1, lanes] * e2
                gate[il] = gate[il] + jnp.where(s2 >= thr_ref[h, il:il + 1, lanes], w, 0.0)
        for il in range(n_i1):
            rows = pl.ds(il * PEER_NKEYS + r0, GATE_ROWS)
            a = a_scr[rows, lanes]
            act = 0.5 * a * (1.0 + lax.erf(a * sqrt_half))
            w_scr[rows, lanes] = (gate[il] * act).astype(bf16)
        return carry

    lax.fori_loop(0, (EXPERT_TOK // LANE) * row_chunks, tile, 0)
    o_ref[...] += _dot(vt_ref[...], w_scr[...])


def _experts(xm, u_bf, vt_bf, thr, e1, s2, e2):
    rows, d = xm.shape
    n_exp = u_bf.shape[0]
    i1_per_blk = EXPERT_BLK // PEER_NKEYS
    sel = pl.BlockSpec((PEER_HEADS, i1_per_blk, EXPERT_TOK), lambda i, j: (0, j, i))
    full = pl.BlockSpec((PEER_HEADS, PEER_NKEYS, EXPERT_TOK), lambda i, j: (0, 0, i))
    return pl.pallas_call(
        _expert_kernel,
        out_shape=jax.ShapeDtypeStruct((d, rows), f32),
        grid=(rows // EXPERT_TOK, n_exp // EXPERT_BLK),
        in_specs=[pl.BlockSpec((EXPERT_TOK, d), lambda i, j: (i, 0)),
                  pl.BlockSpec((EXPERT_BLK, d), lambda i, j: (j, 0)),
                  pl.BlockSpec((d, EXPERT_BLK), lambda i, j: (0, j)),
                  sel, sel, full, full],
        out_specs=pl.BlockSpec((d, EXPERT_TOK), lambda i, j: (0, i)),
        scratch_shapes=[pltpu.VMEM((EXPERT_BLK, EXPERT_TOK), f32),
                        pltpu.VMEM((EXPERT_BLK, EXPERT_TOK), bf16)],
        compiler_params=_params("parallel", "arbitrary"),
    )(xm, u_bf, vt_bf, thr, e1, s2, e2)


def _peer_residual(x, shift_t, scale_t, gate_t, peer_q, peer_keys, peer_u, peer_v, ln_g, ln_b):
    wq_hi, wq_lo = _split(peer_q)
    keys_hi, keys_lo = _split(peer_keys)
    xm, thr, s2, e1, e2 = _route(x, shift_t, scale_t, wq_hi, wq_lo, keys_hi, keys_lo)
    y_t = _experts(xm, peer_u.astype(bf16), peer_v.T.astype(bf16), thr, e1, s2, e2)
    return _resid_ln(x, y_t, gate_t, ln_g, ln_b, transposed=True)


def _grid_sincos(n_tokens, d):
    t = jnp.arange(n_tokens)
    row = (t // GRID_W).astype(f32)[:, None]
    col = (t % GRID_W).astype(f32)[:, None]
    n_freq = d // 4
    freq = jnp.exp(-math.log(POS_BASE) * jnp.arange(n_freq, dtype=f32) / n_freq)[None, :]
    return jnp.concatenate([jnp.sin(row * freq), jnp.cos(row * freq), jnp.sin(col * freq), jnp.cos(col * freq)], -1)


def _tile_rows(mod, src):
    d = mod.shape[1] // 6
    rows = jnp.concatenate([jnp.broadcast_to(mod[r:r + 1], (n, mod.shape[1])) for r, n in src], axis=0)
    return [rows[:, None, i * d:(i + 1) * d] for i in range(6)]


def _pad_cols(w, n):
    return jnp.pad(w, ((0, 0), (0, n - w.shape[1])))


def kernel(x, c, ctx, c_ctx, l0_ada_w, l0_ada_b, l0_ln1_g, l0_ln1_b, l0_w_in, l0_conv_a, l0_a_log_a, l0_dt_bias_a, l0_norm_a, l0_conv_b, l0_conv_b_bias, l0_a_log_b, l0_dt_bias_b, l0_d_skip_b, l0_norm_b, l0_w_out, l0_ln2_g, l0_ln2_b, l0_peer_q, l0_peer_keys, l0_peer_u, l0_peer_v, l1_ada_w, l1_ada_b, l1_ln1_g, l1_ln1_b, l1_w_in, l1_conv_c, l1_i_bias, l1_f_bias, l1_norm_c, l1_w_out, l1_ln2_g, l1_ln2_b, l1_peer_q, l1_peer_keys, l1_peer_u, l1_peer_v):
    bsz, n_lat, d = x.shape
    n_ctx_tok = ctx.shape[1]
    rows_grid = n_lat // GRID_W
    seq = n_ctx_tok + n_lat
    n_ctx = n_ctx_tok // CHUNK
    tiles_b = seq // ROW_TILE
    ctx_tiles = n_ctx_tok // ROW_TILE
    lat_tiles = n_lat // ROW_TILE

    cond = jnp.concatenate([c, c_ctx[None, :], jnp.zeros((8 - bsz - 1, d), f32)], axis=0)
    src_all = [run for b in range(bsz) for run in ((bsz, ctx_tiles), (b, lat_tiles))]

    x_lat = x + _grid_sincos(n_lat, d).astype(x.dtype)
    xs_all = jnp.concatenate([ctx, x_lat], axis=1).reshape(bsz * seq, d)

    sh1, sc1, g1, sh2, sc2, g2 = _tile_rows(_ada(cond, l0_ada_w, l0_ada_b), src_all)
    qkv_w = A_HEADS * (2 * A_DK + A_DV)
    az = A_HEADS * A_DV
    ag = N_DIR * A_HEADS
    bd = B_HEADS * B_HEADDIM
    bx = bd + 2 * B_GROUPS * B_STATE
    bt = N_DIR * B_HEADS
    o0 = np.cumsum([0, qkv_w, az, ag, ag, bd, bx, bt])
    w_qkv, w_za, w_beta, w_dec, w_zb, w_xbc, w_dt = (l0_w_in[:, o0[i]:o0[i + 1]] for i in range(7))
    main_w = qkv_w + az + bd + bx
    w0 = _pad_cols(jnp.concatenate([w_qkv, w_za, w_zb, w_xbc, w_beta, w_dec, w_dt], axis=1), main_w + LANE)
    proj = _proj(xs_all, sh1, sc1, w0.astype(bf16)).reshape(bsz, seq, main_w + LANE)
    conv_w = jnp.concatenate([l0_conv_a, l0_conv_b], axis=1)
    conv_b = jnp.concatenate([jnp.zeros((qkv_w,), f32), l0_conv_b_bias]).reshape(1, -1)
    xbc_blk = (qkv_w + az + bd) // LANE
    col_blocks = list(range(qkv_w // LANE)) + list(range(xbc_blk, xbc_blk + bx // LANE))
    act = _conv_act(proj, conv_w, conv_b, col_blocks, seg0=n_ctx_tok,
                    n_norm=2 * A_HEADS, n_qscale=A_HEADS, qscale=A_DK ** -0.5)
    gates = proj[:, :, main_w:]
    o_a = _gdn(act, gates[..., :ag], gates[..., ag:2 * ag], l0_a_log_a, l0_dt_bias_a, n_ctx)
    o_b = _ssd(act, gates[..., 2 * ag:2 * ag + bt], l0_a_log_b, l0_dt_bias_b, n_ctx)
    x1 = _merge_even(xs_all, o_a.reshape(N_DIR, bsz * seq, az), o_b.reshape(N_DIR, bsz * seq, bd),
                     proj.reshape(bsz * seq, -1), act.reshape(bsz * seq, -1),
                     l0_norm_a, l0_norm_b, l0_d_skip_b, l0_w_out.astype(bf16), g1, l0_ln1_g, l0_ln1_b)
    x2 = _peer_residual(x1, sh2, sc2, g2, l0_peer_q, l0_peer_keys, l0_peer_u, l0_peer_v, l0_ln2_g, l0_ln2_b)

    sh1, sc1, g1, sh2, sc2, g2 = _tile_rows(_ada(cond, l1_ada_w, l1_ada_b), src_all)
    x2 = x2.reshape(bsz, seq, d)
    x2_ctx, x2_lat = x2[:, :n_ctx_tok], x2[:, n_ctx_tok:]
    lat_cm = x2_lat.reshape(bsz, rows_grid, GRID_W, d).transpose(0, 2, 1, 3).reshape(bsz, n_lat, d)
    xs1 = jnp.concatenate([x2_ctx, lat_cm], axis=1).reshape(bsz * seq, d)
    qk_w = 2 * C_HEADS * C_DK
    vw = C_HEADS * C_DV
    w1 = _pad_cols(l1_w_in, qk_w + 2 * vw + LANE)
    proj1 = _proj(xs1, sh1, sc1, w1.astype(bf16)).reshape(bsz, seq, -1)
    act1 = _conv_act(proj1, l1_conv_c, jnp.zeros((1, qk_w), f32), list(range(qk_w // LANE)), seg0=n_ctx_tok,
                     n_norm=0, n_qscale=C_HEADS, qscale=C_DK ** -0.5)
    gates1 = proj1[:, :, qk_w + 2 * vw:]
    cg = N_DIR * C_HEADS
    h1 = _mlstm(act1, proj1, gates1[..., :cg], gates1[..., cg:2 * cg], l1_i_bias, l1_f_bias, n_ctx)
    y_cm = _merge_odd(h1.reshape(N_DIR, bsz * seq, vw), proj1.reshape(bsz * seq, -1), l1_norm_c,
                      l1_w_out.astype(bf16), tiles_b, ctx_tiles, bsz)
    y_lat = y_cm.reshape(bsz, GRID_W, rows_grid, d).transpose(0, 2, 1, 3).reshape(bsz * n_lat, d)
    lat_sel = lambda t: t.reshape(bsz, tiles_b, 1, d)[:, ctx_tiles:].reshape(bsz * lat_tiles, 1, d)
    x3 = _resid_ln(x2_lat.reshape(bsz * n_lat, d), y_lat, lat_sel(g1), l1_ln1_g, l1_ln1_b, transposed=False)
    x4 = _peer_residual(x3, lat_sel(sh2), lat_sel(sc2), lat_sel(g2), l1_peer_q, l1_peer_keys, l1_peer_u,
                        l1_peer_v, l1_ln2_g, l1_ln2_b)
    return x4.reshape(bsz, n_lat, d)
```

```python
import functools
import math

import numpy as np
import jax
import jax.numpy as jnp
from jax import lax
from jax.experimental import pallas as pl
from jax.experimental.pallas import tpu as pltpu

f32 = jnp.float32
bf16 = jnp.bfloat16

D_MODEL = 1024
GRID_W = 64
CHUNK = 64
CONV_W = 5
POS_BASE = 10000.0
EPS = 1e-6
N_DIR = 2
DEPTH = 2
ALPHA = (2 * DEPTH) ** 0.25

A_HEADS, A_DK, A_DV = 4, 128, 128
B_HEADS, B_HEADDIM, B_GROUPS, B_STATE = 8, 64, 2, 128
C_HEADS, C_DK, C_DV = 4, 128, 256
PEER_HEADS, PEER_NKEYS, PEER_DKEY, PEER_TOPK = 8, 128, 256, 16

LANE = 128
ROW_TILE = 256
ROUTE_TILE = 512
EXPERT_TOK = 512
EXPERT_BLK = 1024
GATE_ROWS = 32
VMEM_LIMIT = 56 * 1024 * 1024
N_CAND_ROWS = -(-sum((PEER_TOPK + 1) // (p + 1) for p in range(PEER_TOPK + 1)) // 8) * 8

NN = (((1,), (0,)), ((), ()))
NT = (((1,), (1,)), ((), ()))


def _params(*sem):
    return pltpu.CompilerParams(dimension_semantics=sem, vmem_limit_bytes=VMEM_LIMIT)


def _dot(a, b, dims=NN):
    return lax.dot_general(a, b, dims, preferred_element_type=f32)


def _split(a):
    hi = a.astype(bf16)
    return hi, (a - hi.astype(f32)).astype(bf16)


def _dot3(a, b, dims=NN):
    ah, al = _split(a)
    bh, bl = _split(b)
    return _dot(ah, bh, dims) + (_dot(ah, bl, dims) + _dot(al, bh, dims))


def _sigmoid(x):
    return 1.0 / (1.0 + jnp.exp(-x))


def _silu(x):
    return x * _sigmoid(x)


def _softplus(x):
    return jnp.maximum(x, 0.0) + jnp.log1p(jnp.exp(-jnp.abs(x)))


def _layer_norm(z, g, b):
    mu = jnp.mean(z, -1, keepdims=True)
    zc = z - mu
    var = jnp.mean(zc * zc, -1, keepdims=True)
    return zc * lax.rsqrt(var + EPS) * g + b


def _rms(x, g):
    return x * lax.rsqrt(jnp.mean(x * x, -1, keepdims=True) + EPS) * g


def _chunk_masks(d):
    row = lax.broadcasted_iota(jnp.int32, (CHUNK, CHUNK), 0)
    col = lax.broadcasted_iota(jnp.int32, (CHUNK, CHUNK), 1)
    diff = jnp.where(d == 0, row - col, col - row)
    return diff >= 0, diff > 0, diff <= 0, row == col


def _cum(mask_incl, mask_incl_t, x_col, x_row):
    c_col = jnp.sum(jnp.where(mask_incl, x_row, 0.0), axis=1, keepdims=True)
    c_row = jnp.sum(jnp.where(mask_incl_t, x_col, 0.0), axis=0, keepdims=True)
    tot = jnp.sum(x_col, axis=0, keepdims=True)
    return c_col, c_row, tot


def _ada_kernel(c_ref, w_ref, b_ref, o_ref):
    o_ref[...] = _dot3(_silu(c_ref[...]), w_ref[...]) + b_ref[...]


def _ada(cond, w, b):
    rows, d = cond.shape
    n = w.shape[1]
    return pl.pallas_call(
        _ada_kernel,
        out_shape=jax.ShapeDtypeStruct((rows, n), f32),
        grid=(n // d,),
        in_specs=[pl.BlockSpec((rows, d), lambda j: (0, 0)),
                  pl.BlockSpec((d, d), lambda j: (0, j)),
                  pl.BlockSpec((1, d), lambda j: (0, j))],
        out_specs=pl.BlockSpec((rows, d), lambda j: (0, j)),
        compiler_params=_params("parallel"),
    )(cond, w, b.reshape(1, n))


def _proj_kernel(x_ref, sh_ref, sc_ref, w_ref, o_ref):
    xm = x_ref[...] * (1.0 + sc_ref[0]) + sh_ref[0]
    o_ref[...] = _dot(xm.astype(bf16), w_ref[...])


def _proj(x, shift_t, scale_t, w):
    rows, d = x.shape
    n = w.shape[1]
    return pl.pallas_call(
        _proj_kernel,
        out_shape=jax.ShapeDtypeStruct((rows, n), f32),
        grid=(rows // ROW_TILE,),
        in_specs=[pl.BlockSpec((ROW_TILE, d), lambda i: (i, 0)),
                  pl.BlockSpec((1, 1, d), lambda i: (i, 0, 0)),
                  pl.BlockSpec((1, 1, d), lambda i: (i, 0, 0)),
                  pl.BlockSpec((d, n), lambda i: (0, 0))],
        out_specs=pl.BlockSpec((ROW_TILE, n), lambda i: (i, 0)),
        compiler_params=_params("parallel"),
    )(x, shift_t, scale_t, w)


def _conv_kernel(x_ref, w_ref, b_ref, o_ref, pad_ref, *, seq, seg0, n_norm, n_qscale, qscale):
    j = pl.program_id(1)
    halo = 8
    pad_ref[0:halo, :] = jnp.zeros((halo, LANE), f32)
    pad_ref[halo + seq:2 * halo + seq, :] = jnp.zeros((halo, LANE), f32)
    pad_ref[halo:halo + seq, :] = x_ref[0]
    w = w_ref[...]
    bias = b_ref[...]
    scale = jnp.where(j < n_qscale, qscale, 1.0).astype(f32)
    norm_on = jnp.where(j < n_norm, 1.0, 0.0).astype(f32)
    rows = ROW_TILE

    def body(c, carry):
        r0 = pl.multiple_of(c * rows, rows)
        win = pad_ref[pl.ds(r0, rows + 2 * halo), :]
        t = r0 + lax.broadcasted_iota(jnp.int32, (rows, LANE), 0)
        acc = bias + w[2:3] * win[halo:halo + rows]
        for k in (0, 1, 3, 4):
            off = k - CONV_W // 2
            tap = win[halo + off:halo + off + rows]
            same_segment = ((t + off) >= seg0) == (t >= seg0)
            acc = acc + w[k:k + 1] * jnp.where(same_segment, tap, 0.0)
        y = _silu(acc)
        yn = y * lax.rsqrt(jnp.sum(y * y, -1, keepdims=True) + EPS)
        y = (norm_on * yn + (1.0 - norm_on) * y) * scale
        o_ref[0, pl.ds(r0, rows), :] = y
        return carry

    lax.fori_loop(0, seq // rows, body, 0)


def _conv_act(x, w, b, col_blocks, *, seg0, n_norm, n_qscale, qscale):
    bsz, seq, _ = x.shape
    n_out = len(col_blocks)
    first_gap = next((i for i, cb in enumerate(col_blocks) if cb != i), n_out)
    gap = col_blocks[first_gap] - first_gap if first_gap < n_out else 0
    in_map = lambda bi, j: (bi, 0, jnp.where(j < first_gap, j, j + gap))
    kern = functools.partial(_conv_kernel, seq=seq, seg0=seg0, n_norm=n_norm, n_qscale=n_qscale, qscale=qscale)
    return pl.pallas_call(
        kern,
        out_shape=jax.ShapeDtypeStruct((bsz, seq, n_out * LANE), f32),
        grid=(bsz, n_out),
        in_specs=[pl.BlockSpec((1, seq, LANE), in_map),
                  pl.BlockSpec((CONV_W, LANE), lambda bi, j: (0, j)),
                  pl.BlockSpec((1, LANE), lambda bi, j: (0, j))],
        out_specs=pl.BlockSpec((1, seq, LANE), lambda bi, j: (bi, 0, j)),
        scratch_shapes=[pltpu.VMEM((seq + 16, LANE), f32)],
        compiler_params=_params("parallel", "parallel"),
    )(x, w, b)


def _chunk_index(d, n, n_ctx, n_all):
    rev = jnp.where(n < n_ctx, n_ctx - 1 - n, n_all + n_ctx - 1 - n)
    return jnp.where(d == 0, n, rev)


def _gate_layouts(raw, heads):
    bsz, seq, _ = raw.shape
    r = raw.reshape(bsz, seq // CHUNK, CHUNK, N_DIR, heads)
    return r.transpose(3, 0, 1, 2, 4), r.transpose(3, 0, 1, 4, 2)


def _param_layouts(*ps):
    return jnp.stack(ps, axis=1).astype(f32), jnp.stack(ps, axis=2).astype(f32)


def _gdn_kernel(q_ref, k_ref, v_ref, bcol_ref, brow_ref, dcol_ref, drow_ref, pcol_ref, prow_ref,
                o_ref, st_ref, *, bsz):
    d = pl.program_id(0)
    n = pl.program_id(1)

    @pl.when(n == 0)
    def _():
        st_ref[...] = jnp.zeros(st_ref.shape, f32)

    m_incl, m_strict, m_incl_t, m_eye = _chunk_masks(d)
    eye = m_eye.astype(f32)
    row = lax.broadcasted_iota(jnp.int32, (CHUNK, CHUNK), 0)
    col = lax.broadcasted_iota(jnp.int32, (CHUNK, CHUNK), 1)
    blocks = [(row >> sh) == (col >> sh) for sh in (3, 4, 5, 6)]
    off_masks = [blocks[lvl + 1] & ~blocks[lvl] for lvl in range(3)]
    pc = pcol_ref[0]
    pr = prow_ref[0]
    chains = [(b, h) for b in range(bsz) for h in range(A_HEADS)]
    idx = range(len(chains))
    beta_col = [_sigmoid(bcol_ref[0, b, 0]) for b in range(bsz)]
    g_col = [-jnp.exp(pc[0:1, :]) * _softplus(dcol_ref[0, b, 0] + pc[1:2, :]) for b in range(bsz)]
    g_row = [-jnp.exp(pr[:, 0:1]) * _softplus(drow_ref[0, b, 0] + pr[:, 1:2]) for b in range(bsz)]
    sl = [slice(h * A_DK, (h + 1) * A_DK) for _, h in chains]
    q = [q_ref[b, :, sl[c]] for c, (b, _) in enumerate(chains)]
    k = [k_ref[b, :, sl[c]] for c, (b, _) in enumerate(chains)]
    v = [v_ref[b, :, sl[c]] for c, (b, _) in enumerate(chains)]
    cums = [_cum(m_incl, m_incl_t, g_col[b][:, h:h + 1], g_row[b][h:h + 1, :]) for b, h in chains]
    decay = [jnp.exp(jnp.where(m_incl, gc - gr, -jnp.inf)) for gc, gr, _ in cums]
    b_col = [beta_col[b][:, h:h + 1] for b, h in chains]
    kb = [k[c] * b_col[c] for c in idx]
    l_mat = [_dot3(kb[c], k[c], NT) * jnp.where(m_strict, decay[c], 0.0) for c in idx]
    nil = [-jnp.where(blocks[0], l_mat[c], 0.0) for c in idx]
    t_inv = [eye + nil[c] for c in idx]
    for _ in range(2):
        nil = [_dot3(nil[c], nil[c]) for c in idx]
        t_inv = [t_inv[c] + _dot3(t_inv[c], nil[c]) for c in idx]
    for lvl in range(3):
        left = [_dot3(t_inv[c], jnp.where(off_masks[lvl], l_mat[c], 0.0)) for c in idx]
        t_inv = [t_inv[c] - _dot3(left[c], t_inv[c]) for c in idx]
    e_col = [jnp.exp(gc) for gc, _, _ in cums]
    uw = [_dot3(t_inv[c], jnp.concatenate([v[c] * b_col[c], kb[c] * e_col[c]], axis=1)) for c in idx]
    qk = [_dot(q[c], k[c], NT) * decay[c] for c in idx]
    k_dec_t = [(k[c] * jnp.exp(cums[c][2] - cums[c][0])).T for c in idx]
    s = [st_ref[c] for c in idx]
    wq = [_dot(jnp.concatenate([uw[c][:, A_DV:], q[c] * e_col[c]], axis=0), s[c]) for c in idx]
    v_new = [uw[c][:, :A_DV] - wq[c][:CHUNK] for c in idx]
    out = [wq[c][CHUNK:] + _dot(qk[c], v_new[c]) for c in idx]
    s_new = [jnp.exp(cums[c][2]) * s[c] + _dot(k_dec_t[c], v_new[c]) for c in idx]
    for c, (b, _) in enumerate(chains):
        o_ref[0, b, :, sl[c]] = out[c]
        st_ref[c] = s_new[c]


def _gdn(act, beta_raw, decay_raw, a_log, dt_bias, n_ctx):
    bsz, seq, _ = act.shape
    n_all = seq // CHUNK
    hw = A_HEADS * A_DK
    bcol, brow = _gate_layouts(beta_raw, A_HEADS)
    dcol, drow = _gate_layouts(decay_raw, A_HEADS)
    pcol, prow = _param_layouts(a_log, dt_bias)
    ci = functools.partial(_chunk_index, n_ctx=n_ctx, n_all=n_all)
    tok = lambda blk: pl.BlockSpec((bsz, CHUNK, hw), lambda d, n: (0, ci(d, n), blk))
    gcol = pl.BlockSpec((1, bsz, 1, CHUNK, A_HEADS), lambda d, n: (d, 0, ci(d, n), 0, 0))
    grow = pl.BlockSpec((1, bsz, 1, A_HEADS, CHUNK), lambda d, n: (d, 0, ci(d, n), 0, 0))
    return pl.pallas_call(
        functools.partial(_gdn_kernel, bsz=bsz),
        out_shape=jax.ShapeDtypeStruct((N_DIR, bsz, seq, hw), f32),
        grid=(N_DIR, n_all),
        in_specs=[tok(0), tok(1), tok(2), gcol, grow, gcol, grow,
                  pl.BlockSpec((1, 2, A_HEADS), lambda d, n: (d, 0, 0)),
                  pl.BlockSpec((1, A_HEADS, 2), lambda d, n: (d, 0, 0))],
        out_specs=pl.BlockSpec((1, bsz, CHUNK, hw), lambda d, n: (d, 0, ci(d, n), 0)),
        scratch_shapes=[pltpu.VMEM((bsz * A_HEADS, A_DK, A_DV), f32)],
        compiler_params=_params("arbitrary", "arbitrary"),
    )(act, act, act, bcol, brow, dcol, drow, pcol, prow)


def _ssd_kernel(x_ref, bm_ref, cm_ref, tcol_ref, trow_ref, pcol_ref, prow_ref, o_ref, st_ref, *, bsz):
    d = pl.program_id(0)
    n = pl.program_id(1)

    @pl.when(n == 0)
    def _():
        st_ref[...] = jnp.zeros(st_ref.shape, f32)

    m_incl, _, m_incl_t, _ = _chunk_masks(d)
    pc = pcol_ref[0]
    pr = prow_ref[0]
    rep = B_HEADS // B_GROUPS
    groups = [(b, g) for b in range(bsz) for g in range(B_GROUPS)]
    heads = [(b, h) for b in range(bsz) for h in range(B_HEADS)]
    gsl = [slice(g * B_STATE, (g + 1) * B_STATE) for _, g in groups]
    hsl = [slice(h * B_HEADDIM, (h + 1) * B_HEADDIM) for _, h in heads]
    dt_col = [_softplus(tcol_ref[0, b, 0] + pc[1:2, :]) for b in range(bsz)]
    dt_row = [_softplus(trow_ref[0, b, 0] + pr[:, 1:2]) for b in range(bsz)]
    da_col = [dt_col[b] * (-jnp.exp(pc[0:1, :])) for b in range(bsz)]
    da_row = [dt_row[b] * (-jnp.exp(pr[:, 0:1])) for b in range(bsz)]
    bm = [bm_ref[b, :, gsl[i]] for i, (b, _) in enumerate(groups)]
    cm = [cm_ref[b, :, gsl[i]] for i, (b, _) in enumerate(groups)]
    hs = [st_ref[i] for i in range(len(groups))]
    cb = [_dot(cm[i], bm[i], NT) for i in range(len(groups))]
    y_off = [_dot(cm[i], hs[i]) for i in range(len(groups))]
    bm_t = [bm[i].T for i in range(len(groups))]
    cums = [_cum(m_incl, m_incl_t, da_col[b][:, h:h + 1], da_row[b][h:h + 1, :]) for b, h in heads]
    seg = [jnp.exp(jnp.where(m_incl, ac - ar, -jnp.inf)) for ac, ar, _ in cums]
    xdt = [x_ref[b, :, hsl[j]] * dt_col[b][:, h:h + 1] for j, (b, h) in enumerate(heads)]
    y_diag = [_dot(cb[j // rep] * seg[j], xdt[j]) for j in range(len(heads))]
    for j, (b, h) in enumerate(heads):
        r = h % rep
        o_ref[0, b, :, hsl[j]] = y_diag[j] + jnp.exp(cums[j][0]) * y_off[j // rep][:, r * B_HEADDIM:(r + 1) * B_HEADDIM]
    xdt_end = [xdt[j] * jnp.exp(cums[j][2] - cums[j][0]) for j in range(len(heads))]
    keep = [jnp.broadcast_to(jnp.exp(cums[j][2]), (1, B_HEADDIM)) for j in range(len(heads))]
    upd = [_dot(bm_t[i], jnp.concatenate(xdt_end[i * rep:(i + 1) * rep], axis=1)) for i in range(len(groups))]
    for i in range(len(groups)):
        st_ref[i] = jnp.concatenate(keep[i * rep:(i + 1) * rep], axis=1) * hs[i] + upd[i]


def _ssd(act, dt_raw, a_log, dt_bias, n_ctx):
    bsz, seq, _ = act.shape
    n_all = seq // CHUNK
    dinner = B_HEADS * B_HEADDIM
    gw = B_GROUPS * B_STATE
    tcol, trow = _gate_layouts(dt_raw, B_HEADS)
    pcol, prow = _param_layouts(a_log, dt_bias)
    ci = functools.partial(_chunk_index, n_ctx=n_ctx, n_all=n_all)
    x_off = (A_HEADS * (2 * A_DK + A_DV)) // dinner
    bm_off = (A_HEADS * (2 * A_DK + A_DV) + dinner) // gw
    return pl.pallas_call(
        functools.partial(_ssd_kernel, bsz=bsz),
        out_shape=jax.ShapeDtypeStruct((N_DIR, bsz, seq, dinner), f32),
        grid=(N_DIR, n_all),
        in_specs=[pl.BlockSpec((bsz, CHUNK, dinner), lambda d, n: (0, ci(d, n), x_off)),
                  pl.BlockSpec((bsz, CHUNK, gw), lambda d, n: (0, ci(d, n), bm_off)),
                  pl.BlockSpec((bsz, CHUNK, gw), lambda d, n: (0, ci(d, n), bm_off + 1)),
                  pl.BlockSpec((1, bsz, 1, CHUNK, B_HEADS), lambda d, n: (d, 0, ci(d, n), 0, 0)),
                  pl.BlockSpec((1, bsz, 1, B_HEADS, CHUNK), lambda d, n: (d, 0, ci(d, n), 0, 0)),
                  pl.BlockSpec((1, 2, B_HEADS), lambda d, n: (d, 0, 0)),
                  pl.BlockSpec((1, B_HEADS, 2), lambda d, n: (d, 0, 0))],
        out_specs=pl.BlockSpec((1, bsz, CHUNK, dinner), lambda d, n: (d, 0, ci(d, n), 0)),
        scratch_shapes=[pltpu.VMEM((bsz * B_GROUPS, B_STATE, (B_HEADS // B_GROUPS) * B_HEADDIM), f32)],
        compiler_params=_params("arbitrary", "arbitrary"),
    )(act, act, act, tcol, trow, pcol, prow)


def _mlstm_kernel(q_ref, k_ref, v_ref, icol_ref, irow_ref, fcol_ref, frow_ref, pcol_ref, prow_ref,
                  o_ref, c_ref, n_ref, m_ref, *, bsz):
    d = pl.program_id(0)
    n = pl.program_id(1)

    @pl.when(n == 0)
    def _():
        c_ref[...] = jnp.zeros(c_ref.shape, f32)
        n_ref[...] = jnp.zeros(n_ref.shape, f32)
        m_ref[...] = jnp.zeros(m_ref.shape, f32)

    m_incl, _, m_incl_t, _ = _chunk_masks(d)
    pc = pcol_ref[0]
    pr = prow_ref[0]
    chains = [(b, h) for b in range(bsz) for h in range(C_HEADS)]
    idx = range(len(chains))
    li_col = [icol_ref[0, b, 0] + pc[0:1, :] for b in range(bsz)]
    li_row = [irow_ref[0, b, 0] + pr[:, 0:1] for b in range(bsz)]
    lf_col = [-_softplus(-(fcol_ref[0, b, 0] + pc[1:2, :])) for b in range(bsz)]
    lf_row = [-_softplus(-(frow_ref[0, b, 0] + pr[:, 1:2])) for b in range(bsz)]
    q = [q_ref[b, :, h * C_DK:(h + 1) * C_DK] for b, h in chains]
    k = [k_ref[b, :, h * C_DK:(h + 1) * C_DK] for b, h in chains]
    v = [v_ref[b, :, h * C_DV:(h + 1) * C_DV] for b, h in chains]
    cums = [_cum(m_incl, m_incl_t, lf_col[b][:, h:h + 1], lf_row[b][h:h + 1, :]) for b, h in chains]
    logw_intra = [jnp.where(m_incl, cums[c][0] - cums[c][1] + li_row[b][h:h + 1, :], -jnp.inf)
                  for c, (b, h) in enumerate(chains)]
    m_prev = [m_ref[c] for c in idx]
    logw_inter = [cums[c][0] + m_prev[c] for c in idx]
    m_t = [jnp.maximum(logw_inter[c], jnp.max(logw_intra[c], axis=1, keepdims=True)) for c in idx]
    qk = [_dot(q[c], k[c], NT) for c in idx]
    c_mem = [c_ref[c] for c in idx]
    n_mem = [n_ref[c] for c in idx]
    qc = [_dot(q[c], c_mem[c]) for c in idx]
    s = [qk[c] * jnp.exp(logw_intra[c] - m_t[c]) for c in idx]
    w_inter = [jnp.exp(logw_inter[c] - m_t[c]) for c in idx]
    num = [_dot(s[c], v[c]) + w_inter[c] * qc[c] for c in idx]
    den = [jnp.sum(s[c], axis=1, keepdims=True) + w_inter[c] * jnp.sum(q[c] * n_mem[c], axis=1, keepdims=True)
           for c in idx]
    for c, (b, h) in enumerate(chains):
        o_ref[0, b, :, h * C_DV:(h + 1) * C_DV] = num[c] / jnp.maximum(jnp.abs(den[c]), jnp.exp(-m_t[c]))
    logw_end = [cums[c][2] - cums[c][0] + li_col[b][:, h:h + 1] for c, (b, h) in enumerate(chains)]
    m_new = [jnp.maximum(cums[c][2] + m_prev[c], jnp.max(logw_end[c], axis=0, keepdims=True)) for c in idx]
    kw = [k[c] * jnp.exp(logw_end[c] - m_new[c]) for c in idx]
    keep = [jnp.exp(cums[c][2] + m_prev[c] - m_new[c]) for c in idx]
    kv = [_dot(kw[c].T, v[c]) for c in idx]
    for c in idx:
        c_ref[c] = keep[c] * c_mem[c] + kv[c]
        n_ref[c] = keep[c] * n_mem[c] + jnp.sum(kw[c], axis=0, keepdims=True)
        m_ref[c] = m_new[c]


def _mlstm(act, proj, i_raw, f_raw, i_bias, f_bias, n_ctx):
    bsz, seq, _ = act.shape
    n_all = seq // CHUNK
    qw = C_HEADS * C_DK
    vw = C_HEADS * C_DV
    icol, irow = _gate_layouts(i_raw, C_HEADS)
    fcol, frow = _gate_layouts(f_raw, C_HEADS)
    pcol, prow = _param_layouts(i_bias, f_bias)
    ci = functools.partial(_chunk_index, n_ctx=n_ctx, n_all=n_all)
    gcol = pl.BlockSpec((1, bsz, 1, CHUNK, C_HEADS), lambda d, n: (d, 0, ci(d, n), 0, 0))
    grow = pl.BlockSpec((1, bsz, 1, C_HEADS, CHUNK), lambda d, n: (d, 0, ci(d, n), 0, 0))
    return pl.pallas_call(
        functools.partial(_mlstm_kernel, bsz=bsz),
        out_shape=jax.ShapeDtypeStruct((N_DIR, bsz, seq, vw), f32),
        grid=(N_DIR, n_all),
        in_specs=[pl.BlockSpec((bsz, CHUNK, qw), lambda d, n: (0, ci(d, n), 0)),
                  pl.BlockSpec((bsz, CHUNK, qw), lambda d, n: (0, ci(d, n), 1)),
                  pl.BlockSpec((bsz, CHUNK, vw), lambda d, n: (0, ci(d, n), 2 * qw // vw)),
                  gcol, grow, gcol, grow,
                  pl.BlockSpec((1, 2, C_HEADS), lambda d, n: (d, 0, 0)),
                  pl.BlockSpec((1, C_HEADS, 2), lambda d, n: (d, 0, 0))],
        out_specs=pl.BlockSpec((1, bsz, CHUNK, vw), lambda d, n: (d, 0, ci(d, n), 0)),
        scratch_shapes=[pltpu.VMEM((bsz * C_HEADS, C_DK, C_DV), f32),
                        pltpu.VMEM((bsz * C_HEADS, 1, C_DK), f32),
                        pltpu.VMEM((bsz * C_HEADS, 1, 1), f32)],
        compiler_params=_params("arbitrary", "arbitrary"),
    )(act, act, proj, icol, irow, fcol, frow, pcol, prow)


def _merge_even_kernel(x_ref, a_ref, b_ref, za_ref, zb_ref, xs_ref, na_ref, nb_ref, dsk_ref, w_ref,
                       g_ref, lg_ref, lb_ref, o_ref):
    a = a_ref[0] + a_ref[1]
    za = za_ref[...]
    na = na_ref[...]
    parts = []
    for h in range(A_HEADS):
        sl = slice(h * A_DV, (h + 1) * A_DV)
        parts.append(_rms(a[:, sl], na) * _silu(za[:, sl]))
    yb = (b_ref[0] + b_ref[1] + dsk_ref[...] * xs_ref[...]) * _silu(zb_ref[...])
    gw = (B_HEADS * B_HEADDIM) // B_GROUPS
    nb = nb_ref[...]
    for g in range(B_GROUPS):
        sl = slice(g * gw, (g + 1) * gw)
        parts.append(_rms(yb[:, sl], nb[:, sl]))
    y = jnp.concatenate(parts, axis=1).astype(bf16)
    z = ALPHA * x_ref[...] + g_ref[0] * _dot(y, w_ref[...])
    o_ref[...] = _layer_norm(z, lg_ref[...], lb_ref[...])


def _merge_even(x, o_a, o_b, proj, act, norm_a, norm_b, d_skip, w_out, gate_t, ln_g, ln_b):
    rows, d = x.shape
    aw = A_HEADS * A_DV
    bw = B_HEADS * B_HEADDIM
    row = lambda w_, blk: pl.BlockSpec((ROW_TILE, w_), lambda i: (i, blk))
    vec = lambda w_: pl.BlockSpec((1, w_), lambda i: (0, 0))
    qkv = A_HEADS * (2 * A_DK + A_DV)
    return pl.pallas_call(
        _merge_even_kernel,
        out_shape=jax.ShapeDtypeStruct((rows, d), f32),
        grid=(rows // ROW_TILE,),
        in_specs=[row(d, 0),
                  pl.BlockSpec((N_DIR, ROW_TILE, aw), lambda i: (0, i, 0)),
                  pl.BlockSpec((N_DIR, ROW_TILE, bw), lambda i: (0, i, 0)),
                  row(aw, qkv // aw), row(bw, (qkv + aw) // bw), row(bw, qkv // bw),
                  vec(A_DV), vec(bw), vec(bw),
                  pl.BlockSpec((aw + bw, d), lambda i: (0, 0)),
                  pl.BlockSpec((1, 1, d), lambda i: (i, 0, 0)),
                  vec(d), vec(d)],
        out_specs=row(d, 0),
        compiler_params=_params("parallel"),
    )(x, o_a, o_b, proj, proj, act, norm_a.reshape(1, -1), norm_b.reshape(1, -1),
      jnp.repeat(d_skip, B_HEADDIM).reshape(1, -1), w_out, gate_t, ln_g.reshape(1, -1), ln_b.reshape(1, -1))


def _merge_odd_kernel(h_ref, o_ref_in, nc_ref, w_ref, out_ref):
    hsum = h_ref[0] + h_ref[1]
    o = o_ref_in[...]
    nc = nc_ref[...]
    parts = []
    for h in range(C_HEADS):
        sl = slice(h * C_DV, (h + 1) * C_DV)
        parts.append(_rms(hsum[:, sl], nc) * _sigmoid(o[:, sl]))
    y = jnp.concatenate(parts, axis=1).astype(bf16)
    out_ref[...] = _dot(y, w_ref[...])


def _merge_odd(h, proj, norm_c, w_out, tiles_per_batch, ctx_tiles, bsz):
    vw = C_HEADS * C_DV
    lat_tiles = tiles_per_batch - ctx_tiles
    src = lambda i: i + ctx_tiles * (i // lat_tiles + 1)
    d = w_out.shape[1]
    return pl.pallas_call(
        _merge_odd_kernel,
        out_shape=jax.ShapeDtypeStruct((bsz * lat_tiles * ROW_TILE, d), f32),
        grid=(bsz * lat_tiles,),
        in_specs=[pl.BlockSpec((N_DIR, ROW_TILE, vw), lambda i: (0, src(i), 0)),
                  pl.BlockSpec((ROW_TILE, vw), lambda i: (src(i), 2 * C_HEADS * C_DK // vw + 1)),
                  pl.BlockSpec((1, C_DV), lambda i: (0, 0)),
                  pl.BlockSpec((vw, d), lambda i: (0, 0))],
        out_specs=pl.BlockSpec((ROW_TILE, d), lambda i: (i, 0)),
        compiler_params=_params("parallel"),
    )(h, proj, norm_c.reshape(1, -1), w_out)


def _resid_ln_kernel(x_ref, y_ref, g_ref, lg_ref, lb_ref, o_ref, *, transposed):
    y = y_ref[...]
    if transposed:
        y = y.T
    o_ref[...] = _layer_norm(ALPHA * x_ref[...] + g_ref[0] * y, lg_ref[...], lb_ref[...])


def _resid_ln(x, y, gate_t, ln_g, ln_b, *, transposed):
    rows, d = x.shape
    y_spec = (pl.BlockSpec((d, ROW_TILE), lambda i: (0, i)) if transposed
              else pl.BlockSpec((ROW_TILE, d), lambda i: (i, 0)))
    return pl.pallas_call(
        functools.partial(_resid_ln_kernel, transposed=transposed),
        out_shape=jax.ShapeDtypeStruct((rows, d), f32),
        grid=(rows // ROW_TILE,),
        in_specs=[pl.BlockSpec((ROW_TILE, d), lambda i: (i, 0)), y_spec,
                  pl.BlockSpec((1, 1, d), lambda i: (i, 0, 0)),
                  pl.BlockSpec((1, d), lambda i: (0, 0)),
                  pl.BlockSpec((1, d), lambda i: (0, 0))],
        out_specs=pl.BlockSpec((ROW_TILE, d), lambda i: (i, 0)),
        compiler_params=_params("parallel"),
    )(x, y, gate_t, ln_g.reshape(1, -1), ln_b.reshape(1, -1))


def _oddeven_merge(lo, hi, r):
    step = r * 2
    if step < hi - lo:
        yield from _oddeven_merge(lo, hi, step)
        yield from _oddeven_merge(lo + r, hi, step)
        yield from [(i, i + r) for i in range(lo + r, hi - r, step)]
    else:
        yield (lo, lo + r)


def _oddeven_sort(lo, hi):
    if hi - lo >= 1:
        mid = lo + (hi - lo) // 2
        yield from _oddeven_sort(lo, mid)
        yield from _oddeven_sort(mid + 1, hi)
        yield from _oddeven_merge(lo, hi, 1)


def _exchange(a, i, j):
    a[i], a[j] = jnp.maximum(a[i], a[j]), jnp.minimum(a[i], a[j])


def _top_sorted(slabs):
    a = list(slabs)
    for i, j in _oddeven_sort(0, len(a) - 1):
        _exchange(a, i, j)
    for shift in (4, 2, 1):
        other = [pltpu.roll(x, shift, axis=0) for x in a]
        if len(a) < PEER_TOPK:
            a = a + other[::-1]
        else:
            a = [jnp.maximum(a[k], other[PEER_TOPK - 1 - k]) for k in range(PEER_TOPK)]
        dist = PEER_TOPK // 2
        while dist >= 1:
            for i in range(PEER_TOPK):
                if i & dist == 0:
                    _exchange(a, i, i + dist)
            dist //= 2
    return a


def _next_largest(slabs, kth):
    count = sum(jnp.where(s >= kth, 1.0, 0.0) for s in slabs)
    below = functools.reduce(jnp.maximum, [jnp.where(s < kth, s, -jnp.inf) for s in slabs])
    count = jnp.sum(count, axis=0, keepdims=True)
    below = jnp.max(below, axis=0, keepdims=True)
    return jnp.where(count > PEER_TOPK, kth, below)


def _top_values(s):
    slabs = [s[i:i + 8] for i in range(0, s.shape[0], 8)]
    if len(slabs) < 8:
        slabs = slabs + [jnp.full_like(slabs[0], -jnp.inf)] * (8 - len(slabs))
    top = [t[0:1] for t in _top_sorted(slabs)]
    return top + [_next_largest(slabs, top[-1])]


def _route_kernel(x_ref, sh_ref, sc_ref, wh_ref, wl_ref, kh_ref, kl_ref,
                  xm_ref, thr_ref, s2_ref, e1_ref, e2_ref, xh_ref, xl_ref, sc_scr, cand_ref):
    for t in range(ROUTE_TILE // ROW_TILE):
        rows = slice(t * ROW_TILE, (t + 1) * ROW_TILE)
        xm = x_ref[rows, :] * (1.0 + sc_ref[t]) + sh_ref[t]
        xm_ref[rows, :] = xm.astype(bf16)
        xh_ref[rows, :], xl_ref[rows, :] = _split(xm)
    half = PEER_DKEY // 2
    n_top = PEER_TOPK + 1
    pairs = [(p, r) for p in range(n_top) for r in range(n_top) if (p + 1) * (r + 1) <= n_top]
    cand_ref[...] = jnp.full(cand_ref.shape, -jnp.inf, f32)

    def scores(h, sc_scr):
        wsl = pl.ds(pl.multiple_of(h * PEER_DKEY, PEER_DKEY), PEER_DKEY)
        xh = xh_ref[...]
        q = _dot(xh, wh_ref[:, wsl]) + (_dot(xh, wl_ref[:, wsl]) + _dot(xl_ref[...], wh_ref[:, wsl]))
        for c in range(2):
            qh, ql = _split(q[:, c * half:(c + 1) * half])
            kh = kh_ref[h, c]
            sc_scr[c] = _dot(kh, qh, NT) + (_dot(kl_ref[h, c], qh, NT) + _dot(kh, ql, NT))

    def select(h, sc_scr):
        for lb in range(ROUTE_TILE // LANE):
            lanes = slice(lb * LANE, (lb + 1) * LANE)
            s1 = sc_scr[0, :, lanes]
            s2 = sc_scr[1, :, lanes]
            t1 = _top_values(s1)
            t2 = _top_values(s2)
            for i, (p, r) in enumerate(pairs):
                cand_ref[i:i + 1, lanes] = t1[p] + t2[r]
            best = _top_values(cand_ref[:, lanes])
            z = sum(jnp.exp(bv - best[0]) for bv in best[:PEER_TOPK])
            tau = 0.5 * (best[PEER_TOPK - 1] + best[PEER_TOPK])
            thr_ref[h, :, lanes] = tau - s1
            s2_ref[h, :, lanes] = s2
            e1_ref[h, :, lanes] = jnp.exp(s1 - t1[0]) / z
            e2_ref[h, :, lanes] = jnp.exp(s2 - t2[0])

    def head(h, carry):
        scores(h, sc_scr)
        select(h, sc_scr)
        return carry

    lax.fori_loop(0, PEER_HEADS, head, 0)


def _route(x, shift_t, scale_t, wq_hi, wq_lo, keys_hi, keys_lo):
    rows, d = x.shape
    nq = wq_hi.shape[1]
    half = PEER_DKEY // 2
    score = jax.ShapeDtypeStruct((PEER_HEADS, PEER_NKEYS, rows), f32)
    score_spec = pl.BlockSpec((PEER_HEADS, PEER_NKEYS, ROUTE_TILE), lambda i: (0, 0, i))
    key_spec = pl.BlockSpec((PEER_HEADS, 2, PEER_NKEYS, half), lambda i: (0, 0, 0, 0))
    mods = ROUTE_TILE // ROW_TILE
    return pl.pallas_call(
        _route_kernel,
        out_shape=(jax.ShapeDtypeStruct((rows, d), bf16), score, score, score, score),
        grid=(rows // ROUTE_TILE,),
        in_specs=[pl.BlockSpec((ROUTE_TILE, d), lambda i: (i, 0)),
                  pl.BlockSpec((mods, 1, d), lambda i: (i, 0, 0)),
                  pl.BlockSpec((mods, 1, d), lambda i: (i, 0, 0)),
                  pl.BlockSpec((d, nq), lambda i: (0, 0)),
                  pl.BlockSpec((d, nq), lambda i: (0, 0)),
                  key_spec, key_spec],
        out_specs=(pl.BlockSpec((ROUTE_TILE, d), lambda i: (i, 0)),
                   score_spec, score_spec, score_spec, score_spec),
        scratch_shapes=[pltpu.VMEM((ROUTE_TILE, d), bf16), pltpu.VMEM((ROUTE_TILE, d), bf16),
                        pltpu.VMEM((2, PEER_NKEYS, ROUTE_TILE), f32),
                        pltpu.VMEM((N_CAND_ROWS, ROUTE_TILE), f32)],
        compiler_params=_params("parallel"),
    )(x, shift_t, scale_t, wq_hi, wq_lo, keys_hi, keys_lo)


def _expert_kernel(xm_ref, u_ref, vt_ref, thr_ref, e1_ref, s2_ref, e2_ref, o_ref, a_scr, w_scr):
    j = pl.program_id(1)

    @pl.when(j == 0)
    def _():
        o_ref[...] = jnp.zeros(o_ref.shape, f32)

    a_scr[...] = _dot(u_ref[...], xm_ref[...], NT)
    sqrt_half = math.sqrt(0.5)
    n_i1 = EXPERT_BLK // PEER_NKEYS
    row_chunks = PEER_NKEYS // GATE_ROWS

    def tile(t, carry):
        lanes = pl.ds(pl.multiple_of((t // row_chunks) * LANE, LANE), LANE)
        r0 = pl.multiple_of((t % row_chunks) * GATE_ROWS, GATE_ROWS)
        i2 = pl.ds(r0, GATE_ROWS)
        gate = [jnp.zeros((GATE_ROWS, LANE), f32) for _ in range(n_i1)]
        for h in range(PEER_HEADS):
            s2 = s2_ref[h, i2, lanes]
            e2 = e2_ref[h, i2, lanes]
            for il in range(n_i1):
                w = e1_ref[h, il:il + 1, lanes] * e2
                gate[il] = gate[il] + jnp.where(s2 >= thr_ref[h, il:il + 1, lanes], w, 0.0)
        for il in range(n_i1):
            rows = pl.ds(il * PEER_NKEYS + r0, GATE_ROWS)
            a = a_scr[rows, lanes]
            act = 0.5 * a * (1.0 + lax.erf(a * sqrt_half))
            w_scr[rows, lanes] = (gate[il] * act).astype(bf16)
        return carry

    lax.fori_loop(0, (EXPERT_TOK // LANE) * row_chunks, tile, 0)
    o_ref[...] += _dot(vt_ref[...], w_scr[...])


def _experts(xm, u_bf, vt_bf, thr, e1, s2, e2):
    rows, d = xm.shape
    n_exp = u_bf.shape[0]
    i1_per_blk = EXPERT_BLK // PEER_NKEYS
    sel = pl.BlockSpec((PEER_HEADS, i1_per_blk, EXPERT_TOK), lambda i, j: (0, j, i))
    full = pl.BlockSpec((PEER_HEADS, PEER_NKEYS, EXPERT_TOK), lambda i, j: (0, 0, i))
    return pl.pallas_call(
        _expert_kernel,
        out_shape=jax.ShapeDtypeStruct((d, rows), f32),
        grid=(rows // EXPERT_TOK, n_exp // EXPERT_BLK),
        in_specs=[pl.BlockSpec((EXPERT_TOK, d), lambda i, j: (i, 0)),
                  pl.BlockSpec((EXPERT_BLK, d), lambda i, j: (j, 0)),
                  pl.BlockSpec((d, EXPERT_BLK), lambda i, j: (0, j)),
                  sel, sel, full, full],
        out_specs=pl.BlockSpec((d, EXPERT_TOK), lambda i, j: (0, i)),
        scratch_shapes=[pltpu.VMEM((EXPERT_BLK, EXPERT_TOK), f32),
                        pltpu.VMEM((EXPERT_BLK, EXPERT_TOK), bf16)],
        compiler_params=_params("parallel", "arbitrary"),
    )(xm, u_bf, vt_bf, thr, e1, s2, e2)


def _peer_residual(x, shift_t, scale_t, gate_t, peer_q, peer_keys, peer_u, peer_v, ln_g, ln_b):
    wq_hi, wq_lo = _split(peer_q)
    keys_hi, keys_lo = _split(peer_keys)
    xm, thr, s2, e1, e2 = _route(x, shift_t, scale_t, wq_hi, wq_lo, keys_hi, keys_lo)
    y_t = _experts(xm, peer_u.astype(bf16), peer_v.T.astype(bf16), thr, e1, s2, e2)
    return _resid_ln(x, y_t, gate_t, ln_g, ln_b, transposed=True)


def _grid_sincos(n_tokens, d):
    t = jnp.arange(n_tokens)
    row = (t // GRID_W).astype(f32)[:, None]
    col = (t % GRID_W).astype(f32)[:, None]
    n_freq = d // 4
    freq = jnp.exp(-math.log(POS_BASE) * jnp.arange(n_freq, dtype=f32) / n_freq)[None, :]
    return jnp.concatenate([jnp.sin(row * freq), jnp.cos(row * freq), jnp.sin(col * freq), jnp.cos(col * freq)], -1)


def _tile_rows(mod, src):
    d = mod.shape[1] // 6
    rows = jnp.concatenate([jnp.broadcast_to(mod[r:r + 1], (n, mod.shape[1])) for r, n in src], axis=0)
    return [rows[:, None, i * d:(i + 1) * d] for i in range(6)]


def _pad_cols(w, n):
    return jnp.pad(w, ((0, 0), (0, n - w.shape[1])))


def kernel(x, c, ctx, c_ctx, l0_ada_w, l0_ada_b, l0_ln1_g, l0_ln1_b, l0_w_in, l0_conv_a, l0_a_log_a, l0_dt_bias_a, l0_norm_a, l0_conv_b, l0_conv_b_bias, l0_a_log_b, l0_dt_bias_b, l0_d_skip_b, l0_norm_b, l0_w_out, l0_ln2_g, l0_ln2_b, l0_peer_q, l0_peer_keys, l0_peer_u, l0_peer_v, l1_ada_w, l1_ada_b, l1_ln1_g, l1_ln1_b, l1_w_in, l1_conv_c, l1_i_bias, l1_f_bias, l1_norm_c, l1_w_out, l1_ln2_g, l1_ln2_b, l1_peer_q, l1_peer_keys, l1_peer_u, l1_peer_v):
    bsz, n_lat, d = x.shape
    n_ctx_tok = ctx.shape[1]
    rows_grid = n_lat // GRID_W
    seq = n_ctx_tok + n_lat
    n_ctx = n_ctx_tok // CHUNK
    tiles_b = seq // ROW_TILE
    ctx_tiles = n_ctx_tok // ROW_TILE
    lat_tiles = n_lat // ROW_TILE

    cond = jnp.concatenate([c, c_ctx[None, :], jnp.zeros((8 - bsz - 1, d), f32)], axis=0)
    src_all = [run for b in range(bsz) for run in ((bsz, ctx_tiles), (b, lat_tiles))]

    x_lat = x + _grid_sincos(n_lat, d).astype(x.dtype)
    xs_all = jnp.concatenate([ctx, x_lat], axis=1).reshape(bsz * seq, d)

    sh1, sc1, g1, sh2, sc2, g2 = _tile_rows(_ada(cond, l0_ada_w, l0_ada_b), src_all)
    qkv_w = A_HEADS * (2 * A_DK + A_DV)
    az = A_HEADS * A_DV
    ag = N_DIR * A_HEADS
    bd = B_HEADS * B_HEADDIM
    bx = bd + 2 * B_GROUPS * B_STATE
    bt = N_DIR * B_HEADS
    o0 = np.cumsum([0, qkv_w, az, ag, ag, bd, bx, bt])
    w_qkv, w_za, w_beta, w_dec, w_zb, w_xbc, w_dt = (l0_w_in[:, o0[i]:o0[i + 1]] for i in range(7))
    main_w = qkv_w + az + bd + bx
    w0 = _pad_cols(jnp.concatenate([w_qkv, w_za, w_zb, w_xbc, w_beta, w_dec, w_dt], axis=1), main_w + LANE)
    proj = _proj(xs_all, sh1, sc1, w0.astype(bf16)).reshape(bsz, seq, main_w + LANE)
    conv_w = jnp.concatenate([l0_conv_a, l0_conv_b], axis=1)
    conv_b = jnp.concatenate([jnp.zeros((qkv_w,), f32), l0_conv_b_bias]).reshape(1, -1)
    xbc_blk = (qkv_w + az + bd) // LANE
    col_blocks = list(range(qkv_w // LANE)) + list(range(xbc_blk, xbc_blk + bx // LANE))
    act = _conv_act(proj, conv_w, conv_b, col_blocks, seg0=n_ctx_tok,
                    n_norm=2 * A_HEADS, n_qscale=A_HEADS, qscale=A_DK ** -0.5)
    gates = proj[:, :, main_w:]
    o_a = _gdn(act, gates[..., :ag], gates[..., ag:2 * ag], l0_a_log_a, l0_dt_bias_a, n_ctx)
    o_b = _ssd(act, gates[..., 2 * ag:2 * ag + bt], l0_a_log_b, l0_dt_bias_b, n_ctx)
    x1 = _merge_even(xs_all, o_a.reshape(N_DIR, bsz * seq, az), o_b.reshape(N_DIR, bsz * seq, bd),
                     proj.reshape(bsz * seq, -1), act.reshape(bsz * seq, -1),
                     l0_norm_a, l0_norm_b, l0_d_skip_b, l0_w_out.astype(bf16), g1, l0_ln1_g, l0_ln1_b)
    x2 = _peer_residual(x1, sh2, sc2, g2, l0_peer_q, l0_peer_keys, l0_peer_u, l0_peer_v, l0_ln2_g, l0_ln2_b)

    sh1, sc1, g1, sh2, sc2, g2 = _tile_rows(_ada(cond, l1_ada_w, l1_ada_b), src_all)
    x2 = x2.reshape(bsz, seq, d)
    x2_ctx, x2_lat = x2[:, :n_ctx_tok], x2[:, n_ctx_tok:]
    lat_cm = x2_lat.reshape(bsz, rows_grid, GRID_W, d).transpose(0, 2, 1, 3).reshape(bsz, n_lat, d)
    xs1 = jnp.concatenate([x2_ctx, lat_cm], axis=1).reshape(bsz * seq, d)
    qk_w = 2 * C_HEADS * C_DK
    vw = C_HEADS * C_DV
    w1 = _pad_cols(l1_w_in, qk_w + 2 * vw + LANE)
    proj1 = _proj(xs1, sh1, sc1, w1.astype(bf16)).reshape(bsz, seq, -1)
    act1 = _conv_act(proj1, l1_conv_c, jnp.zeros((1, qk_w), f32), list(range(qk_w // LANE)), seg0=n_ctx_tok,
                     n_norm=0, n_qscale=C_HEADS, qscale=C_DK ** -0.5)
    gates1 = proj1[:, :, qk_w + 2 * vw:]
    cg = N_DIR * C_HEADS
    h1 = _mlstm(act1, proj1, gates1[..., :cg], gates1[..., cg:2 * cg], l1_i_bias, l1_f_bias, n_ctx)
    y_cm = _merge_odd(h1.reshape(N_DIR, bsz * seq, vw), proj1.reshape(bsz * seq, -1), l1_norm_c,
                      l1_w_out.astype(bf16), tiles_b, ctx_tiles, bsz)
    y_lat = y_cm.reshape(bsz, GRID_W, rows_grid, d).transpose(0, 2, 1, 3).reshape(bsz * n_lat, d)
    lat_sel = lambda t: t.reshape(bsz, tiles_b, 1, d)[:, ctx_tiles:].reshape(bsz * lat_tiles, 1, d)
    x3 = _resid_ln(x2_lat.reshape(bsz * n_lat, d), y_lat, lat_sel(g1), l1_ln1_g, l1_ln1_b, transposed=False)
    x4 = _peer_residual(x3, lat_sel(sh2), lat_sel(sc2), lat_sel(g2), l1_peer_q, l1_peer_keys, l1_peer_u,
                        l1_peer_v, l1_ln2_g, l1_ln2_b)
    return x4.reshape(bsz, n_lat, d)
```

```python
import functools
import math

import numpy as np
import jax
import jax.numpy as jnp
from jax import lax
from jax.experimental import pallas as pl
from jax.experimental.pallas import tpu as pltpu

f32 = jnp.float32
bf16 = jnp.bfloat16

D_MODEL = 1024
GRID_W = 64
CHUNK = 64
CONV_W = 5
POS_BASE = 10000.0
EPS = 1e-6
N_DIR = 2
DEPTH = 2
ALPHA = (2 * DEPTH) ** 0.25

A_HEADS, A_DK, A_DV = 4, 128, 128
B_HEADS, B_HEADDIM, B_GROUPS, B_STATE = 8, 64, 2, 128
C_HEADS, C_DK, C_DV = 4, 128, 256
PEER_HEADS, PEER_NKEYS, PEER_DKEY, PEER_TOPK = 8, 128, 256, 16

LANE = 128
ROW_TILE = 256
ROUTE_TILE = 512
EXPERT_TOK = 512
EXPERT_BLK = 1024
GATE_ROWS = 32
VMEM_LIMIT = 56 * 1024 * 1024
N_CAND_ROWS = -(-sum((PEER_TOPK + 1) // (p + 1) for p in range(PEER_TOPK + 1)) // 8) * 8

NN = (((1,), (0,)), ((), ()))
NT = (((1,), (1,)), ((), ()))


def _params(*sem):
    return pltpu.CompilerParams(dimension_semantics=sem, vmem_limit_bytes=VMEM_LIMIT)


def _dot(a, b, dims=NN):
    return lax.dot_general(a, b, dims, preferred_element_type=f32)


def _split(a):
    hi = a.astype(bf16)
    return hi, (a - hi.astype(f32)).astype(bf16)


def _dot3(a, b, dims=NN):
    ah, al = _split(a)
    bh, bl = _split(b)
    return _dot(ah, bh, dims) + (_dot(ah, bl, dims) + _dot(al, bh, dims))


def _sigmoid(x):
    return 1.0 / (1.0 + jnp.exp(-x))


def _silu(x):
    return x * _sigmoid(x)


def _softplus(x):
    return jnp.maximum(x, 0.0) + jnp.log1p(jnp.exp(-jnp.abs(x)))


def _layer_norm(z, g, b):
    mu = jnp.mean(z, -1, keepdims=True)
    zc = z - mu
    var = jnp.mean(zc * zc, -1, keepdims=True)
    return zc * lax.rsqrt(var + EPS) * g + b


def _rms(x, g):
    return x * lax.rsqrt(jnp.mean(x * x, -1, keepdims=True) + EPS) * g


def _chunk_masks(d):
    row = lax.broadcasted_iota(jnp.int32, (CHUNK, CHUNK), 0)
    col = lax.broadcasted_iota(jnp.int32, (CHUNK, CHUNK), 1)
    diff = row - col if d == 0 else col - row
    return diff >= 0, diff > 0, diff <= 0, row == col


def _cum(mask_incl, mask_incl_t, x_col, x_row):
    c_col = jnp.sum(jnp.where(mask_incl, x_row, 0.0), axis=1, keepdims=True)
    c_row = jnp.sum(jnp.where(mask_incl_t, x_col, 0.0), axis=0, keepdims=True)
    tot = jnp.sum(x_col, axis=0, keepdims=True)
    return c_col, c_row, tot


def _ada_kernel(c_ref, w_ref, b_ref, o_ref):
    o_ref[...] = _dot3(_silu(c_ref[...]), w_ref[...]) + b_ref[...]


def _ada(cond, w, b):
    rows, d = cond.shape
    n = w.shape[1]
    return pl.pallas_call(
        _ada_kernel,
        out_shape=jax.ShapeDtypeStruct((rows, n), f32),
        grid=(n // d,),
        in_specs=[pl.BlockSpec((rows, d), lambda j: (0, 0)),
                  pl.BlockSpec((d, d), lambda j: (0, j)),
                  pl.BlockSpec((1, d), lambda j: (0, j))],
        out_specs=pl.BlockSpec((rows, d), lambda j: (0, j)),
        compiler_params=_params("parallel"),
    )(cond, w, b.reshape(1, n))


def _proj_kernel(x_ref, sh_ref, sc_ref, w_ref, o_ref):
    xm = x_ref[...] * (1.0 + sc_ref[0]) + sh_ref[0]
    o_ref[...] = _dot(xm.astype(bf16), w_ref[...])


def _proj(x, shift_t, scale_t, w):
    rows, d = x.shape
    n = w.shape[1]
    return pl.pallas_call(
        _proj_kernel,
        out_shape=jax.ShapeDtypeStruct((rows, n), f32),
        grid=(rows // ROW_TILE,),
        in_specs=[pl.BlockSpec((ROW_TILE, d), lambda i: (i, 0)),
                  pl.BlockSpec((1, 1, d), lambda i: (i, 0, 0)),
                  pl.BlockSpec((1, 1, d), lambda i: (i, 0, 0)),
                  pl.BlockSpec((d, n), lambda i: (0, 0))],
        out_specs=pl.BlockSpec((ROW_TILE, n), lambda i: (i, 0)),
        compiler_params=_params("parallel"),
    )(x, shift_t, scale_t, w)


def _conv_kernel(x_ref, w_ref, b_ref, o_ref, pad_ref, *, seq, seg0, n_norm, n_qscale, qscale):
    j = pl.program_id(1)
    halo = 8
    pad_ref[0:halo, :] = jnp.zeros((halo, LANE), f32)
    pad_ref[halo + seq:2 * halo + seq, :] = jnp.zeros((halo, LANE), f32)
    pad_ref[halo:halo + seq, :] = x_ref[0]
    w = w_ref[...]
    bias = b_ref[...]
    scale = jnp.where(j < n_qscale, qscale, 1.0).astype(f32)
    norm_on = jnp.where(j < n_norm, 1.0, 0.0).astype(f32)
    rows = ROW_TILE

    def body(c, carry):
        r0 = pl.multiple_of(c * rows, rows)
        win = pad_ref[pl.ds(r0, rows + 2 * halo), :]
        t = r0 + lax.broadcasted_iota(jnp.int32, (rows, LANE), 0)
        acc = bias + w[2:3] * win[halo:halo + rows]
        for k in (0, 1, 3, 4):
            off = k - CONV_W // 2
            tap = win[halo + off:halo + off + rows]
            same_segment = ((t + off) >= seg0) == (t >= seg0)
            acc = acc + w[k:k + 1] * jnp.where(same_segment, tap, 0.0)
        y = _silu(acc)
        yn = y * lax.rsqrt(jnp.sum(y * y, -1, keepdims=True) + EPS)
        y = (norm_on * yn + (1.0 - norm_on) * y) * scale
        o_ref[0, pl.ds(r0, rows), :] = y
        return carry

    lax.fori_loop(0, seq // rows, body, 0)


def _conv_act(x, w, b, col_blocks, *, seg0, n_norm, n_qscale, qscale):
    bsz, seq, _ = x.shape
    n_out = len(col_blocks)
    first_gap = next((i for i, cb in enumerate(col_blocks) if cb != i), n_out)
    gap = col_blocks[first_gap] - first_gap if first_gap < n_out else 0
    in_map = lambda bi, j: (bi, 0, jnp.where(j < first_gap, j, j + gap))
    kern = functools.partial(_conv_kernel, seq=seq, seg0=seg0, n_norm=n_norm, n_qscale=n_qscale, qscale=qscale)
    return pl.pallas_call(
        kern,
        out_shape=jax.ShapeDtypeStruct((bsz, seq, n_out * LANE), f32),
        grid=(bsz, n_out),
        in_specs=[pl.BlockSpec((1, seq, LANE), in_map),
                  pl.BlockSpec((CONV_W, LANE), lambda bi, j: (0, j)),
                  pl.BlockSpec((1, LANE), lambda bi, j: (0, j))],
        out_specs=pl.BlockSpec((1, seq, LANE), lambda bi, j: (bi, 0, j)),
        scratch_shapes=[pltpu.VMEM((seq + 16, LANE), f32)],
        compiler_params=_params("parallel", "parallel"),
    )(x, w, b)


def _chunk_index(d, n, n_ctx, n_all):
    return n if d == 0 else jnp.where(n < n_ctx, n_ctx - 1 - n, n_all + n_ctx - 1 - n)


def _both_directions(make_specs):
    return [spec for d in range(N_DIR) for spec in make_specs(d)]


def _gate_layouts(raw, heads):
    bsz, seq, _ = raw.shape
    r = raw.reshape(bsz, seq // CHUNK, CHUNK, N_DIR, heads)
    return r.transpose(3, 0, 1, 2, 4), r.transpose(3, 0, 1, 4, 2)


def _param_layouts(*ps):
    return jnp.stack(ps, axis=1).astype(f32), jnp.stack(ps, axis=2).astype(f32)


def _gdn_kernel(*refs, bsz):
    n_in = 7
    ins = [refs[d * n_in:(d + 1) * n_in] for d in range(N_DIR)]
    pcol_ref, prow_ref = refs[N_DIR * n_in:N_DIR * n_in + 2]
    o_refs = refs[N_DIR * n_in + 2:N_DIR * n_in + 2 + N_DIR]
    st_ref = refs[-1]
    n = pl.program_id(0)

    @pl.when(n == 0)
    def _():
        st_ref[...] = jnp.zeros(st_ref.shape, f32)

    masks = [_chunk_masks(d) for d in range(N_DIR)]
    eye = masks[0][3].astype(f32)
    row = lax.broadcasted_iota(jnp.int32, (CHUNK, CHUNK), 0)
    col = lax.broadcasted_iota(jnp.int32, (CHUNK, CHUNK), 1)
    blocks = [(row >> sh) == (col >> sh) for sh in (3, 4, 5, 6)]
    off_masks = [blocks[lvl + 1] & ~blocks[lvl] for lvl in range(3)]
    chains = [(d, b, h) for d in range(N_DIR) for b in range(bsz) for h in range(A_HEADS)]
    idx = range(len(chains))
    pc = [pcol_ref[d] for d in range(N_DIR)]
    pr = [prow_ref[d] for d in range(N_DIR)]
    beta_col = [[_sigmoid(ins[d][3][0, b, 0]) for b in range(bsz)] for d in range(N_DIR)]
    g_col = [[-jnp.exp(pc[d][0:1, :]) * _softplus(ins[d][5][0, b, 0] + pc[d][1:2, :]) for b in range(bsz)]
             for d in range(N_DIR)]
    g_row = [[-jnp.exp(pr[d][:, 0:1]) * _softplus(ins[d][6][0, b, 0] + pr[d][:, 1:2]) for b in range(bsz)]
             for d in range(N_DIR)]
    sl = [slice(h * A_DK, (h + 1) * A_DK) for _, _, h in chains]
    q = [ins[d][0][b, :, sl[c]] for c, (d, b, _) in enumerate(chains)]
    k = [ins[d][1][b, :, sl[c]] for c, (d, b, _) in enumerate(chains)]
    v = [ins[d][2][b, :, sl[c]] for c, (d, b, _) in enumerate(chains)]
    cums = [_cum(masks[d][0], masks[d][2], g_col[d][b][:, h:h + 1], g_row[d][b][h:h + 1, :]) for d, b, h in chains]
    decay = [jnp.exp(jnp.where(masks[d][0], cums[c][0] - cums[c][1], -jnp.inf)) for c, (d, _, _) in enumerate(chains)]
    b_col = [beta_col[d][b][:, h:h + 1] for d, b, h in chains]
    kb = [k[c] * b_col[c] for c in idx]
    l_mat = [_dot3(kb[c], k[c], NT) * jnp.where(masks[d][1], decay[c], 0.0) for c, (d, _, _) in enumerate(chains)]
    nil = [-jnp.where(blocks[0], l_mat[c], 0.0) for c in idx]
    t_inv = [eye + nil[c] for c in idx]
    for _ in range(2):
        nil = [_dot3(nil[c], nil[c]) for c in idx]
        t_inv = [t_inv[c] + _dot3(t_inv[c], nil[c]) for c in idx]
    for lvl in range(3):
        left = [_dot3(t_inv[c], jnp.where(off_masks[lvl], l_mat[c], 0.0)) for c in idx]
        t_inv = [t_inv[c] - _dot3(left[c], t_inv[c]) for c in idx]
    e_col = [jnp.exp(gc) for gc, _, _ in cums]
    uw = [_dot3(t_inv[c], jnp.concatenate([v[c] * b_col[c], kb[c] * e_col[c]], axis=1)) for c in idx]
    qk = [_dot(q[c], k[c], NT) * decay[c] for c in idx]
    k_dec_t = [(k[c] * jnp.exp(cums[c][2] - cums[c][0])).T for c in idx]
    s = [st_ref[c] for c in idx]
    wq = [_dot(jnp.concatenate([uw[c][:, A_DV:], q[c] * e_col[c]], axis=0), s[c]) for c in idx]
    v_new = [uw[c][:, :A_DV] - wq[c][:CHUNK] for c in idx]
    out = [wq[c][CHUNK:] + _dot(qk[c], v_new[c]) for c in idx]
    s_new = [jnp.exp(cums[c][2]) * s[c] + _dot(k_dec_t[c], v_new[c]) for c in idx]
    for c, (d, b, _) in enumerate(chains):
        o_refs[d][b, :, sl[c]] = out[c]
        st_ref[c] = s_new[c]


def _gdn(act, beta_raw, decay_raw, a_log, dt_bias, n_ctx):
    bsz, seq, _ = act.shape
    n_all = seq // CHUNK
    hw = A_HEADS * A_DK
    bcol, brow = _gate_layouts(beta_raw, A_HEADS)
    dcol, drow = _gate_layouts(decay_raw, A_HEADS)
    pcol, prow = _param_layouts(a_log, dt_bias)
    ci = functools.partial(_chunk_index, n_ctx=n_ctx, n_all=n_all)

    def specs(d):
        tok = lambda blk: pl.BlockSpec((bsz, CHUNK, hw), lambda n: (0, ci(d, n), blk))
        gcol = pl.BlockSpec((1, bsz, 1, CHUNK, A_HEADS), lambda n: (d, 0, ci(d, n), 0, 0))
        grow = pl.BlockSpec((1, bsz, 1, A_HEADS, CHUNK), lambda n: (d, 0, ci(d, n), 0, 0))
        return [tok(0), tok(1), tok(2), gcol, grow, gcol, grow]

    out = jax.ShapeDtypeStruct((bsz, seq, hw), f32)
    return pl.pallas_call(
        functools.partial(_gdn_kernel, bsz=bsz),
        out_shape=(out,) * N_DIR,
        grid=(n_all,),
        in_specs=_both_directions(specs) + [pl.BlockSpec((N_DIR, 2, A_HEADS), lambda n: (0, 0, 0)),
                                            pl.BlockSpec((N_DIR, A_HEADS, 2), lambda n: (0, 0, 0))],
        out_specs=tuple(pl.BlockSpec((bsz, CHUNK, hw), lambda n, d=d: (0, ci(d, n), 0)) for d in range(N_DIR)),
        scratch_shapes=[pltpu.VMEM((N_DIR * bsz * A_HEADS, A_DK, A_DV), f32)],
        compiler_params=_params("arbitrary"),
    )(*([act, act, act, bcol, brow, dcol, drow] * N_DIR), pcol, prow)


def _ssd_kernel(*refs, bsz):
    n_in = 5
    ins = [refs[d * n_in:(d + 1) * n_in] for d in range(N_DIR)]
    pcol_ref, prow_ref = refs[N_DIR * n_in:N_DIR * n_in + 2]
    o_refs = refs[N_DIR * n_in + 2:N_DIR * n_in + 2 + N_DIR]
    st_ref = refs[-1]
    n = pl.program_id(0)

    @pl.when(n == 0)
    def _():
        st_ref[...] = jnp.zeros(st_ref.shape, f32)

    masks = [_chunk_masks(d) for d in range(N_DIR)]
    rep = B_HEADS // B_GROUPS
    streams = [(d, b) for d in range(N_DIR) for b in range(bsz)]
    groups = [(s, g) for s in range(len(streams)) for g in range(B_GROUPS)]
    heads = [(s, h) for s in range(len(streams)) for h in range(B_HEADS)]
    gsl = [slice(g * B_STATE, (g + 1) * B_STATE) for _, g in groups]
    hsl = [slice(h * B_HEADDIM, (h + 1) * B_HEADDIM) for _, h in heads]
    pc = [pcol_ref[d] for d, _ in streams]
    pr = [prow_ref[d] for d, _ in streams]
    dt_col = [_softplus(ins[d][3][0, b, 0] + pc[s][1:2, :]) for s, (d, b) in enumerate(streams)]
    dt_row = [_softplus(ins[d][4][0, b, 0] + pr[s][:, 1:2]) for s, (d, b) in enumerate(streams)]
    da_col = [dt_col[s] * (-jnp.exp(pc[s][0:1, :])) for s in range(len(streams))]
    da_row = [dt_row[s] * (-jnp.exp(pr[s][:, 0:1])) for s in range(len(streams))]
    bm = [ins[streams[s][0]][1][streams[s][1], :, gsl[i]] for i, (s, _) in enumerate(groups)]
    cm = [ins[streams[s][0]][2][streams[s][1], :, gsl[i]] for i, (s, _) in enumerate(groups)]
    hs = [st_ref[i] for i in range(len(groups))]
    cb = [_dot(cm[i], bm[i], NT) for i in range(len(groups))]
    y_off = [_dot(cm[i], hs[i]) for i in range(len(groups))]
    bm_t = [bm[i].T for i in range(len(groups))]
    m_incl = [masks[streams[s][0]][0] for s, _ in heads]
    m_incl_t = [masks[streams[s][0]][2] for s, _ in heads]
    cums = [_cum(m_incl[j], m_incl_t[j], da_col[s][:, h:h + 1], da_row[s][h:h + 1, :]) for j, (s, h) in enumerate(heads)]
    seg = [jnp.exp(jnp.where(m_incl[j], cums[j][0] - cums[j][1], -jnp.inf)) for j in range(len(heads))]
    xdt = [ins[streams[s][0]][0][streams[s][1], :, hsl[j]] * dt_col[s][:, h:h + 1] for j, (s, h) in enumerate(heads)]
    y_diag = [_dot(cb[j // rep] * seg[j], xdt[j]) for j in range(len(heads))]
    for j, (s, h) in enumerate(heads):
        r = h % rep
        d, b = streams[s]
        o_refs[d][b, :, hsl[j]] = y_diag[j] + jnp.exp(cums[j][0]) * y_off[j // rep][:, r * B_HEADDIM:(r + 1) * B_HEADDIM]
    xdt_end = [xdt[j] * jnp.exp(cums[j][2] - cums[j][0]) for j in range(len(heads))]
    keep = [jnp.broadcast_to(jnp.exp(cums[j][2]), (1, B_HEADDIM)) for j in range(len(heads))]
    upd = [_dot(bm_t[i], jnp.concatenate(xdt_end[i * rep:(i + 1) * rep], axis=1)) for i in range(len(groups))]
    for i in range(len(groups)):
        st_ref[i] = jnp.concatenate(keep[i * rep:(i + 1) * rep], axis=1) * hs[i] + upd[i]


def _ssd(act, dt_raw, a_log, dt_bias, n_ctx):
    bsz, seq, _ = act.shape
    n_all = seq // CHUNK
    dinner = B_HEADS * B_HEADDIM
    gw = B_GROUPS * B_STATE
    tcol, trow = _gate_layouts(dt_raw, B_HEADS)
    pcol, prow = _param_layouts(a_log, dt_bias)
    ci = functools.partial(_chunk_index, n_ctx=n_ctx, n_all=n_all)
    x_off = (A_HEADS * (2 * A_DK + A_DV)) // dinner
    bm_off = (A_HEADS * (2 * A_DK + A_DV) + dinner) // gw

    def specs(d):
        return [pl.BlockSpec((bsz, CHUNK, dinner), lambda n: (0, ci(d, n), x_off)),
                pl.BlockSpec((bsz, CHUNK, gw), lambda n: (0, ci(d, n), bm_off)),
                pl.BlockSpec((bsz, CHUNK, gw), lambda n: (0, ci(d, n), bm_off + 1)),
                pl.BlockSpec((1, bsz, 1, CHUNK, B_HEADS), lambda n: (d, 0, ci(d, n), 0, 0)),
                pl.BlockSpec((1, bsz, 1, B_HEADS, CHUNK), lambda n: (d, 0, ci(d, n), 0, 0))]

    out = jax.ShapeDtypeStruct((bsz, seq, dinner), f32)
    return pl.pallas_call(
        functools.partial(_ssd_kernel, bsz=bsz),
        out_shape=(out,) * N_DIR,
        grid=(n_all,),
        in_specs=_both_directions(specs) + [pl.BlockSpec((N_DIR, 2, B_HEADS), lambda n: (0, 0, 0)),
                                            pl.BlockSpec((N_DIR, B_HEADS, 2), lambda n: (0, 0, 0))],
        out_specs=tuple(pl.BlockSpec((bsz, CHUNK, dinner), lambda n, d=d: (0, ci(d, n), 0)) for d in range(N_DIR)),
        scratch_shapes=[pltpu.VMEM((N_DIR * bsz * B_GROUPS, B_STATE, (B_HEADS // B_GROUPS) * B_HEADDIM), f32)],
        compiler_params=_params("arbitrary"),
    )(*([act, act, act, tcol, trow] * N_DIR), pcol, prow)


def _mlstm_kernel(*refs, bsz):
    n_in = 7
    ins = [refs[d * n_in:(d + 1) * n_in] for d in range(N_DIR)]
    pcol_ref, prow_ref = refs[N_DIR * n_in:N_DIR * n_in + 2]
    o_refs = refs[N_DIR * n_in + 2:N_DIR * n_in + 2 + N_DIR]
    c_ref, n_ref, m_ref = refs[-3:]
    n = pl.program_id(0)

    @pl.when(n == 0)
    def _():
        c_ref[...] = jnp.zeros(c_ref.shape, f32)
        n_ref[...] = jnp.zeros(n_ref.shape, f32)
        m_ref[...] = jnp.zeros(m_ref.shape, f32)

    masks = [_chunk_masks(d) for d in range(N_DIR)]
    streams = [(d, b) for d in range(N_DIR) for b in range(bsz)]
    chains = [(s, h) for s in range(len(streams)) for h in range(C_HEADS)]
    idx = range(len(chains))
    pc = [pcol_ref[d] for d, _ in streams]
    pr = [prow_ref[d] for d, _ in streams]
    li_col = [ins[d][3][0, b, 0] + pc[s][0:1, :] for s, (d, b) in enumerate(streams)]
    li_row = [ins[d][4][0, b, 0] + pr[s][:, 0:1] for s, (d, b) in enumerate(streams)]
    lf_col = [-_softplus(-(ins[d][5][0, b, 0] + pc[s][1:2, :])) for s, (d, b) in enumerate(streams)]
    lf_row = [-_softplus(-(ins[d][6][0, b, 0] + pr[s][:, 1:2])) for s, (d, b) in enumerate(streams)]
    q = [ins[streams[s][0]][0][streams[s][1], :, h * C_DK:(h + 1) * C_DK] for s, h in chains]
    k = [ins[streams[s][0]][1][streams[s][1], :, h * C_DK:(h + 1) * C_DK] for s, h in chains]
    v = [ins[streams[s][0]][2][streams[s][1], :, h * C_DV:(h + 1) * C_DV] for s, h in chains]
    m_incl = [masks[streams[s][0]][0] for s, _ in chains]
    m_incl_t = [masks[streams[s][0]][2] for s, _ in chains]
    cums = [_cum(m_incl[c], m_incl_t[c], lf_col[s][:, h:h + 1], lf_row[s][h:h + 1, :]) for c, (s, h) in enumerate(chains)]
    logw_intra = [jnp.where(m_incl[c], cums[c][0] - cums[c][1] + li_row[s][h:h + 1, :], -jnp.inf)
                  for c, (s, h) in enumerate(chains)]
    m_prev = [m_ref[c] for c in idx]
    logw_inter = [cums[c][0] + m_prev[c] for c in idx]
    m_t = [jnp.maximum(logw_inter[c], jnp.max(logw_intra[c], axis=1, keepdims=True)) for c in idx]
    qk = [_dot(q[c], k[c], NT) for c in idx]
    c_mem = [c_ref[c] for c in idx]
    n_mem = [n_ref[c] for c in idx]
    qc = [_dot(q[c], c_mem[c]) for c in idx]
    s = [qk[c] * jnp.exp(logw_intra[c] - m_t[c]) for c in idx]
    w_inter = [jnp.exp(logw_inter[c] - m_t[c]) for c in idx]
    num = [_dot(s[c], v[c]) + w_inter[c] * qc[c] for c in idx]
    den = [jnp.sum(s[c], axis=1, keepdims=True) + w_inter[c] * jnp.sum(q[c] * n_mem[c], axis=1, keepdims=True)
           for c in idx]
    for c, (s, h) in enumerate(chains):
        d, b = streams[s]
        o_refs[d][b, :, h * C_DV:(h + 1) * C_DV] = num[c] / jnp.maximum(jnp.abs(den[c]), jnp.exp(-m_t[c]))
    logw_end = [cums[c][2] - cums[c][0] + li_col[s][:, h:h + 1] for c, (s, h) in enumerate(chains)]
    m_new = [jnp.maximum(cums[c][2] + m_prev[c], jnp.max(logw_end[c], axis=0, keepdims=True)) for c in idx]
    kw = [k[c] * jnp.exp(logw_end[c] - m_new[c]) for c in idx]
    keep = [jnp.exp(cums[c][2] + m_prev[c] - m_new[c]) for c in idx]
    kv = [_dot(kw[c].T, v[c]) for c in idx]
    for c in idx:
        c_ref[c] = keep[c] * c_mem[c] + kv[c]
        n_ref[c] = keep[c] * n_mem[c] + jnp.sum(kw[c], axis=0, keepdims=True)
        m_ref[c] = m_new[c]


def _mlstm(act, proj, i_raw, f_raw, i_bias, f_bias, n_ctx):
    bsz, seq, _ = act.shape
    n_all = seq // CHUNK
    qw = C_HEADS * C_DK
    vw = C_HEADS * C_DV
    icol, irow = _gate_layouts(i_raw, C_HEADS)
    fcol, frow = _gate_layouts(f_raw, C_HEADS)
    pcol, prow = _param_layouts(i_bias, f_bias)
    ci = functools.partial(_chunk_index, n_ctx=n_ctx, n_all=n_all)

    def specs(d):
        gcol = pl.BlockSpec((1, bsz, 1, CHUNK, C_HEADS), lambda n: (d, 0, ci(d, n), 0, 0))
        grow = pl.BlockSpec((1, bsz, 1, C_HEADS, CHUNK), lambda n: (d, 0, ci(d, n), 0, 0))
        return [pl.BlockSpec((bsz, CHUNK, qw), lambda n: (0, ci(d, n), 0)),
                pl.BlockSpec((bsz, CHUNK, qw), lambda n: (0, ci(d, n), 1)),
                pl.BlockSpec((bsz, CHUNK, vw), lambda n: (0, ci(d, n), 2 * qw // vw)),
                gcol, grow, gcol, grow]

    out = jax.ShapeDtypeStruct((bsz, seq, vw), f32)
    n_chain = N_DIR * bsz * C_HEADS
    return pl.pallas_call(
        functools.partial(_mlstm_kernel, bsz=bsz),
        out_shape=(out,) * N_DIR,
        grid=(n_all,),
        in_specs=_both_directions(specs) + [pl.BlockSpec((N_DIR, 2, C_HEADS), lambda n: (0, 0, 0)),
                                            pl.BlockSpec((N_DIR, C_HEADS, 2), lambda n: (0, 0, 0))],
        out_specs=tuple(pl.BlockSpec((bsz, CHUNK, vw), lambda n, d=d: (0, ci(d, n), 0)) for d in range(N_DIR)),
        scratch_shapes=[pltpu.VMEM((n_chain, C_DK, C_DV), f32),
                        pltpu.VMEM((n_chain, 1, C_DK), f32),
                        pltpu.VMEM((n_chain, 1, 1), f32)],
        compiler_params=_params("arbitrary"),
    )(*([act, act, proj, icol, irow, fcol, frow] * N_DIR), pcol, prow)


def _merge_even_kernel(x_ref, af_ref, ar_ref, bf_ref, br_ref, za_ref, zb_ref, xs_ref, na_ref, nb_ref, dsk_ref,
                       w_ref, g_ref, lg_ref, lb_ref, o_ref):
    a = af_ref[...] + ar_ref[...]
    za = za_ref[...]
    na = na_ref[...]
    parts = []
    for h in range(A_HEADS):
        sl = slice(h * A_DV, (h + 1) * A_DV)
        parts.append(_rms(a[:, sl], na) * _silu(za[:, sl]))
    yb = (bf_ref[...] + br_ref[...] + dsk_ref[...] * xs_ref[...]) * _silu(zb_ref[...])
    gw = (B_HEADS * B_HEADDIM) // B_GROUPS
    nb = nb_ref[...]
    for g in range(B_GROUPS):
        sl = slice(g * gw, (g + 1) * gw)
        parts.append(_rms(yb[:, sl], nb[:, sl]))
    y = jnp.concatenate(parts, axis=1).astype(bf16)
    z = ALPHA * x_ref[...] + g_ref[0] * _dot(y, w_ref[...])
    o_ref[...] = _layer_norm(z, lg_ref[...], lb_ref[...])


def _merge_even(x, o_a, o_b, proj, act, norm_a, norm_b, d_skip, w_out, gate_t, ln_g, ln_b):
    rows, d = x.shape
    aw = A_HEADS * A_DV
    bw = B_HEADS * B_HEADDIM
    row = lambda w_, blk: pl.BlockSpec((ROW_TILE, w_), lambda i: (i, blk))
    vec = lambda w_: pl.BlockSpec((1, w_), lambda i: (0, 0))
    qkv = A_HEADS * (2 * A_DK + A_DV)
    return pl.pallas_call(
        _merge_even_kernel,
        out_shape=jax.ShapeDtypeStruct((rows, d), f32),
        grid=(rows // ROW_TILE,),
        in_specs=[row(d, 0), row(aw, 0), row(aw, 0), row(bw, 0), row(bw, 0),
                  row(aw, qkv // aw), row(bw, (qkv + aw) // bw), row(bw, qkv // bw),
                  vec(A_DV), vec(bw), vec(bw),
                  pl.BlockSpec((aw + bw, d), lambda i: (0, 0)),
                  pl.BlockSpec((1, 1, d), lambda i: (i, 0, 0)),
                  vec(d), vec(d)],
        out_specs=row(d, 0),
        compiler_params=_params("parallel"),
    )(x, *o_a, *o_b, proj, proj, act, norm_a.reshape(1, -1), norm_b.reshape(1, -1),
      jnp.repeat(d_skip, B_HEADDIM).reshape(1, -1), w_out, gate_t, ln_g.reshape(1, -1), ln_b.reshape(1, -1))


def _merge_odd_kernel(hf_ref, hr_ref, o_ref_in, nc_ref, w_ref, out_ref):
    hsum = hf_ref[...] + hr_ref[...]
    o = o_ref_in[...]
    nc = nc_ref[...]
    parts = []
    for h in range(C_HEADS):
        sl = slice(h * C_DV, (h + 1) * C_DV)
        parts.append(_rms(hsum[:, sl], nc) * _sigmoid(o[:, sl]))
    y = jnp.concatenate(parts, axis=1).astype(bf16)
    out_ref[...] = _dot(y, w_ref[...])


def _merge_odd(h, proj, norm_c, w_out, tiles_per_batch, ctx_tiles, bsz):
    vw = C_HEADS * C_DV
    lat_tiles = tiles_per_batch - ctx_tiles
    src = lambda i: i + ctx_tiles * (i // lat_tiles + 1)
    d = w_out.shape[1]
    return pl.pallas_call(
        _merge_odd_kernel,
        out_shape=jax.ShapeDtypeStruct((bsz * lat_tiles * ROW_TILE, d), f32),
        grid=(bsz * lat_tiles,),
        in_specs=[pl.BlockSpec((ROW_TILE, vw), lambda i: (src(i), 0)),
                  pl.BlockSpec((ROW_TILE, vw), lambda i: (src(i), 0)),
                  pl.BlockSpec((ROW_TILE, vw), lambda i: (src(i), 2 * C_HEADS * C_DK // vw + 1)),
                  pl.BlockSpec((1, C_DV), lambda i: (0, 0)),
                  pl.BlockSpec((vw, d), lambda i: (0, 0))],
        out_specs=pl.BlockSpec((ROW_TILE, d), lambda i: (i, 0)),
        compiler_params=_params("parallel"),
    )(*h, proj, norm_c.reshape(1, -1), w_out)


def _resid_ln_kernel(x_ref, y_ref, g_ref, lg_ref, lb_ref, o_ref, *, transposed):
    y = y_ref[...]
    if transposed:
        y = y.T
    o_ref[...] = _layer_norm(ALPHA * x_ref[...] + g_ref[0] * y, lg_ref[...], lb_ref[...])


def _resid_ln(x, y, gate_t, ln_g, ln_b, *, transposed):
    rows, d = x.shape
    y_spec = (pl.BlockSpec((d, ROW_TILE), lambda i: (0, i)) if transposed
              else pl.BlockSpec((ROW_TILE, d), lambda i: (i, 0)))
    return pl.pallas_call(
        functools.partial(_resid_ln_kernel, transposed=transposed),
        out_shape=jax.ShapeDtypeStruct((rows, d), f32),
        grid=(rows // ROW_TILE,),
        in_specs=[pl.BlockSpec((ROW_TILE, d), lambda i: (i, 0)), y_spec,
                  pl.BlockSpec((1, 1, d), lambda i: (i, 0, 0)),
                  pl.BlockSpec((1, d), lambda i: (0, 0)),
                  pl.BlockSpec((1, d), lambda i: (0, 0))],
        out_specs=pl.BlockSpec((ROW_TILE, d), lambda i: (i, 0)),
        compiler_params=_params("parallel"),
    )(x, y, gate_t, ln_g.reshape(1, -1), ln_b.reshape(1, -1))


def _oddeven_merge(lo, hi, r):
    step = r * 2
    if step < hi - lo:
        yield from _oddeven_merge(lo, hi, step)
        yield from _oddeven_merge(lo + r, hi, step)
        yield from [(i, i + r) for i in range(lo + r, hi - r, step)]
    else:
        yield (lo, lo + r)


def _oddeven_sort(lo, hi):
    if hi - lo >= 1:
        mid = lo + (hi - lo) // 2
        yield from _oddeven_sort(lo, mid)
        yield from _oddeven_sort(mid + 1, hi)
        yield from _oddeven_merge(lo, hi, 1)


def _exchange(a, i, j):
    a[i], a[j] = jnp.maximum(a[i], a[j]), jnp.minimum(a[i], a[j])


def _top_sorted(slabs):
    a = list(slabs)
    for i, j in _oddeven_sort(0, len(a) - 1):
        _exchange(a, i, j)
    for shift in (4, 2, 1):
        other = [pltpu.roll(x, shift, axis=0) for x in a]
        if len(a) < PEER_TOPK:
            a = a + other[::-1]
        else:
            a = [jnp.maximum(a[k], other[PEER_TOPK - 1 - k]) for k in range(PEER_TOPK)]
        dist = PEER_TOPK // 2
        while dist >= 1:
            for i in range(PEER_TOPK):
                if i & dist == 0:
                    _exchange(a, i, i + dist)
            dist //= 2
    return a


def _next_largest(slabs, kth):
    count = sum(jnp.where(s >= kth, 1.0, 0.0) for s in slabs)
    below = functools.reduce(jnp.maximum, [jnp.where(s < kth, s, -jnp.inf) for s in slabs])
    count = jnp.sum(count, axis=0, keepdims=True)
    below = jnp.max(below, axis=0, keepdims=True)
    return jnp.where(count > PEER_TOPK, kth, below)


def _top_values(s):
    slabs = [s[i:i + 8] for i in range(0, s.shape[0], 8)]
    if len(slabs) < 8:
        slabs = slabs + [jnp.full_like(slabs[0], -jnp.inf)] * (8 - len(slabs))
    top = [t[0:1] for t in _top_sorted(slabs)]
    return top + [_next_largest(slabs, top[-1])]


def _route_kernel(x_ref, sh_ref, sc_ref, wh_ref, wl_ref, kh_ref, kl_ref,
                  xm_ref, thr_ref, s2_ref, e1_ref, e2_ref, xh_ref, xl_ref, sc_scr, cand_ref):
    for t in range(ROUTE_TILE // ROW_TILE):
        rows = slice(t * ROW_TILE, (t + 1) * ROW_TILE)
        xm = x_ref[rows, :] * (1.0 + sc_ref[t]) + sh_ref[t]
        xm_ref[rows, :] = xm.astype(bf16)
        xh_ref[rows, :], xl_ref[rows, :] = _split(xm)
    half = PEER_DKEY // 2
    n_top = PEER_TOPK + 1
    pairs = [(p, r) for p in range(n_top) for r in range(n_top) if (p + 1) * (r + 1) <= n_top]
    cand_ref[...] = jnp.full(cand_ref.shape, -jnp.inf, f32)

    def scores(h, sc_scr):
        wsl = pl.ds(pl.multiple_of(h * PEER_DKEY, PEER_DKEY), PEER_DKEY)
        xh = xh_ref[...]
        q = _dot(xh, wh_ref[:, wsl]) + (_dot(xh, wl_ref[:, wsl]) + _dot(xl_ref[...], wh_ref[:, wsl]))
        for c in range(2):
            qh, ql = _split(q[:, c * half:(c + 1) * half])
            kh = kh_ref[h, c]
            sc_scr[c] = _dot(kh, qh, NT) + (_dot(kl_ref[h, c], qh, NT) + _dot(kh, ql, NT))

    def select(h, sc_scr):
        for lb in range(ROUTE_TILE // LANE):
            lanes = slice(lb * LANE, (lb + 1) * LANE)
            s1 = sc_scr[0, :, lanes]
            s2 = sc_scr[1, :, lanes]
            t1 = _top_values(s1)
            t2 = _top_values(s2)
            for i, (p, r) in enumerate(pairs):
                cand_ref[i:i + 1, lanes] = t1[p] + t2[r]
            best = _top_values(cand_ref[:, lanes])
            z = sum(jnp.exp(bv - best[0]) for bv in best[:PEER_TOPK])
            tau = 0.5 * (best[PEER_TOPK - 1] + best[PEER_TOPK])
            thr_ref[h, :, lanes] = tau - s1
            s2_ref[h, :, lanes] = s2
            e1_ref[h, :, lanes] = jnp.exp(s1 - t1[0]) / z
            e2_ref[h, :, lanes] = jnp.exp(s2 - t2[0])

    def head(h, carry):
        scores(h, sc_scr)
        select(h, sc_scr)
        return carry

    lax.fori_loop(0, PEER_HEADS, head, 0)


def _route(x, shift_t, scale_t, wq_hi, wq_lo, keys_hi, keys_lo):
    rows, d = x.shape
    nq = wq_hi.shape[1]
    half = PEER_DKEY // 2
    score = jax.ShapeDtypeStruct((PEER_HEADS, PEER_NKEYS, rows), f32)
    score_spec = pl.BlockSpec((PEER_HEADS, PEER_NKEYS, ROUTE_TILE), lambda i: (0, 0, i))
    key_spec = pl.BlockSpec((PEER_HEADS, 2, PEER_NKEYS, half), lambda i: (0, 0, 0, 0))
    mods = ROUTE_TILE // ROW_TILE
    return pl.pallas_call(
        _route_kernel,
        out_shape=(jax.ShapeDtypeStruct((rows, d), bf16), score, score, score, score),
        grid=(rows // ROUTE_TILE,),
        in_specs=[pl.BlockSpec((ROUTE_TILE, d), lambda i: (i, 0)),
                  pl.BlockSpec((mods, 1, d), lambda i: (i, 0, 0)),
                  pl.BlockSpec((mods, 1, d), lambda i: (i, 0, 0)),
                  pl.BlockSpec((d, nq), lambda i: (0, 0)),
                  pl.BlockSpec((d, nq), lambda i: (0, 0)),
                  key_spec, key_spec],
        out_specs=(pl.BlockSpec((ROUTE_TILE, d), lambda i: (i, 0)),
                   score_spec, score_spec, score_spec, score_spec),
        scratch_shapes=[pltpu.VMEM((ROUTE_TILE, d), bf16), pltpu.VMEM((ROUTE_TILE, d), bf16),
                        pltpu.VMEM((2, PEER_NKEYS, ROUTE_TILE), f32),
                        pltpu.VMEM((N_CAND_ROWS, ROUTE_TILE), f32)],
        compiler_params=_params("parallel"),
    )(x, shift_t, scale_t, wq_hi, wq_lo, keys_hi, keys_lo)


def _expert_kernel(xm_ref, u_ref, vt_ref, thr_ref, e1_ref, s2_ref, e2_ref, o_ref, a_scr, w_scr):
    j = pl.program_id(1)

    @pl.when(j == 0)
    def _():
        o_ref[...] = jnp.zeros(o_ref.shape, f32)

    a_scr[...] = _dot(u_ref[...], xm_ref[...], NT)
    sqrt_half = math.sqrt(0.5)
    n_i1 = EXPERT_BLK // PEER_NKEYS
    row_chunks = PEER_NKEYS // GATE_ROWS

    def tile(t, carry):
        lanes = pl.ds(pl.multiple_of((t // row_chunks) * LANE, LANE), LANE)
        r0 = pl.multiple_of((t % row_chunks) * GATE_ROWS, GATE_ROWS)
        i2 = pl.ds(r0, GATE_ROWS)
        gate = [jnp.zeros((GATE_ROWS, LANE), f32) for _ in range(n_i1)]
        for h in range(PEER_HEADS):
            s2 = s2_ref[h, i2, lanes]
            e2 = e2_ref[h, i2, lanes]
            for il in range(n_i1):
                w = e1_ref[h, il:il + 1, lanes] * e2
                gate[il] = gate[il] + jnp.where(s2 >= thr_ref[h, il:il + 1, lanes], w, 0.0)
        for il in range(n_i1):
            rows = pl.ds(il * PEER_NKEYS + r0, GATE_ROWS)
            a = a_scr[rows, lanes]
            act = 0.5 * a * (1.0 + lax.erf(a * sqrt_half))
            w_scr[rows, lanes] = (gate[il] * act).astype(bf16)
        return carry

    lax.fori_loop(0, (EXPERT_TOK // LANE) * row_chunks, tile, 0)
    o_ref[...] += _dot(vt_ref[...], w_scr[...])


def _experts(xm, u_bf, vt_bf, thr, e1, s2, e2):
    rows, d = xm.shape
    n_exp = u_bf.shape[0]
    i1_per_blk = EXPERT_BLK // PEER_NKEYS
    sel = pl.BlockSpec((PEER_HEADS, i1_per_blk, EXPERT_TOK), lambda i, j: (0, j, i))
    full = pl.BlockSpec((PEER_HEADS, PEER_NKEYS, EXPERT_TOK), lambda i, j: (0, 0, i))
    return pl.pallas_call(
        _expert_kernel,
        out_shape=jax.ShapeDtypeStruct((d, rows), f32),
        grid=(rows // EXPERT_TOK, n_exp // EXPERT_BLK),
        in_specs=[pl.BlockSpec((EXPERT_TOK, d), lambda i, j: (i, 0)),
                  pl.BlockSpec((EXPERT_BLK, d), lambda i, j: (j, 0)),
                  pl.BlockSpec((d, EXPERT_BLK), lambda i, j: (0, j)),
                  sel, sel, full, full],
        out_specs=pl.BlockSpec((d, EXPERT_TOK), lambda i, j: (0, i)),
        scratch_shapes=[pltpu.VMEM((EXPERT_BLK, EXPERT_TOK), f32),
                        pltpu.VMEM((EXPERT_BLK, EXPERT_TOK), bf16)],
        compiler_params=_params("parallel", "arbitrary"),
    )(xm, u_bf, vt_bf, thr, e1, s2, e2)


def _peer_residual(x, shift_t, scale_t, gate_t, peer_q, peer_keys, peer_u, peer_v, ln_g, ln_b):
    wq_hi, wq_lo = _split(peer_q)
    keys_hi, keys_lo = _split(peer_keys)
    xm, thr, s2, e1, e2 = _route(x, shift_t, scale_t, wq_hi, wq_lo, keys_hi, keys_lo)
    y_t = _experts(xm, peer_u.astype(bf16), peer_v.T.astype(bf16), thr, e1, s2, e2)
    return _resid_ln(x, y_t, gate_t, ln_g, ln_b, transposed=True)


def _grid_sincos(n_tokens, d):
    t = jnp.arange(n_tokens)
    row = (t // GRID_W).astype(f32)[:, None]
    col = (t % GRID_W).astype(f32)[:, None]
    n_freq = d // 4
    freq = jnp.exp(-math.log(POS_BASE) * jnp.arange(n_freq, dtype=f32) / n_freq)[None, :]
    return jnp.concatenate([jnp.sin(row * freq), jnp.cos(row * freq), jnp.sin(col * freq), jnp.cos(col * freq)], -1)


def _tile_rows(mod, src):
    d = mod.shape[1] // 6
    rows = jnp.concatenate([jnp.broadcast_to(mod[r:r + 1], (n, mod.shape[1])) for r, n in src], axis=0)
    return [rows[:, None, i * d:(i + 1) * d] for i in range(6)]


def _pad_cols(w, n):
    return jnp.pad(w, ((0, 0), (0, n - w.shape[1])))


def kernel(x, c, ctx, c_ctx, l0_ada_w, l0_ada_b, l0_ln1_g, l0_ln1_b, l0_w_in, l0_conv_a, l0_a_log_a, l0_dt_bias_a, l0_norm_a, l0_conv_b, l0_conv_b_bias, l0_a_log_b, l0_dt_bias_b, l0_d_skip_b, l0_norm_b, l0_w_out, l0_ln2_g, l0_ln2_b, l0_peer_q, l0_peer_keys, l0_peer_u, l0_peer_v, l1_ada_w, l1_ada_b, l1_ln1_g, l1_ln1_b, l1_w_in, l1_conv_c, l1_i_bias, l1_f_bias, l1_norm_c, l1_w_out, l1_ln2_g, l1_ln2_b, l1_peer_q, l1_peer_keys, l1_peer_u, l1_peer_v):
    bsz, n_lat, d = x.shape
    n_ctx_tok = ctx.shape[1]
    rows_grid = n_lat // GRID_W
    seq = n_ctx_tok + n_lat
    n_ctx = n_ctx_tok // CHUNK
    tiles_b = seq // ROW_TILE
    ctx_tiles = n_ctx_tok // ROW_TILE
    lat_tiles = n_lat // ROW_TILE

    cond = jnp.concatenate([c, c_ctx[None, :], jnp.zeros((8 - bsz - 1, d), f32)], axis=0)
    src_all = [run for b in range(bsz) for run in ((bsz, ctx_tiles), (b, lat_tiles))]

    x_lat = x + _grid_sincos(n_lat, d).astype(x.dtype)
    xs_all = jnp.concatenate([ctx, x_lat], axis=1).reshape(bsz * seq, d)

    sh1, sc1, g1, sh2, sc2, g2 = _tile_rows(_ada(cond, l0_ada_w, l0_ada_b), src_all)
    qkv_w = A_HEADS * (2 * A_DK + A_DV)
    az = A_HEADS * A_DV
    ag = N_DIR * A_HEADS
    bd = B_HEADS * B_HEADDIM
    bx = bd + 2 * B_GROUPS * B_STATE
    bt = N_DIR * B_HEADS
    o0 = np.cumsum([0, qkv_w, az, ag, ag, bd, bx, bt])
    w_qkv, w_za, w_beta, w_dec, w_zb, w_xbc, w_dt = (l0_w_in[:, o0[i]:o0[i + 1]] for i in range(7))
    main_w = qkv_w + az + bd + bx
    w0 = _pad_cols(jnp.concatenate([w_qkv, w_za, w_zb, w_xbc, w_beta, w_dec, w_dt], axis=1), main_w + LANE)
    proj = _proj(xs_all, sh1, sc1, w0.astype(bf16)).reshape(bsz, seq, main_w + LANE)
    conv_w = jnp.concatenate([l0_conv_a, l0_conv_b], axis=1)
    conv_b = jnp.concatenate([jnp.zeros((qkv_w,), f32), l0_conv_b_bias]).reshape(1, -1)
    xbc_blk = (qkv_w + az + bd) // LANE
    col_blocks = list(range(qkv_w // LANE)) + list(range(xbc_blk, xbc_blk + bx // LANE))
    act = _conv_act(proj, conv_w, conv_b, col_blocks, seg0=n_ctx_tok,
                    n_norm=2 * A_HEADS, n_qscale=A_HEADS, qscale=A_DK ** -0.5)
    gates = proj[:, :, main_w:]
    o_a = _gdn(act, gates[..., :ag], gates[..., ag:2 * ag], l0_a_log_a, l0_dt_bias_a, n_ctx)
    o_b = _ssd(act, gates[..., 2 * ag:2 * ag + bt], l0_a_log_b, l0_dt_bias_b, n_ctx)
    x1 = _merge_even(xs_all, [o.reshape(bsz * seq, az) for o in o_a], [o.reshape(bsz * seq, bd) for o in o_b],
                     proj.reshape(bsz * seq, -1), act.reshape(bsz * seq, -1),
                     l0_norm_a, l0_norm_b, l0_d_skip_b, l0_w_out.astype(bf16), g1, l0_ln1_g, l0_ln1_b)
    x2 = _peer_residual(x1, sh2, sc2, g2, l0_peer_q, l0_peer_keys, l0_peer_u, l0_peer_v, l0_ln2_g, l0_ln2_b)

    sh1, sc1, g1, sh2, sc2, g2 = _tile_rows(_ada(cond, l1_ada_w, l1_ada_b), src_all)
    x2 = x2.reshape(bsz, seq, d)
    x2_ctx, x2_lat = x2[:, :n_ctx_tok], x2[:, n_ctx_tok:]
    lat_cm = x2_lat.reshape(bsz, rows_grid, GRID_W, d).transpose(0, 2, 1, 3).reshape(bsz, n_lat, d)
    xs1 = jnp.concatenate([x2_ctx, lat_cm], axis=1).reshape(bsz * seq, d)
    qk_w = 2 * C_HEADS * C_DK
    vw = C_HEADS * C_DV
    w1 = _pad_cols(l1_w_in, qk_w + 2 * vw + LANE)
    proj1 = _proj(xs1, sh1, sc1, w1.astype(bf16)).reshape(bsz, seq, -1)
    act1 = _conv_act(proj1, l1_conv_c, jnp.zeros((1, qk_w), f32), list(range(qk_w // LANE)), seg0=n_ctx_tok,
                     n_norm=0, n_qscale=C_HEADS, qscale=C_DK ** -0.5)
    gates1 = proj1[:, :, qk_w + 2 * vw:]
    cg = N_DIR * C_HEADS
    h1 = _mlstm(act1, proj1, gates1[..., :cg], gates1[..., cg:2 * cg], l1_i_bias, l1_f_bias, n_ctx)
    y_cm = _merge_odd([h.reshape(bsz * seq, vw) for h in h1], proj1.reshape(bsz * seq, -1), l1_norm_c,
                      l1_w_out.astype(bf16), tiles_b, ctx_tiles, bsz)
    y_lat = y_cm.reshape(bsz, GRID_W, rows_grid, d).transpose(0, 2, 1, 3).reshape(bsz * n_lat, d)
    lat_sel = lambda t: t.reshape(bsz, tiles_b, 1, d)[:, ctx_tiles:].reshape(bsz * lat_tiles, 1, d)
    x3 = _resid_ln(x2_lat.reshape(bsz * n_lat, d), y_lat, lat_sel(g1), l1_ln1_g, l1_ln1_b, transposed=False)
    x4 = _peer_residual(x3, lat_sel(sh2), lat_sel(sc2), lat_sel(g2), l1_peer_q, l1_peer_keys, l1_peer_u,
                        l1_peer_v, l1_ln2_g, l1_ln2_b)
    return x4.reshape(bsz, n_lat, d)
```

```python
import functools
import math

import numpy as np
import jax
import jax.numpy as jnp
from jax import lax
from jax.experimental import pallas as pl
from jax.experimental.pallas import tpu as pltpu

f32 = jnp.float32
bf16 = jnp.bfloat16

D_MODEL = 1024
GRID_W = 64
CHUNK = 64
CONV_W = 5
POS_BASE = 10000.0
EPS = 1e-6
N_DIR = 2
DEPTH = 2
ALPHA = (2 * DEPTH) ** 0.25

A_HEADS, A_DK, A_DV = 4, 128, 128
B_HEADS, B_HEADDIM, B_GROUPS, B_STATE = 8, 64, 2, 128
C_HEADS, C_DK, C_DV = 4, 128, 256
PEER_HEADS, PEER_NKEYS, PEER_DKEY, PEER_TOPK = 8, 128, 256, 16

LANE = 128
ROW_TILE = 256
ROUTE_TILE = 512
EXPERT_TOK = 512
EXPERT_BLK = 2048
I1_TILE = 8
GATE_ROWS = 32
VMEM_LIMIT = 56 * 1024 * 1024
N_CAND_ROWS = -(-sum((PEER_TOPK + 1) // (p + 1) for p in range(PEER_TOPK + 1)) // 8) * 8

NN = (((1,), (0,)), ((), ()))
NT = (((1,), (1,)), ((), ()))


def _params(*sem):
    return pltpu.CompilerParams(dimension_semantics=sem, vmem_limit_bytes=VMEM_LIMIT)


def _dot(a, b, dims=NN):
    return lax.dot_general(a, b, dims, preferred_element_type=f32)


def _split(a):
    hi = a.astype(bf16)
    return hi, (a - hi.astype(f32)).astype(bf16)


def _dot3(a, b, dims=NN):
    ah, al = _split(a)
    bh, bl = _split(b)
    return _dot(ah, bh, dims) + (_dot(ah, bl, dims) + _dot(al, bh, dims))


def _sigmoid(x):
    return 1.0 / (1.0 + jnp.exp(-x))


def _silu(x):
    return x * _sigmoid(x)


def _softplus(x):
    return jnp.maximum(x, 0.0) + jnp.log1p(jnp.exp(-jnp.abs(x)))


def _layer_norm(z, g, b):
    mu = jnp.mean(z, -1, keepdims=True)
    zc = z - mu
    var = jnp.mean(zc * zc, -1, keepdims=True)
    return zc * lax.rsqrt(var + EPS) * g + b


def _rms(x, g):
    return x * lax.rsqrt(jnp.mean(x * x, -1, keepdims=True) + EPS) * g


def _chunk_masks(d):
    row = lax.broadcasted_iota(jnp.int32, (CHUNK, CHUNK), 0)
    col = lax.broadcasted_iota(jnp.int32, (CHUNK, CHUNK), 1)
    diff = row - col if d == 0 else col - row
    return diff >= 0, diff > 0, diff <= 0, row == col


def _cum(mask_incl, mask_incl_t, x_col, x_row):
    c_col = jnp.sum(jnp.where(mask_incl, x_row, 0.0), axis=1, keepdims=True)
    c_row = jnp.sum(jnp.where(mask_incl_t, x_col, 0.0), axis=0, keepdims=True)
    tot = jnp.sum(x_col, axis=0, keepdims=True)
    return c_col, c_row, tot


def _ada_kernel(c_ref, w_ref, b_ref, o_ref):
    o_ref[...] = _dot3(_silu(c_ref[...]), w_ref[...]) + b_ref[...]


def _ada(cond, w, b):
    rows, d = cond.shape
    n = w.shape[1]
    return pl.pallas_call(
        _ada_kernel,
        out_shape=jax.ShapeDtypeStruct((rows, n), f32),
        grid=(n // d,),
        in_specs=[pl.BlockSpec((rows, d), lambda j: (0, 0)),
                  pl.BlockSpec((d, d), lambda j: (0, j)),
                  pl.BlockSpec((1, d), lambda j: (0, j))],
        out_specs=pl.BlockSpec((rows, d), lambda j: (0, j)),
        compiler_params=_params("parallel"),
    )(cond, w, b.reshape(1, n))


def _proj_kernel(x_ref, sh_ref, sc_ref, w_ref, o_ref):
    xm = x_ref[...] * (1.0 + sc_ref[0]) + sh_ref[0]
    o_ref[...] = _dot(xm.astype(bf16), w_ref[...])


def _proj(x, shift_t, scale_t, w):
    rows, d = x.shape
    n = w.shape[1]
    return pl.pallas_call(
        _proj_kernel,
        out_shape=jax.ShapeDtypeStruct((rows, n), f32),
        grid=(rows // ROW_TILE,),
        in_specs=[pl.BlockSpec((ROW_TILE, d), lambda i: (i, 0)),
                  pl.BlockSpec((1, 1, d), lambda i: (i, 0, 0)),
                  pl.BlockSpec((1, 1, d), lambda i: (i, 0, 0)),
                  pl.BlockSpec((d, n), lambda i: (0, 0))],
        out_specs=pl.BlockSpec((ROW_TILE, n), lambda i: (i, 0)),
        compiler_params=_params("parallel"),
    )(x, shift_t, scale_t, w)


def _conv_kernel(x_ref, w_ref, b_ref, o_ref, pad_ref, *, seq, seg0, n_norm, n_qscale, qscale):
    j = pl.program_id(1)
    halo = 8
    pad_ref[0:halo, :] = jnp.zeros((halo, LANE), f32)
    pad_ref[halo + seq:2 * halo + seq, :] = jnp.zeros((halo, LANE), f32)
    pad_ref[halo:halo + seq, :] = x_ref[0]
    w = w_ref[...]
    bias = b_ref[...]
    scale = jnp.where(j < n_qscale, qscale, 1.0).astype(f32)
    norm_on = jnp.where(j < n_norm, 1.0, 0.0).astype(f32)
    rows = ROW_TILE

    def body(c, carry):
        r0 = pl.multiple_of(c * rows, rows)
        win = pad_ref[pl.ds(r0, rows + 2 * halo), :]
        t = r0 + lax.broadcasted_iota(jnp.int32, (rows, LANE), 0)
        acc = bias + w[2:3] * win[halo:halo + rows]
        for k in (0, 1, 3, 4):
            off = k - CONV_W // 2
            tap = win[halo + off:halo + off + rows]
            same_segment = ((t + off) >= seg0) == (t >= seg0)
            acc = acc + w[k:k + 1] * jnp.where(same_segment, tap, 0.0)
        y = _silu(acc)
        yn = y * lax.rsqrt(jnp.sum(y * y, -1, keepdims=True) + EPS)
        y = (norm_on * yn + (1.0 - norm_on) * y) * scale
        o_ref[0, pl.ds(r0, rows), :] = y
        return carry

    lax.fori_loop(0, seq // rows, body, 0)


def _conv_act(x, w, b, col_blocks, *, seg0, n_norm, n_qscale, qscale):
    bsz, seq, _ = x.shape
    n_out = len(col_blocks)
    first_gap = next((i for i, cb in enumerate(col_blocks) if cb != i), n_out)
    gap = col_blocks[first_gap] - first_gap if first_gap < n_out else 0
    in_map = lambda bi, j: (bi, 0, jnp.where(j < first_gap, j, j + gap))
    kern = functools.partial(_conv_kernel, seq=seq, seg0=seg0, n_norm=n_norm, n_qscale=n_qscale, qscale=qscale)
    return pl.pallas_call(
        kern,
        out_shape=jax.ShapeDtypeStruct((bsz, seq, n_out * LANE), f32),
        grid=(bsz, n_out),
        in_specs=[pl.BlockSpec((1, seq, LANE), in_map),
                  pl.BlockSpec((CONV_W, LANE), lambda bi, j: (0, j)),
                  pl.BlockSpec((1, LANE), lambda bi, j: (0, j))],
        out_specs=pl.BlockSpec((1, seq, LANE), lambda bi, j: (bi, 0, j)),
        scratch_shapes=[pltpu.VMEM((seq + 16, LANE), f32)],
        compiler_params=_params("parallel", "parallel"),
    )(x, w, b)


def _chunk_index(d, n, n_ctx, n_all):
    return n if d == 0 else jnp.where(n < n_ctx, n_ctx - 1 - n, n_all + n_ctx - 1 - n)


def _both_directions(make_specs):
    return [spec for d in range(N_DIR) for spec in make_specs(d)]


def _gate_layouts(raw, heads):
    bsz, seq, _ = raw.shape
    r = raw.reshape(bsz, seq // CHUNK, CHUNK, N_DIR, heads)
    return r.transpose(3, 0, 1, 2, 4), r.transpose(3, 0, 1, 4, 2)


def _param_layouts(*ps):
    return jnp.stack(ps, axis=1).astype(f32), jnp.stack(ps, axis=2).astype(f32)


def _gdn_kernel(*refs, bsz):
    n_in = 7
    ins = [refs[d * n_in:(d + 1) * n_in] for d in range(N_DIR)]
    pcol_ref, prow_ref = refs[N_DIR * n_in:N_DIR * n_in + 2]
    o_refs = refs[N_DIR * n_in + 2:N_DIR * n_in + 2 + N_DIR]
    st_ref = refs[-1]
    n = pl.program_id(0)

    @pl.when(n == 0)
    def _():
        st_ref[...] = jnp.zeros(st_ref.shape, f32)

    masks = [_chunk_masks(d) for d in range(N_DIR)]
    eye = masks[0][3].astype(f32)
    row = lax.broadcasted_iota(jnp.int32, (CHUNK, CHUNK), 0)
    col = lax.broadcasted_iota(jnp.int32, (CHUNK, CHUNK), 1)
    blocks = [(row >> sh) == (col >> sh) for sh in (3, 4, 5, 6)]
    off_masks = [blocks[lvl + 1] & ~blocks[lvl] for lvl in range(3)]
    chains = [(d, b, h) for d in range(N_DIR) for b in range(bsz) for h in range(A_HEADS)]
    idx = range(len(chains))
    pc = [pcol_ref[d] for d in range(N_DIR)]
    pr = [prow_ref[d] for d in range(N_DIR)]
    beta_col = [[_sigmoid(ins[d][3][0, b, 0]) for b in range(bsz)] for d in range(N_DIR)]
    g_col = [[-jnp.exp(pc[d][0:1, :]) * _softplus(ins[d][5][0, b, 0] + pc[d][1:2, :]) for b in range(bsz)]
             for d in range(N_DIR)]
    g_row = [[-jnp.exp(pr[d][:, 0:1]) * _softplus(ins[d][6][0, b, 0] + pr[d][:, 1:2]) for b in range(bsz)]
             for d in range(N_DIR)]
    sl = [slice(h * A_DK, (h + 1) * A_DK) for _, _, h in chains]
    q = [ins[d][0][b, :, sl[c]] for c, (d, b, _) in enumerate(chains)]
    k = [ins[d][1][b, :, sl[c]] for c, (d, b, _) in enumerate(chains)]
    v = [ins[d][2][b, :, sl[c]] for c, (d, b, _) in enumerate(chains)]
    cums = [_cum(masks[d][0], masks[d][2], g_col[d][b][:, h:h + 1], g_row[d][b][h:h + 1, :]) for d, b, h in chains]
    decay = [jnp.exp(jnp.where(masks[d][0], cums[c][0] - cums[c][1], -jnp.inf)) for c, (d, _, _) in enumerate(chains)]
    b_col = [beta_col[d][b][:, h:h + 1] for d, b, h in chains]
    kb = [k[c] * b_col[c] for c in idx]
    l_mat = [_dot3(kb[c], k[c], NT) * jnp.where(masks[d][1], decay[c], 0.0) for c, (d, _, _) in enumerate(chains)]
    nil = [-jnp.where(blocks[0], l_mat[c], 0.0) for c in idx]
    t_inv = [eye + nil[c] for c in idx]
    for _ in range(2):
        nil = [_dot3(nil[c], nil[c]) for c in idx]
        t_inv = [t_inv[c] + _dot3(t_inv[c], nil[c]) for c in idx]
    for lvl in range(3):
        left = [_dot3(t_inv[c], jnp.where(off_masks[lvl], l_mat[c], 0.0)) for c in idx]
        t_inv = [t_inv[c] - _dot3(left[c], t_inv[c]) for c in idx]
    e_col = [jnp.exp(gc) for gc, _, _ in cums]
    uw = [_dot3(t_inv[c], jnp.concatenate([v[c] * b_col[c], kb[c] * e_col[c]], axis=1)) for c in idx]
    qk = [_dot(q[c], k[c], NT) * decay[c] for c in idx]
    k_dec_t = [(k[c] * jnp.exp(cums[c][2] - cums[c][0])).T for c in idx]
    s = [st_ref[c] for c in idx]
    wq = [_dot(jnp.concatenate([uw[c][:, A_DV:], q[c] * e_col[c]], axis=0), s[c]) for c in idx]
    v_new = [uw[c][:, :A_DV] - wq[c][:CHUNK] for c in idx]
    out = [wq[c][CHUNK:] + _dot(qk[c], v_new[c]) for c in idx]
    s_new = [jnp.exp(cums[c][2]) * s[c] + _dot(k_dec_t[c], v_new[c]) for c in idx]
    for c, (d, b, _) in enumerate(chains):
        o_refs[d][b, :, sl[c]] = out[c]
        st_ref[c] = s_new[c]


def _gdn(act, beta_raw, decay_raw, a_log, dt_bias, n_ctx):
    bsz, seq, _ = act.shape
    n_all = seq // CHUNK
    hw = A_HEADS * A_DK
    bcol, brow = _gate_layouts(beta_raw, A_HEADS)
    dcol, drow = _gate_layouts(decay_raw, A_HEADS)
    pcol, prow = _param_layouts(a_log, dt_bias)
    ci = functools.partial(_chunk_index, n_ctx=n_ctx, n_all=n_all)

    def specs(d):
        tok = lambda blk: pl.BlockSpec((bsz, CHUNK, hw), lambda n: (0, ci(d, n), blk))
        gcol = pl.BlockSpec((1, bsz, 1, CHUNK, A_HEADS), lambda n: (d, 0, ci(d, n), 0, 0))
        grow = pl.BlockSpec((1, bsz, 1, A_HEADS, CHUNK), lambda n: (d, 0, ci(d, n), 0, 0))
        return [tok(0), tok(1), tok(2), gcol, grow, gcol, grow]

    out = jax.ShapeDtypeStruct((bsz, seq, hw), f32)
    return pl.pallas_call(
        functools.partial(_gdn_kernel, bsz=bsz),
        out_shape=(out,) * N_DIR,
        grid=(n_all,),
        in_specs=_both_directions(specs) + [pl.BlockSpec((N_DIR, 2, A_HEADS), lambda n: (0, 0, 0)),
                                            pl.BlockSpec((N_DIR, A_HEADS, 2), lambda n: (0, 0, 0))],
        out_specs=tuple(pl.BlockSpec((bsz, CHUNK, hw), lambda n, d=d: (0, ci(d, n), 0)) for d in range(N_DIR)),
        scratch_shapes=[pltpu.VMEM((N_DIR * bsz * A_HEADS, A_DK, A_DV), f32)],
        compiler_params=_params("arbitrary"),
    )(*([act, act, act, bcol, brow, dcol, drow] * N_DIR), pcol, prow)


def _ssd_kernel(*refs, bsz):
    n_in = 5
    ins = [refs[d * n_in:(d + 1) * n_in] for d in range(N_DIR)]
    pcol_ref, prow_ref = refs[N_DIR * n_in:N_DIR * n_in + 2]
    o_refs = refs[N_DIR * n_in + 2:N_DIR * n_in + 2 + N_DIR]
    st_ref = refs[-1]
    n = pl.program_id(0)

    @pl.when(n == 0)
    def _():
        st_ref[...] = jnp.zeros(st_ref.shape, f32)

    masks = [_chunk_masks(d) for d in range(N_DIR)]
    rep = B_HEADS // B_GROUPS
    streams = [(d, b) for d in range(N_DIR) for b in range(bsz)]
    groups = [(s, g) for s in range(len(streams)) for g in range(B_GROUPS)]
    heads = [(s, h) for s in range(len(streams)) for h in range(B_HEADS)]
    gsl = [slice(g * B_STATE, (g + 1) * B_STATE) for _, g in groups]
    hsl = [slice(h * B_HEADDIM, (h + 1) * B_HEADDIM) for _, h in heads]
    pc = [pcol_ref[d] for d, _ in streams]
    pr = [prow_ref[d] for d, _ in streams]
    dt_col = [_softplus(ins[d][3][0, b, 0] + pc[s][1:2, :]) for s, (d, b) in enumerate(streams)]
    dt_row = [_softplus(ins[d][4][0, b, 0] + pr[s][:, 1:2]) for s, (d, b) in enumerate(streams)]
    da_col = [dt_col[s] * (-jnp.exp(pc[s][0:1, :])) for s in range(len(streams))]
    da_row = [dt_row[s] * (-jnp.exp(pr[s][:, 0:1])) for s in range(len(streams))]
    bm = [ins[streams[s][0]][1][streams[s][1], :, gsl[i]] for i, (s, _) in enumerate(groups)]
    cm = [ins[streams[s][0]][2][streams[s][1], :, gsl[i]] for i, (s, _) in enumerate(groups)]
    hs = [st_ref[i] for i in range(len(groups))]
    cb = [_dot(cm[i], bm[i], NT) for i in range(len(groups))]
    y_off = [_dot(cm[i], hs[i]) for i in range(len(groups))]
    bm_t = [bm[i].T for i in range(len(groups))]
    m_incl = [masks[streams[s][0]][0] for s, _ in heads]
    m_incl_t = [masks[streams[s][0]][2] for s, _ in heads]
    cums = [_cum(m_incl[j], m_incl_t[j], da_col[s][:, h:h + 1], da_row[s][h:h + 1, :]) for j, (s, h) in enumerate(heads)]
    seg = [jnp.exp(jnp.where(m_incl[j], cums[j][0] - cums[j][1], -jnp.inf)) for j in range(len(heads))]
    xdt = [ins[streams[s][0]][0][streams[s][1], :, hsl[j]] * dt_col[s][:, h:h + 1] for j, (s, h) in enumerate(heads)]
    y_diag = [_dot(cb[j // rep] * seg[j], xdt[j]) for j in range(len(heads))]
    for j, (s, h) in enumerate(heads):
        r = h % rep
        d, b = streams[s]
        o_refs[d][b, :, hsl[j]] = y_diag[j] + jnp.exp(cums[j][0]) * y_off[j // rep][:, r * B_HEADDIM:(r + 1) * B_HEADDIM]
    xdt_end = [xdt[j] * jnp.exp(cums[j][2] - cums[j][0]) for j in range(len(heads))]
    keep = [jnp.broadcast_to(jnp.exp(cums[j][2]), (1, B_HEADDIM)) for j in range(len(heads))]
    upd = [_dot(bm_t[i], jnp.concatenate(xdt_end[i * rep:(i + 1) * rep], axis=1)) for i in range(len(groups))]
    for i in range(len(groups)):
        st_ref[i] = jnp.concatenate(keep[i * rep:(i + 1) * rep], axis=1) * hs[i] + upd[i]


def _ssd(act, dt_raw, a_log, dt_bias, n_ctx):
    bsz, seq, _ = act.shape
    n_all = seq // CHUNK
    dinner = B_HEADS * B_HEADDIM
    gw = B_GROUPS * B_STATE
    tcol, trow = _gate_layouts(dt_raw, B_HEADS)
    pcol, prow = _param_layouts(a_log, dt_bias)
    ci = functools.partial(_chunk_index, n_ctx=n_ctx, n_all=n_all)
    x_off = (A_HEADS * (2 * A_DK + A_DV)) // dinner
    bm_off = (A_HEADS * (2 * A_DK + A_DV) + dinner) // gw

    def specs(d):
        return [pl.BlockSpec((bsz, CHUNK, dinner), lambda n: (0, ci(d, n), x_off)),
                pl.BlockSpec((bsz, CHUNK, gw), lambda n: (0, ci(d, n), bm_off)),
                pl.BlockSpec((bsz, CHUNK, gw), lambda n: (0, ci(d, n), bm_off + 1)),
                pl.BlockSpec((1, bsz, 1, CHUNK, B_HEADS), lambda n: (d, 0, ci(d, n), 0, 0)),
                pl.BlockSpec((1, bsz, 1, B_HEADS, CHUNK), lambda n: (d, 0, ci(d, n), 0, 0))]

    out = jax.ShapeDtypeStruct((bsz, seq, dinner), f32)
    return pl.pallas_call(
        functools.partial(_ssd_kernel, bsz=bsz),
        out_shape=(out,) * N_DIR,
        grid=(n_all,),
        in_specs=_both_directions(specs) + [pl.BlockSpec((N_DIR, 2, B_HEADS), lambda n: (0, 0, 0)),
                                            pl.BlockSpec((N_DIR, B_HEADS, 2), lambda n: (0, 0, 0))],
        out_specs=tuple(pl.BlockSpec((bsz, CHUNK, dinner), lambda n, d=d: (0, ci(d, n), 0)) for d in range(N_DIR)),
        scratch_shapes=[pltpu.VMEM((N_DIR * bsz * B_GROUPS, B_STATE, (B_HEADS // B_GROUPS) * B_HEADDIM), f32)],
        compiler_params=_params("arbitrary"),
    )(*([act, act, act, tcol, trow] * N_DIR), pcol, prow)


def _mlstm_kernel(*refs, bsz):
    n_in = 7
    ins = [refs[d * n_in:(d + 1) * n_in] for d in range(N_DIR)]
    pcol_ref, prow_ref = refs[N_DIR * n_in:N_DIR * n_in + 2]
    o_refs = refs[N_DIR * n_in + 2:N_DIR * n_in + 2 + N_DIR]
    c_ref, n_ref, m_ref = refs[-3:]
    n = pl.program_id(0)

    @pl.when(n == 0)
    def _():
        c_ref[...] = jnp.zeros(c_ref.shape, f32)
        n_ref[...] = jnp.zeros(n_ref.shape, f32)
        m_ref[...] = jnp.zeros(m_ref.shape, f32)

    masks = [_chunk_masks(d) for d in range(N_DIR)]
    streams = [(d, b) for d in range(N_DIR) for b in range(bsz)]
    chains = [(s, h) for s in range(len(streams)) for h in range(C_HEADS)]
    idx = range(len(chains))
    pc = [pcol_ref[d] for d, _ in streams]
    pr = [prow_ref[d] for d, _ in streams]
    li_col = [ins[d][3][0, b, 0] + pc[s][0:1, :] for s, (d, b) in enumerate(streams)]
    li_row = [ins[d][4][0, b, 0] + pr[s][:, 0:1] for s, (d, b) in enumerate(streams)]
    lf_col = [-_softplus(-(ins[d][5][0, b, 0] + pc[s][1:2, :])) for s, (d, b) in enumerate(streams)]
    lf_row = [-_softplus(-(ins[d][6][0, b, 0] + pr[s][:, 1:2])) for s, (d, b) in enumerate(streams)]
    q = [ins[streams[s][0]][0][streams[s][1], :, h * C_DK:(h + 1) * C_DK] for s, h in chains]
    k = [ins[streams[s][0]][1][streams[s][1], :, h * C_DK:(h + 1) * C_DK] for s, h in chains]
    v = [ins[streams[s][0]][2][streams[s][1], :, h * C_DV:(h + 1) * C_DV] for s, h in chains]
    m_incl = [masks[streams[s][0]][0] for s, _ in chains]
    m_incl_t = [masks[streams[s][0]][2] for s, _ in chains]
    cums = [_cum(m_incl[c], m_incl_t[c], lf_col[s][:, h:h + 1], lf_row[s][h:h + 1, :]) for c, (s, h) in enumerate(chains)]
    logw_intra = [jnp.where(m_incl[c], cums[c][0] - cums[c][1] + li_row[s][h:h + 1, :], -jnp.inf)
                  for c, (s, h) in enumerate(chains)]
    m_prev = [m_ref[c] for c in idx]
    logw_inter = [cums[c][0] + m_prev[c] for c in idx]
    m_t = [jnp.maximum(logw_inter[c], jnp.max(logw_intra[c], axis=1, keepdims=True)) for c in idx]
    qk = [_dot(q[c], k[c], NT) for c in idx]
    c_mem = [c_ref[c] for c in idx]
    n_mem = [n_ref[c] for c in idx]
    qc = [_dot(q[c], c_mem[c]) for c in idx]
    s = [qk[c] * jnp.exp(logw_intra[c] - m_t[c]) for c in idx]
    w_inter = [jnp.exp(logw_inter[c] - m_t[c]) for c in idx]
    num = [_dot(s[c], v[c]) + w_inter[c] * qc[c] for c in idx]
    den = [jnp.sum(s[c], axis=1, keepdims=True) + w_inter[c] * jnp.sum(q[c] * n_mem[c], axis=1, keepdims=True)
           for c in idx]
    for c, (s, h) in enumerate(chains):
        d, b = streams[s]
        o_refs[d][b, :, h * C_DV:(h + 1) * C_DV] = num[c] / jnp.maximum(jnp.abs(den[c]), jnp.exp(-m_t[c]))
    logw_end = [cums[c][2] - cums[c][0] + li_col[s][:, h:h + 1] for c, (s, h) in enumerate(chains)]
    m_new = [jnp.maximum(cums[c][2] + m_prev[c], jnp.max(logw_end[c], axis=0, keepdims=True)) for c in idx]
    kw = [k[c] * jnp.exp(logw_end[c] - m_new[c]) for c in idx]
    keep = [jnp.exp(cums[c][2] + m_prev[c] - m_new[c]) for c in idx]
    kv = [_dot(kw[c].T, v[c]) for c in idx]
    for c in idx:
        c_ref[c] = keep[c] * c_mem[c] + kv[c]
        n_ref[c] = keep[c] * n_mem[c] + jnp.sum(kw[c], axis=0, keepdims=True)
        m_ref[c] = m_new[c]


def _mlstm(act, proj, i_raw, f_raw, i_bias, f_bias, n_ctx):
    bsz, seq, _ = act.shape
    n_all = seq // CHUNK
    qw = C_HEADS * C_DK
    vw = C_HEADS * C_DV
    icol, irow = _gate_layouts(i_raw, C_HEADS)
    fcol, frow = _gate_layouts(f_raw, C_HEADS)
    pcol, prow = _param_layouts(i_bias, f_bias)
    ci = functools.partial(_chunk_index, n_ctx=n_ctx, n_all=n_all)

    def specs(d):
        gcol = pl.BlockSpec((1, bsz, 1, CHUNK, C_HEADS), lambda n: (d, 0, ci(d, n), 0, 0))
        grow = pl.BlockSpec((1, bsz, 1, C_HEADS, CHUNK), lambda n: (d, 0, ci(d, n), 0, 0))
        return [pl.BlockSpec((bsz, CHUNK, qw), lambda n: (0, ci(d, n), 0)),
                pl.BlockSpec((bsz, CHUNK, qw), lambda n: (0, ci(d, n), 1)),
                pl.BlockSpec((bsz, CHUNK, vw), lambda n: (0, ci(d, n), 2 * qw // vw)),
                gcol, grow, gcol, grow]

    out = jax.ShapeDtypeStruct((bsz, seq, vw), f32)
    n_chain = N_DIR * bsz * C_HEADS
    return pl.pallas_call(
        functools.partial(_mlstm_kernel, bsz=bsz),
        out_shape=(out,) * N_DIR,
        grid=(n_all,),
        in_specs=_both_directions(specs) + [pl.BlockSpec((N_DIR, 2, C_HEADS), lambda n: (0, 0, 0)),
                                            pl.BlockSpec((N_DIR, C_HEADS, 2), lambda n: (0, 0, 0))],
        out_specs=tuple(pl.BlockSpec((bsz, CHUNK, vw), lambda n, d=d: (0, ci(d, n), 0)) for d in range(N_DIR)),
        scratch_shapes=[pltpu.VMEM((n_chain, C_DK, C_DV), f32),
                        pltpu.VMEM((n_chain, 1, C_DK), f32),
                        pltpu.VMEM((n_chain, 1, 1), f32)],
        compiler_params=_params("arbitrary"),
    )(*([act, act, proj, icol, irow, fcol, frow] * N_DIR), pcol, prow)


def _merge_even_kernel(x_ref, af_ref, ar_ref, bf_ref, br_ref, za_ref, zb_ref, xs_ref, na_ref, nb_ref, dsk_ref,
                       w_ref, g_ref, lg_ref, lb_ref, o_ref):
    a = af_ref[...] + ar_ref[...]
    za = za_ref[...]
    na = na_ref[...]
    parts = []
    for h in range(A_HEADS):
        sl = slice(h * A_DV, (h + 1) * A_DV)
        parts.append(_rms(a[:, sl], na) * _silu(za[:, sl]))
    yb = (bf_ref[...] + br_ref[...] + dsk_ref[...] * xs_ref[...]) * _silu(zb_ref[...])
    gw = (B_HEADS * B_HEADDIM) // B_GROUPS
    nb = nb_ref[...]
    for g in range(B_GROUPS):
        sl = slice(g * gw, (g + 1) * gw)
        parts.append(_rms(yb[:, sl], nb[:, sl]))
    y = jnp.concatenate(parts, axis=1).astype(bf16)
    z = ALPHA * x_ref[...] + g_ref[0] * _dot(y, w_ref[...])
    o_ref[...] = _layer_norm(z, lg_ref[...], lb_ref[...])


def _merge_even(x, o_a, o_b, proj, act, norm_a, norm_b, d_skip, w_out, gate_t, ln_g, ln_b):
    rows, d = x.shape
    aw = A_HEADS * A_DV
    bw = B_HEADS * B_HEADDIM
    row = lambda w_, blk: pl.BlockSpec((ROW_TILE, w_), lambda i: (i, blk))
    vec = lambda w_: pl.BlockSpec((1, w_), lambda i: (0, 0))
    qkv = A_HEADS * (2 * A_DK + A_DV)
    return pl.pallas_call(
        _merge_even_kernel,
        out_shape=jax.ShapeDtypeStruct((rows, d), f32),
        grid=(rows // ROW_TILE,),
        in_specs=[row(d, 0), row(aw, 0), row(aw, 0), row(bw, 0), row(bw, 0),
                  row(aw, qkv // aw), row(bw, (qkv + aw) // bw), row(bw, qkv // bw),
                  vec(A_DV), vec(bw), vec(bw),
                  pl.BlockSpec((aw + bw, d), lambda i: (0, 0)),
                  pl.BlockSpec((1, 1, d), lambda i: (i, 0, 0)),
                  vec(d), vec(d)],
        out_specs=row(d, 0),
        compiler_params=_params("parallel"),
    )(x, *o_a, *o_b, proj, proj, act, norm_a.reshape(1, -1), norm_b.reshape(1, -1),
      jnp.repeat(d_skip, B_HEADDIM).reshape(1, -1), w_out, gate_t, ln_g.reshape(1, -1), ln_b.reshape(1, -1))


def _merge_odd_kernel(hf_ref, hr_ref, o_ref_in, nc_ref, w_ref, out_ref):
    hsum = hf_ref[...] + hr_ref[...]
    o = o_ref_in[...]
    nc = nc_ref[...]
    parts = []
    for h in range(C_HEADS):
        sl = slice(h * C_DV, (h + 1) * C_DV)
        parts.append(_rms(hsum[:, sl], nc) * _sigmoid(o[:, sl]))
    y = jnp.concatenate(parts, axis=1).astype(bf16)
    out_ref[...] = _dot(y, w_ref[...])


def _merge_odd(h, proj, norm_c, w_out, tiles_per_batch, ctx_tiles, bsz):
    vw = C_HEADS * C_DV
    lat_tiles = tiles_per_batch - ctx_tiles
    src = lambda i: i + ctx_tiles * (i // lat_tiles + 1)
    d = w_out.shape[1]
    return pl.pallas_call(
        _merge_odd_kernel,
        out_shape=jax.ShapeDtypeStruct((bsz * lat_tiles * ROW_TILE, d), f32),
        grid=(bsz * lat_tiles,),
        in_specs=[pl.BlockSpec((ROW_TILE, vw), lambda i: (src(i), 0)),
                  pl.BlockSpec((ROW_TILE, vw), lambda i: (src(i), 0)),
                  pl.BlockSpec((ROW_TILE, vw), lambda i: (src(i), 2 * C_HEADS * C_DK // vw + 1)),
                  pl.BlockSpec((1, C_DV), lambda i: (0, 0)),
                  pl.BlockSpec((vw, d), lambda i: (0, 0))],
        out_specs=pl.BlockSpec((ROW_TILE, d), lambda i: (i, 0)),
        compiler_params=_params("parallel"),
    )(*h, proj, norm_c.reshape(1, -1), w_out)


def _resid_ln_kernel(x_ref, y_ref, g_ref, lg_ref, lb_ref, o_ref, *, transposed):
    y = y_ref[...]
    if transposed:
        y = y.T
    o_ref[...] = _layer_norm(ALPHA * x_ref[...] + g_ref[0] * y, lg_ref[...], lb_ref[...])


def _resid_ln(x, y, gate_t, ln_g, ln_b, *, transposed):
    rows, d = x.shape
    y_spec = (pl.BlockSpec((d, ROW_TILE), lambda i: (0, i)) if transposed
              else pl.BlockSpec((ROW_TILE, d), lambda i: (i, 0)))
    return pl.pallas_call(
        functools.partial(_resid_ln_kernel, transposed=transposed),
        out_shape=jax.ShapeDtypeStruct((rows, d), f32),
        grid=(rows // ROW_TILE,),
        in_specs=[pl.BlockSpec((ROW_TILE, d), lambda i: (i, 0)), y_spec,
                  pl.BlockSpec((1, 1, d), lambda i: (i, 0, 0)),
                  pl.BlockSpec((1, d), lambda i: (0, 0)),
                  pl.BlockSpec((1, d), lambda i: (0, 0))],
        out_specs=pl.BlockSpec((ROW_TILE, d), lambda i: (i, 0)),
        compiler_params=_params("parallel"),
    )(x, y, gate_t, ln_g.reshape(1, -1), ln_b.reshape(1, -1))


def _oddeven_merge(lo, hi, r):
    step = r * 2
    if step < hi - lo:
        yield from _oddeven_merge(lo, hi, step)
        yield from _oddeven_merge(lo + r, hi, step)
        yield from [(i, i + r) for i in range(lo + r, hi - r, step)]
    else:
        yield (lo, lo + r)


def _oddeven_sort(lo, hi):
    if hi - lo >= 1:
        mid = lo + (hi - lo) // 2
        yield from _oddeven_sort(lo, mid)
        yield from _oddeven_sort(mid + 1, hi)
        yield from _oddeven_merge(lo, hi, 1)


def _exchange(a, i, j):
    a[i], a[j] = jnp.maximum(a[i], a[j]), jnp.minimum(a[i], a[j])


def _top_sorted(slabs):
    a = list(slabs)
    for i, j in _oddeven_sort(0, len(a) - 1):
        _exchange(a, i, j)
    for shift in (4, 2, 1):
        other = [pltpu.roll(x, shift, axis=0) for x in a]
        if len(a) < PEER_TOPK:
            a = a + other[::-1]
        else:
            a = [jnp.maximum(a[k], other[PEER_TOPK - 1 - k]) for k in range(PEER_TOPK)]
        dist = PEER_TOPK // 2
        while dist >= 1:
            for i in range(PEER_TOPK):
                if i & dist == 0:
                    _exchange(a, i, i + dist)
            dist //= 2
    return a


def _next_largest(slabs, kth):
    count = sum(jnp.where(s >= kth, 1.0, 0.0) for s in slabs)
    below = functools.reduce(jnp.maximum, [jnp.where(s < kth, s, -jnp.inf) for s in slabs])
    count = jnp.sum(count, axis=0, keepdims=True)
    below = jnp.max(below, axis=0, keepdims=True)
    return jnp.where(count > PEER_TOPK, kth, below)


def _top_values(s):
    slabs = [s[i:i + 8] for i in range(0, s.shape[0], 8)]
    if len(slabs) < 8:
        slabs = slabs + [jnp.full_like(slabs[0], -jnp.inf)] * (8 - len(slabs))
    top = [t[0:1] for t in _top_sorted(slabs)]
    return top + [_next_largest(slabs, top[-1])]


def _route_kernel(x_ref, sh_ref, sc_ref, wh_ref, wl_ref, kh_ref, kl_ref,
                  xm_ref, thr_ref, s2_ref, e1_ref, e2_ref, xh_ref, xl_ref, sc_scr, cand_ref):
    for t in range(ROUTE_TILE // ROW_TILE):
        rows = slice(t * ROW_TILE, (t + 1) * ROW_TILE)
        xm = x_ref[rows, :] * (1.0 + sc_ref[t]) + sh_ref[t]
        xm_ref[rows, :] = xm.astype(bf16)
        xh_ref[rows, :], xl_ref[rows, :] = _split(xm)
    half = PEER_DKEY // 2
    n_top = PEER_TOPK + 1
    pairs = [(p, r) for p in range(n_top) for r in range(n_top) if (p + 1) * (r + 1) <= n_top]
    cand_ref[...] = jnp.full(cand_ref.shape, -jnp.inf, f32)

    def scores(h, sc_scr):
        wsl = pl.ds(pl.multiple_of(h * PEER_DKEY, PEER_DKEY), PEER_DKEY)
        xh = xh_ref[...]
        q = _dot(xh, wh_ref[:, wsl]) + (_dot(xh, wl_ref[:, wsl]) + _dot(xl_ref[...], wh_ref[:, wsl]))
        for c in range(2):
            qh, ql = _split(q[:, c * half:(c + 1) * half])
            kh = kh_ref[h, c]
            sc_scr[c] = _dot(kh, qh, NT) + (_dot(kl_ref[h, c], qh, NT) + _dot(kh, ql, NT))

    def select(h, sc_scr):
        for lb in range(ROUTE_TILE // LANE):
            lanes = slice(lb * LANE, (lb + 1) * LANE)
            s1 = sc_scr[0, :, lanes]
            s2 = sc_scr[1, :, lanes]
            t1 = _top_values(s1)
            t2 = _top_values(s2)
            for i, (p, r) in enumerate(pairs):
                cand_ref[i:i + 1, lanes] = t1[p] + t2[r]
            best = _top_values(cand_ref[:, lanes])
            z = sum(jnp.exp(bv - best[0]) for bv in best[:PEER_TOPK])
            tau = 0.5 * (best[PEER_TOPK - 1] + best[PEER_TOPK])
            thr_ref[h, :, lanes] = tau - s1
            s2_ref[h, :, lanes] = s2
            e1_ref[h, :, lanes] = jnp.exp(s1 - t1[0]) / z
            e2_ref[h, :, lanes] = jnp.exp(s2 - t2[0])

    def head(h, carry):
        scores(h, sc_scr)
        select(h, sc_scr)
        return carry

    lax.fori_loop(0, PEER_HEADS, head, 0)


def _route(x, shift_t, scale_t, wq_hi, wq_lo, keys_hi, keys_lo):
    rows, d = x.shape
    nq = wq_hi.shape[1]
    half = PEER_DKEY // 2
    score = jax.ShapeDtypeStruct((PEER_HEADS, PEER_NKEYS, rows), f32)
    score_spec = pl.BlockSpec((PEER_HEADS, PEER_NKEYS, ROUTE_TILE), lambda i: (0, 0, i))
    key_spec = pl.BlockSpec((PEER_HEADS, 2, PEER_NKEYS, half), lambda i: (0, 0, 0, 0))
    mods = ROUTE_TILE // ROW_TILE
    return pl.pallas_call(
        _route_kernel,
        out_shape=(jax.ShapeDtypeStruct((rows, d), bf16), score, score, score, score),
        grid=(rows // ROUTE_TILE,),
        in_specs=[pl.BlockSpec((ROUTE_TILE, d), lambda i: (i, 0)),
                  pl.BlockSpec((mods, 1, d), lambda i: (i, 0, 0)),
                  pl.BlockSpec((mods, 1, d), lambda i: (i, 0, 0)),
                  pl.BlockSpec((d, nq), lambda i: (0, 0)),
                  pl.BlockSpec((d, nq), lambda i: (0, 0)),
                  key_spec, key_spec],
        out_specs=(pl.BlockSpec((ROUTE_TILE, d), lambda i: (i, 0)),
                   score_spec, score_spec, score_spec, score_spec),
        scratch_shapes=[pltpu.VMEM((ROUTE_TILE, d), bf16), pltpu.VMEM((ROUTE_TILE, d), bf16),
                        pltpu.VMEM((2, PEER_NKEYS, ROUTE_TILE), f32),
                        pltpu.VMEM((N_CAND_ROWS, ROUTE_TILE), f32)],
        compiler_params=_params("parallel"),
    )(x, shift_t, scale_t, wq_hi, wq_lo, keys_hi, keys_lo)


def _expert_kernel(xm_ref, u_ref, vt_ref, thr_ref, e1_ref, s2_ref, e2_ref, o_ref, a_scr, w_scr):
    j = pl.program_id(1)

    @pl.when(j == 0)
    def _():
        o_ref[...] = jnp.zeros(o_ref.shape, f32)

    a_scr[...] = _dot(u_ref[...], xm_ref[...], NT)
    sqrt_half = math.sqrt(0.5)
    row_chunks = PEER_NKEYS // GATE_ROWS
    tiles_per_group = (EXPERT_TOK // LANE) * row_chunks

    def tile(r, carry, i1_0):
        lanes = pl.ds(pl.multiple_of((r // row_chunks) * LANE, LANE), LANE)
        r0 = pl.multiple_of((r % row_chunks) * GATE_ROWS, GATE_ROWS)
        i2 = pl.ds(r0, GATE_ROWS)
        gate = [jnp.zeros((GATE_ROWS, LANE), f32) for _ in range(I1_TILE)]
        for h in range(PEER_HEADS):
            s2 = s2_ref[h, i2, lanes]
            e2 = e2_ref[h, i2, lanes]
            for il in range(I1_TILE):
                i1 = i1_0 + il
                w = e1_ref[h, i1:i1 + 1, lanes] * e2
                gate[il] = gate[il] + jnp.where(s2 >= thr_ref[h, i1:i1 + 1, lanes], w, 0.0)
        for il in range(I1_TILE):
            rows = pl.ds((i1_0 + il) * PEER_NKEYS + r0, GATE_ROWS)
            a = a_scr[rows, lanes]
            act = 0.5 * a * (1.0 + lax.erf(a * sqrt_half))
            w_scr[rows, lanes] = (gate[il] * act).astype(bf16)
        return carry

    for i1_0 in range(0, EXPERT_BLK // PEER_NKEYS, I1_TILE):
        lax.fori_loop(0, tiles_per_group, functools.partial(tile, i1_0=i1_0), 0)
    o_ref[...] += _dot(vt_ref[...], w_scr[...])


def _experts(xm, u_bf, vt_bf, thr, e1, s2, e2):
    rows, d = xm.shape
    n_exp = u_bf.shape[0]
    i1_per_blk = EXPERT_BLK // PEER_NKEYS
    sel = pl.BlockSpec((PEER_HEADS, i1_per_blk, EXPERT_TOK), lambda i, j: (0, j, i))
    full = pl.BlockSpec((PEER_HEADS, PEER_NKEYS, EXPERT_TOK), lambda i, j: (0, 0, i))
    return pl.pallas_call(
        _expert_kernel,
        out_shape=jax.ShapeDtypeStruct((d, rows), f32),
        grid=(rows // EXPERT_TOK, n_exp // EXPERT_BLK),
        in_specs=[pl.BlockSpec((EXPERT_TOK, d), lambda i, j: (i, 0)),
                  pl.BlockSpec((EXPERT_BLK, d), lambda i, j: (j, 0)),
                  pl.BlockSpec((d, EXPERT_BLK), lambda i, j: (0, j)),
                  sel, sel, full, full],
        out_specs=pl.BlockSpec((d, EXPERT_TOK), lambda i, j: (0, i)),
        scratch_shapes=[pltpu.VMEM((EXPERT_BLK, EXPERT_TOK), f32),
                        pltpu.VMEM((EXPERT_BLK, EXPERT_TOK), bf16)],
        compiler_params=_params("parallel", "arbitrary"),
    )(xm, u_bf, vt_bf, thr, e1, s2, e2)


def _peer_residual(x, shift_t, scale_t, gate_t, peer_q, peer_keys, peer_u, peer_v, ln_g, ln_b):
    wq_hi, wq_lo = _split(peer_q)
    keys_hi, keys_lo = _split(peer_keys)
    xm, thr, s2, e1, e2 = _route(x, shift_t, scale_t, wq_hi, wq_lo, keys_hi, keys_lo)
    y_t = _experts(xm, peer_u.astype(bf16), peer_v.T.astype(bf16), thr, e1, s2, e2)
    return _resid_ln(x, y_t, gate_t, ln_g, ln_b, transposed=True)


def _grid_sincos(n_tokens, d):
    n_rows = n_tokens // GRID_W
    n_freq = d // 4
    freq = jnp.exp(-math.log(POS_BASE) * jnp.arange(n_freq, dtype=f32) / n_freq)[None, :]
    row = jnp.arange(n_rows, dtype=f32)[:, None] * freq
    col = jnp.arange(GRID_W, dtype=f32)[:, None] * freq
    row_part = jnp.concatenate([jnp.sin(row), jnp.cos(row)], -1)[:, None, :]
    col_part = jnp.concatenate([jnp.sin(col), jnp.cos(col)], -1)[None, :, :]
    shape = (n_rows, GRID_W, 2 * n_freq)
    table = jnp.concatenate([jnp.broadcast_to(row_part, shape), jnp.broadcast_to(col_part, shape)], -1)
    return table.reshape(n_tokens, d)


def _tile_rows(mod, src):
    d = mod.shape[1] // 6
    rows = jnp.concatenate([jnp.broadcast_to(mod[r:r + 1], (n, mod.shape[1])) for r, n in src], axis=0)
    return [rows[:, None, i * d:(i + 1) * d] for i in range(6)]


def _pad_cols(w, n):
    return jnp.pad(w, ((0, 0), (0, n - w.shape[1])))


def kernel(x, c, ctx, c_ctx, l0_ada_w, l0_ada_b, l0_ln1_g, l0_ln1_b, l0_w_in, l0_conv_a, l0_a_log_a, l0_dt_bias_a, l0_norm_a, l0_conv_b, l0_conv_b_bias, l0_a_log_b, l0_dt_bias_b, l0_d_skip_b, l0_norm_b, l0_w_out, l0_ln2_g, l0_ln2_b, l0_peer_q, l0_peer_keys, l0_peer_u, l0_peer_v, l1_ada_w, l1_ada_b, l1_ln1_g, l1_ln1_b, l1_w_in, l1_conv_c, l1_i_bias, l1_f_bias, l1_norm_c, l1_w_out, l1_ln2_g, l1_ln2_b, l1_peer_q, l1_peer_keys, l1_peer_u, l1_peer_v):
    bsz, n_lat, d = x.shape
    n_ctx_tok = ctx.shape[1]
    rows_grid = n_lat // GRID_W
    seq = n_ctx_tok + n_lat
    n_ctx = n_ctx_tok // CHUNK
    tiles_b = seq // ROW_TILE
    ctx_tiles = n_ctx_tok // ROW_TILE
    lat_tiles = n_lat // ROW_TILE

    cond = jnp.concatenate([c, c_ctx[None, :], jnp.zeros((8 - bsz - 1, d), f32)], axis=0)
    src_all = [run for b in range(bsz) for run in ((bsz, ctx_tiles), (b, lat_tiles))]

    x_lat = x + _grid_sincos(n_lat, d).astype(x.dtype)
    xs_all = jnp.concatenate([ctx, x_lat], axis=1).reshape(bsz * seq, d)

    sh1, sc1, g1, sh2, sc2, g2 = _tile_rows(_ada(cond, l0_ada_w, l0_ada_b), src_all)
    qkv_w = A_HEADS * (2 * A_DK + A_DV)
    az = A_HEADS * A_DV
    ag = N_DIR * A_HEADS
    bd = B_HEADS * B_HEADDIM
    bx = bd + 2 * B_GROUPS * B_STATE
    bt = N_DIR * B_HEADS
    o0 = np.cumsum([0, qkv_w, az, ag, ag, bd, bx, bt])
    w_qkv, w_za, w_beta, w_dec, w_zb, w_xbc, w_dt = (l0_w_in[:, o0[i]:o0[i + 1]] for i in range(7))
    main_w = qkv_w + az + bd + bx
    w0 = _pad_cols(jnp.concatenate([w_qkv, w_za, w_zb, w_xbc, w_beta, w_dec, w_dt], axis=1), main_w + LANE)
    proj = _proj(xs_all, sh1, sc1, w0.astype(bf16)).reshape(bsz, seq, main_w + LANE)
    conv_w = jnp.concatenate([l0_conv_a, l0_conv_b], axis=1)
    conv_b = jnp.concatenate([jnp.zeros((qkv_w,), f32), l0_conv_b_bias]).reshape(1, -1)
    xbc_blk = (qkv_w + az + bd) // LANE
    col_blocks = list(range(qkv_w // LANE)) + list(range(xbc_blk, xbc_blk + bx // LANE))
    act = _conv_act(proj, conv_w, conv_b, col_blocks, seg0=n_ctx_tok,
                    n_norm=2 * A_HEADS, n_qscale=A_HEADS, qscale=A_DK ** -0.5)
    gates = proj[:, :, main_w:]
    o_a = _gdn(act, gates[..., :ag], gates[..., ag:2 * ag], l0_a_log_a, l0_dt_bias_a, n_ctx)
    o_b = _ssd(act, gates[..., 2 * ag:2 * ag + bt], l0_a_log_b, l0_dt_bias_b, n_ctx)
    x1 = _merge_even(xs_all, [o.reshape(bsz * seq, az) for o in o_a], [o.reshape(bsz * seq, bd) for o in o_b],
                     proj.reshape(bsz * seq, -1), act.reshape(bsz * seq, -1),
                     l0_norm_a, l0_norm_b, l0_d_skip_b, l0_w_out.astype(bf16), g1, l0_ln1_g, l0_ln1_b)
    x2 = _peer_residual(x1, sh2, sc2, g2, l0_peer_q, l0_peer_keys, l0_peer_u, l0_peer_v, l0_ln2_g, l0_ln2_b)

    sh1, sc1, g1, sh2, sc2, g2 = _tile_rows(_ada(cond, l1_ada_w, l1_ada_b), src_all)
    x2 = x2.reshape(bsz, seq, d)
    x2_ctx, x2_lat = x2[:, :n_ctx_tok], x2[:, n_ctx_tok:]
    lat_cm = x2_lat.reshape(bsz, rows_grid, GRID_W, d).transpose(0, 2, 1, 3).reshape(bsz, n_lat, d)
    xs1 = jnp.concatenate([x2_ctx, lat_cm], axis=1).reshape(bsz * seq, d)
    qk_w = 2 * C_HEADS * C_DK
    vw = C_HEADS * C_DV
    w1 = _pad_cols(l1_w_in, qk_w + 2 * vw + LANE)
    proj1 = _proj(xs1, sh1, sc1, w1.astype(bf16)).reshape(bsz, seq, -1)
    act1 = _conv_act(proj1, l1_conv_c, jnp.zeros((1, qk_w), f32), list(range(qk_w // LANE)), seg0=n_ctx_tok,
                     n_norm=0, n_qscale=C_HEADS, qscale=C_DK ** -0.5)
    gates1 = proj1[:, :, qk_w + 2 * vw:]
    cg = N_DIR * C_HEADS
    h1 = _mlstm(act1, proj1, gates1[..., :cg], gates1[..., cg:2 * cg], l1_i_bias, l1_f_bias, n_ctx)
    y_cm = _merge_odd([h.reshape(bsz * seq, vw) for h in h1], proj1.reshape(bsz * seq, -1), l1_norm_c,
                      l1_w_out.astype(bf16), tiles_b, ctx_tiles, bsz)
    y_lat = y_cm.reshape(bsz, GRID_W, rows_grid, d).transpose(0, 2, 1, 3).reshape(bsz * n_lat, d)
    lat_sel = lambda t: t.reshape(bsz, tiles_b, 1, d)[:, ctx_tiles:].reshape(bsz * lat_tiles, 1, d)
    x3 = _resid_ln(x2_lat.reshape(bsz * n_lat, d), y_lat, lat_sel(g1), l1_ln1_g, l1_ln1_b, transposed=False)
    x4 = _peer_residual(x3, lat_sel(sh2), lat_sel(sc2), lat_sel(g2), l1_peer_q, l1_peer_keys, l1_peer_u,
                        l1_peer_v, l1_ln2_g, l1_ln2_b)
    return x4.reshape(bsz, n_lat, d)
```

```python
import functools
import math

import numpy as np
import jax
import jax.numpy as jnp
from jax import lax
from jax.experimental import pallas as pl
from jax.experimental.pallas import tpu as pltpu

f32 = jnp.float32
bf16 = jnp.bfloat16

D_MODEL = 1024
GRID_W = 64
CHUNK = 64
CONV_W = 5
POS_BASE = 10000.0
EPS = 1e-6
N_DIR = 2
DEPTH = 2
ALPHA = (2 * DEPTH) ** 0.25

A_HEADS, A_DK, A_DV = 4, 128, 128
B_HEADS, B_HEADDIM, B_GROUPS, B_STATE = 8, 64, 2, 128
C_HEADS, C_DK, C_DV = 4, 128, 256
PEER_HEADS, PEER_NKEYS, PEER_DKEY, PEER_TOPK = 8, 128, 256, 16

LANE = 128
ROW_TILE = 256
ROUTE_TILE = 512
EXPERT_TOK = 512
EXPERT_BLK = 2048
I1_TILE = 8
GATE_ROWS = 32
VMEM_LIMIT = 56 * 1024 * 1024
N_CAND_ROWS = -(-sum((PEER_TOPK + 1) // (p + 1) for p in range(PEER_TOPK + 1)) // 8) * 8

NN = (((1,), (0,)), ((), ()))
NT = (((1,), (1,)), ((), ()))


def _params(*sem):
    return pltpu.CompilerParams(dimension_semantics=sem, vmem_limit_bytes=VMEM_LIMIT)


def _dot(a, b, dims=NN):
    return lax.dot_general(a, b, dims, preferred_element_type=f32)


def _split(a):
    hi = a.astype(bf16)
    return hi, (a - hi.astype(f32)).astype(bf16)


def _dot3(a, b, dims=NN):
    ah, al = _split(a)
    bh, bl = _split(b)
    return _dot(ah, bh, dims) + (_dot(ah, bl, dims) + _dot(al, bh, dims))


def _sigmoid(x):
    return 1.0 / (1.0 + jnp.exp(-x))


def _silu(x):
    return x * _sigmoid(x)


def _softplus(x):
    return jnp.maximum(x, 0.0) + jnp.log1p(jnp.exp(-jnp.abs(x)))


def _layer_norm(z, g, b):
    mu = jnp.mean(z, -1, keepdims=True)
    zc = z - mu
    var = jnp.mean(zc * zc, -1, keepdims=True)
    return zc * lax.rsqrt(var + EPS) * g + b


def _rms(x, g):
    return x * lax.rsqrt(jnp.mean(x * x, -1, keepdims=True) + EPS) * g


def _chunk_masks(d):
    row = lax.broadcasted_iota(jnp.int32, (CHUNK, CHUNK), 0)
    col = lax.broadcasted_iota(jnp.int32, (CHUNK, CHUNK), 1)
    diff = row - col if d == 0 else col - row
    return diff >= 0, diff > 0, diff <= 0, row == col


def _cum(mask_incl, mask_incl_t, x_col, x_row):
    c_col = jnp.sum(jnp.where(mask_incl, x_row, 0.0), axis=1, keepdims=True)
    c_row = jnp.sum(jnp.where(mask_incl_t, x_col, 0.0), axis=0, keepdims=True)
    tot = jnp.sum(x_col, axis=0, keepdims=True)
    return c_col, c_row, tot


def _ada_kernel(c_ref, w_ref, b_ref, o_ref):
    o_ref[...] = _dot3(_silu(c_ref[...]), w_ref[...]) + b_ref[...]


def _ada(cond, w, b):
    rows, d = cond.shape
    n = w.shape[1]
    return pl.pallas_call(
        _ada_kernel,
        out_shape=jax.ShapeDtypeStruct((rows, n), f32),
        grid=(n // d,),
        in_specs=[pl.BlockSpec((rows, d), lambda j: (0, 0)),
                  pl.BlockSpec((d, d), lambda j: (0, j)),
                  pl.BlockSpec((1, d), lambda j: (0, j))],
        out_specs=pl.BlockSpec((rows, d), lambda j: (0, j)),
        compiler_params=_params("parallel"),
    )(cond, w, b.reshape(1, n))


def _proj_kernel(x_ref, sh_ref, sc_ref, w_ref, o_ref):
    xm = x_ref[...] * (1.0 + sc_ref[0]) + sh_ref[0]
    o_ref[...] = _dot(xm.astype(bf16), w_ref[...])


def _proj(x, shift_t, scale_t, w):
    rows, d = x.shape
    n = w.shape[1]
    return pl.pallas_call(
        _proj_kernel,
        out_shape=jax.ShapeDtypeStruct((rows, n), f32),
        grid=(rows // ROW_TILE,),
        in_specs=[pl.BlockSpec((ROW_TILE, d), lambda i: (i, 0)),
                  pl.BlockSpec((1, 1, d), lambda i: (i, 0, 0)),
                  pl.BlockSpec((1, 1, d), lambda i: (i, 0, 0)),
                  pl.BlockSpec((d, n), lambda i: (0, 0))],
        out_specs=pl.BlockSpec((ROW_TILE, n), lambda i: (i, 0)),
        compiler_params=_params("parallel"),
    )(x, shift_t, scale_t, w)


def _conv_kernel(x_ref, w_ref, b_ref, o_ref, pad_ref, *, seq, seg0, n_norm, n_qscale, qscale):
    j = pl.program_id(1)
    halo = 8
    pad_ref[0:halo, :] = jnp.zeros((halo, LANE), f32)
    pad_ref[halo + seq:2 * halo + seq, :] = jnp.zeros((halo, LANE), f32)
    pad_ref[halo:halo + seq, :] = x_ref[0]
    w = w_ref[...]
    bias = b_ref[...]
    scale = jnp.where(j < n_qscale, qscale, 1.0).astype(f32)
    norm_on = jnp.where(j < n_norm, 1.0, 0.0).astype(f32)
    rows = ROW_TILE

    def body(c, carry):
        r0 = pl.multiple_of(c * rows, rows)
        win = pad_ref[pl.ds(r0, rows + 2 * halo), :]
        t = r0 + lax.broadcasted_iota(jnp.int32, (rows, LANE), 0)
        acc = bias + w[2:3] * win[halo:halo + rows]
        for k in (0, 1, 3, 4):
            off = k - CONV_W // 2
            tap = win[halo + off:halo + off + rows]
            same_segment = ((t + off) >= seg0) == (t >= seg0)
            acc = acc + w[k:k + 1] * jnp.where(same_segment, tap, 0.0)
        y = _silu(acc)
        yn = y * lax.rsqrt(jnp.sum(y * y, -1, keepdims=True) + EPS)
        y = (norm_on * yn + (1.0 - norm_on) * y) * scale
        o_ref[0, pl.ds(r0, rows), :] = y
        return carry

    lax.fori_loop(0, seq // rows, body, 0)


def _conv_act(x, w, b, col_blocks, *, seg0, n_norm, n_qscale, qscale):
    bsz, seq, _ = x.shape
    n_out = len(col_blocks)
    first_gap = next((i for i, cb in enumerate(col_blocks) if cb != i), n_out)
    gap = col_blocks[first_gap] - first_gap if first_gap < n_out else 0
    in_map = lambda bi, j: (bi, 0, jnp.where(j < first_gap, j, j + gap))
    kern = functools.partial(_conv_kernel, seq=seq, seg0=seg0, n_norm=n_norm, n_qscale=n_qscale, qscale=qscale)
    return pl.pallas_call(
        kern,
        out_shape=jax.ShapeDtypeStruct((bsz, seq, n_out * LANE), f32),
        grid=(bsz, n_out),
        in_specs=[pl.BlockSpec((1, seq, LANE), in_map),
                  pl.BlockSpec((CONV_W, LANE), lambda bi, j: (0, j)),
                  pl.BlockSpec((1, LANE), lambda bi, j: (0, j))],
        out_specs=pl.BlockSpec((1, seq, LANE), lambda bi, j: (bi, 0, j)),
        scratch_shapes=[pltpu.VMEM((seq + 16, LANE), f32)],
        compiler_params=_params("parallel", "parallel"),
    )(x, w, b)


def _chunk_index(d, n, n_ctx, n_all):
    return n if d == 0 else jnp.where(n < n_ctx, n_ctx - 1 - n, n_all + n_ctx - 1 - n)


def _both_directions(make_specs):
    return [spec for d in range(N_DIR) for spec in make_specs(d)]


def _gate_layouts(raw, heads):
    bsz, seq, _ = raw.shape
    r = raw.reshape(bsz, seq // CHUNK, CHUNK, N_DIR, heads)
    return r.transpose(3, 0, 1, 2, 4), r.transpose(3, 0, 1, 4, 2)


def _param_layouts(*ps):
    return jnp.stack(ps, axis=1).astype(f32), jnp.stack(ps, axis=2).astype(f32)


def _gdn_kernel(*refs, bsz):
    n_in = 7
    ins = [refs[d * n_in:(d + 1) * n_in] for d in range(N_DIR)]
    pcol_ref, prow_ref = refs[N_DIR * n_in:N_DIR * n_in + 2]
    o_refs = refs[N_DIR * n_in + 2:N_DIR * n_in + 2 + N_DIR]
    st_ref = refs[-1]
    n = pl.program_id(0)

    @pl.when(n == 0)
    def _():
        st_ref[...] = jnp.zeros(st_ref.shape, f32)

    masks = [_chunk_masks(d) for d in range(N_DIR)]
    eye = masks[0][3].astype(f32)
    row = lax.broadcasted_iota(jnp.int32, (CHUNK, CHUNK), 0)
    col = lax.broadcasted_iota(jnp.int32, (CHUNK, CHUNK), 1)
    blocks = [(row >> sh) == (col >> sh) for sh in (3, 4, 5, 6)]
    off_masks = [blocks[lvl + 1] & ~blocks[lvl] for lvl in range(3)]
    chains = [(d, b, h) for d in range(N_DIR) for b in range(bsz) for h in range(A_HEADS)]
    idx = range(len(chains))
    pc = [pcol_ref[d] for d in range(N_DIR)]
    pr = [prow_ref[d] for d in range(N_DIR)]
    beta_col = [[_sigmoid(ins[d][3][0, b, 0]) for b in range(bsz)] for d in range(N_DIR)]
    g_col = [[-jnp.exp(pc[d][0:1, :]) * _softplus(ins[d][5][0, b, 0] + pc[d][1:2, :]) for b in range(bsz)]
             for d in range(N_DIR)]
    g_row = [[-jnp.exp(pr[d][:, 0:1]) * _softplus(ins[d][6][0, b, 0] + pr[d][:, 1:2]) for b in range(bsz)]
             for d in range(N_DIR)]
    sl = [slice(h * A_DK, (h + 1) * A_DK) for _, _, h in chains]
    q = [ins[d][0][b, :, sl[c]] for c, (d, b, _) in enumerate(chains)]
    k = [ins[d][1][b, :, sl[c]] for c, (d, b, _) in enumerate(chains)]
    v = [ins[d][2][b, :, sl[c]] for c, (d, b, _) in enumerate(chains)]
    cums = [_cum(masks[d][0], masks[d][2], g_col[d][b][:, h:h + 1], g_row[d][b][h:h + 1, :]) for d, b, h in chains]
    decay = [jnp.exp(jnp.where(masks[d][0], cums[c][0] - cums[c][1], -jnp.inf)) for c, (d, _, _) in enumerate(chains)]
    b_col = [beta_col[d][b][:, h:h + 1] for d, b, h in chains]
    kb = [k[c] * b_col[c] for c in idx]
    l_mat = [_dot3(kb[c], k[c], NT) * jnp.where(masks[d][1], decay[c], 0.0) for c, (d, _, _) in enumerate(chains)]
    nil = [-jnp.where(blocks[0], l_mat[c], 0.0) for c in idx]
    t_inv = [eye + nil[c] for c in idx]
    for _ in range(2):
        nil = [_dot3(nil[c], nil[c]) for c in idx]
        t_inv = [t_inv[c] + _dot3(t_inv[c], nil[c]) for c in idx]
    for lvl in range(3):
        left = [_dot3(t_inv[c], jnp.where(off_masks[lvl], l_mat[c], 0.0)) for c in idx]
        t_inv = [t_inv[c] - _dot3(left[c], t_inv[c]) for c in idx]
    e_col = [jnp.exp(gc) for gc, _, _ in cums]
    uw = [_dot3(t_inv[c], jnp.concatenate([v[c] * b_col[c], kb[c] * e_col[c]], axis=1)) for c in idx]
    qk = [_dot(q[c], k[c], NT) * decay[c] for c in idx]
    k_dec_t = [(k[c] * jnp.exp(cums[c][2] - cums[c][0])).T for c in idx]
    s = [st_ref[c] for c in idx]
    wq = [_dot(jnp.concatenate([uw[c][:, A_DV:], q[c] * e_col[c]], axis=0), s[c]) for c in idx]
    v_new = [uw[c][:, :A_DV] - wq[c][:CHUNK] for c in idx]
    out = [wq[c][CHUNK:] + _dot(qk[c], v_new[c]) for c in idx]
    s_new = [jnp.exp(cums[c][2]) * s[c] + _dot(k_dec_t[c], v_new[c]) for c in idx]
    for c, (d, b, _) in enumerate(chains):
        o_refs[d][b, :, sl[c]] = out[c]
        st_ref[c] = s_new[c]


def _gdn(act, beta_raw, decay_raw, a_log, dt_bias, n_ctx):
    bsz, seq, _ = act.shape
    n_all = seq // CHUNK
    hw = A_HEADS * A_DK
    bcol, brow = _gate_layouts(beta_raw, A_HEADS)
    dcol, drow = _gate_layouts(decay_raw, A_HEADS)
    pcol, prow = _param_layouts(a_log, dt_bias)
    ci = functools.partial(_chunk_index, n_ctx=n_ctx, n_all=n_all)

    def specs(d):
        tok = lambda blk: pl.BlockSpec((bsz, CHUNK, hw), lambda n: (0, ci(d, n), blk))
        gcol = pl.BlockSpec((1, bsz, 1, CHUNK, A_HEADS), lambda n: (d, 0, ci(d, n), 0, 0))
        grow = pl.BlockSpec((1, bsz, 1, A_HEADS, CHUNK), lambda n: (d, 0, ci(d, n), 0, 0))
        return [tok(0), tok(1), tok(2), gcol, grow, gcol, grow]

    out = jax.ShapeDtypeStruct((bsz, seq, hw), f32)
    return pl.pallas_call(
        functools.partial(_gdn_kernel, bsz=bsz),
        out_shape=(out,) * N_DIR,
        grid=(n_all,),
        in_specs=_both_directions(specs) + [pl.BlockSpec((N_DIR, 2, A_HEADS), lambda n: (0, 0, 0)),
                                            pl.BlockSpec((N_DIR, A_HEADS, 2), lambda n: (0, 0, 0))],
        out_specs=tuple(pl.BlockSpec((bsz, CHUNK, hw), lambda n, d=d: (0, ci(d, n), 0)) for d in range(N_DIR)),
        scratch_shapes=[pltpu.VMEM((N_DIR * bsz * A_HEADS, A_DK, A_DV), f32)],
        compiler_params=_params("arbitrary"),
    )(*([act, act, act, bcol, brow, dcol, drow] * N_DIR), pcol, prow)


def _ssd_kernel(*refs, bsz):
    n_in = 5
    ins = [refs[d * n_in:(d + 1) * n_in] for d in range(N_DIR)]
    pcol_ref, prow_ref = refs[N_DIR * n_in:N_DIR * n_in + 2]
    o_refs = refs[N_DIR * n_in + 2:N_DIR * n_in + 2 + N_DIR]
    st_ref = refs[-1]
    n = pl.program_id(0)

    @pl.when(n == 0)
    def _():
        st_ref[...] = jnp.zeros(st_ref.shape, f32)

    masks = [_chunk_masks(d) for d in range(N_DIR)]
    rep = B_HEADS // B_GROUPS
    streams = [(d, b) for d in range(N_DIR) for b in range(bsz)]
    groups = [(s, g) for s in range(len(streams)) for g in range(B_GROUPS)]
    heads = [(s, h) for s in range(len(streams)) for h in range(B_HEADS)]
    gsl = [slice(g * B_STATE, (g + 1) * B_STATE) for _, g in groups]
    hsl = [slice(h * B_HEADDIM, (h + 1) * B_HEADDIM) for _, h in heads]
    pc = [pcol_ref[d] for d, _ in streams]
    pr = [prow_ref[d] for d, _ in streams]
    dt_col = [_softplus(ins[d][3][0, b, 0] + pc[s][1:2, :]) for s, (d, b) in enumerate(streams)]
    dt_row = [_softplus(ins[d][4][0, b, 0] + pr[s][:, 1:2]) for s, (d, b) in enumerate(streams)]
    da_col = [dt_col[s] * (-jnp.exp(pc[s][0:1, :])) for s in range(len(streams))]
    da_row = [dt_row[s] * (-jnp.exp(pr[s][:, 0:1])) for s in range(len(streams))]
    bm = [ins[streams[s][0]][1][streams[s][1], :, gsl[i]] for i, (s, _) in enumerate(groups)]
    cm = [ins[streams[s][0]][2][streams[s][1], :, gsl[i]] for i, (s, _) in enumerate(groups)]
    hs = [st_ref[i] for i in range(len(groups))]
    cb = [_dot(cm[i], bm[i], NT) for i in range(len(groups))]
    y_off = [_dot(cm[i], hs[i]) for i in range(len(groups))]
    bm_t = [bm[i].T for i in range(len(groups))]
    m_incl = [masks[streams[s][0]][0] for s, _ in heads]
    m_incl_t = [masks[streams[s][0]][2] for s, _ in heads]
    cums = [_cum(m_incl[j], m_incl_t[j], da_col[s][:, h:h + 1], da_row[s][h:h + 1, :]) for j, (s, h) in enumerate(heads)]
    seg = [jnp.exp(jnp.where(m_incl[j], cums[j][0] - cums[j][1], -jnp.inf)) for j in range(len(heads))]
    xdt = [ins[streams[s][0]][0][streams[s][1], :, hsl[j]] * dt_col[s][:, h:h + 1] for j, (s, h) in enumerate(heads)]
    y_diag = [_dot(cb[j // rep] * seg[j], xdt[j]) for j in range(len(heads))]
    for j, (s, h) in enumerate(heads):
        r = h % rep
        d, b = streams[s]
        o_refs[d][b, :, hsl[j]] = y_diag[j] + jnp.exp(cums[j][0]) * y_off[j // rep][:, r * B_HEADDIM:(r + 1) * B_HEADDIM]
    xdt_end = [xdt[j] * jnp.exp(cums[j][2] - cums[j][0]) for j in range(len(heads))]
    keep = [jnp.broadcast_to(jnp.exp(cums[j][2]), (1, B_HEADDIM)) for j in range(len(heads))]
    upd = [_dot(bm_t[i], jnp.concatenate(xdt_end[i * rep:(i + 1) * rep], axis=1)) for i in range(len(groups))]
    for i in range(len(groups)):
        st_ref[i] = jnp.concatenate(keep[i * rep:(i + 1) * rep], axis=1) * hs[i] + upd[i]


def _ssd(act, dt_raw, a_log, dt_bias, n_ctx):
    bsz, seq, _ = act.shape
    n_all = seq // CHUNK
    dinner = B_HEADS * B_HEADDIM
    gw = B_GROUPS * B_STATE
    tcol, trow = _gate_layouts(dt_raw, B_HEADS)
    pcol, prow = _param_layouts(a_log, dt_bias)
    ci = functools.partial(_chunk_index, n_ctx=n_ctx, n_all=n_all)
    x_off = (A_HEADS * (2 * A_DK + A_DV)) // dinner
    bm_off = (A_HEADS * (2 * A_DK + A_DV) + dinner) // gw

    def specs(d):
        return [pl.BlockSpec((bsz, CHUNK, dinner), lambda n: (0, ci(d, n), x_off)),
                pl.BlockSpec((bsz, CHUNK, gw), lambda n: (0, ci(d, n), bm_off)),
                pl.BlockSpec((bsz, CHUNK, gw), lambda n: (0, ci(d, n), bm_off + 1)),
                pl.BlockSpec((1, bsz, 1, CHUNK, B_HEADS), lambda n: (d, 0, ci(d, n), 0, 0)),
                pl.BlockSpec((1, bsz, 1, B_HEADS, CHUNK), lambda n: (d, 0, ci(d, n), 0, 0))]

    out = jax.ShapeDtypeStruct((bsz, seq, dinner), f32)
    return pl.pallas_call(
        functools.partial(_ssd_kernel, bsz=bsz),
        out_shape=(out,) * N_DIR,
        grid=(n_all,),
        in_specs=_both_directions(specs) + [pl.BlockSpec((N_DIR, 2, B_HEADS), lambda n: (0, 0, 0)),
                                            pl.BlockSpec((N_DIR, B_HEADS, 2), lambda n: (0, 0, 0))],
        out_specs=tuple(pl.BlockSpec((bsz, CHUNK, dinner), lambda n, d=d: (0, ci(d, n), 0)) for d in range(N_DIR)),
        scratch_shapes=[pltpu.VMEM((N_DIR * bsz * B_GROUPS, B_STATE, (B_HEADS // B_GROUPS) * B_HEADDIM), f32)],
        compiler_params=_params("arbitrary"),
    )(*([act, act, act, tcol, trow] * N_DIR), pcol, prow)


def _mlstm_kernel(*refs, bsz):
    n_in = 7
    ins = [refs[d * n_in:(d + 1) * n_in] for d in range(N_DIR)]
    pcol_ref, prow_ref = refs[N_DIR * n_in:N_DIR * n_in + 2]
    o_refs = refs[N_DIR * n_in + 2:N_DIR * n_in + 2 + N_DIR]
    c_ref, n_ref, m_ref = refs[-3:]
    n = pl.program_id(0)

    @pl.when(n == 0)
    def _():
        c_ref[...] = jnp.zeros(c_ref.shape, f32)
        n_ref[...] = jnp.zeros(n_ref.shape, f32)
        m_ref[...] = jnp.zeros(m_ref.shape, f32)

    masks = [_chunk_masks(d) for d in range(N_DIR)]
    streams = [(d, b) for d in range(N_DIR) for b in range(bsz)]
    chains = [(s, h) for s in range(len(streams)) for h in range(C_HEADS)]
    idx = range(len(chains))
    pc = [pcol_ref[d] for d, _ in streams]
    pr = [prow_ref[d] for d, _ in streams]
    li_col = [ins[d][3][0, b, 0] + pc[s][0:1, :] for s, (d, b) in enumerate(streams)]
    li_row = [ins[d][4][0, b, 0] + pr[s][:, 0:1] for s, (d, b) in enumerate(streams)]
    lf_col = [-_softplus(-(ins[d][5][0, b, 0] + pc[s][1:2, :])) for s, (d, b) in enumerate(streams)]
    lf_row = [-_softplus(-(ins[d][6][0, b, 0] + pr[s][:, 1:2])) for s, (d, b) in enumerate(streams)]
    q = [ins[streams[s][0]][0][streams[s][1], :, h * C_DK:(h + 1) * C_DK] for s, h in chains]
    k = [ins[streams[s][0]][1][streams[s][1], :, h * C_DK:(h + 1) * C_DK] for s, h in chains]
    v = [ins[streams[s][0]][2][streams[s][1], :, h * C_DV:(h + 1) * C_DV] for s, h in chains]
    m_incl = [masks[streams[s][0]][0] for s, _ in chains]
    m_incl_t = [masks[streams[s][0]][2] for s, _ in chains]
    cums = [_cum(m_incl[c], m_incl_t[c], lf_col[s][:, h:h + 1], lf_row[s][h:h + 1, :]) for c, (s, h) in enumerate(chains)]
    logw_intra = [jnp.where(m_incl[c], cums[c][0] - cums[c][1] + li_row[s][h:h + 1, :], -jnp.inf)
                  for c, (s, h) in enumerate(chains)]
    m_prev = [m_ref[c] for c in idx]
    logw_inter = [cums[c][0] + m_prev[c] for c in idx]
    m_t = [jnp.maximum(logw_inter[c], jnp.max(logw_intra[c], axis=1, keepdims=True)) for c in idx]
    qk = [_dot(q[c], k[c], NT) for c in idx]
    c_mem = [c_ref[c] for c in idx]
    n_mem = [n_ref[c] for c in idx]
    qc = [_dot(q[c], c_mem[c]) for c in idx]
    s = [qk[c] * jnp.exp(logw_intra[c] - m_t[c]) for c in idx]
    w_inter = [jnp.exp(logw_inter[c] - m_t[c]) for c in idx]
    num = [_dot(s[c], v[c]) + w_inter[c] * qc[c] for c in idx]
    den = [jnp.sum(s[c], axis=1, keepdims=True) + w_inter[c] * jnp.sum(q[c] * n_mem[c], axis=1, keepdims=True)
           for c in idx]
    for c, (s, h) in enumerate(chains):
        d, b = streams[s]
        o_refs[d][b, :, h * C_DV:(h + 1) * C_DV] = num[c] / jnp.maximum(jnp.abs(den[c]), jnp.exp(-m_t[c]))
    logw_end = [cums[c][2] - cums[c][0] + li_col[s][:, h:h + 1] for c, (s, h) in enumerate(chains)]
    m_new = [jnp.maximum(cums[c][2] + m_prev[c], jnp.max(logw_end[c], axis=0, keepdims=True)) for c in idx]
    kw = [k[c] * jnp.exp(logw_end[c] - m_new[c]) for c in idx]
    keep = [jnp.exp(cums[c][2] + m_prev[c] - m_new[c]) for c in idx]
    kv = [_dot(kw[c].T, v[c]) for c in idx]
    for c in idx:
        c_ref[c] = keep[c] * c_mem[c] + kv[c]
        n_ref[c] = keep[c] * n_mem[c] + jnp.sum(kw[c], axis=0, keepdims=True)
        m_ref[c] = m_new[c]


def _mlstm(act, proj, i_raw, f_raw, i_bias, f_bias, n_ctx):
    bsz, seq, _ = act.shape
    n_all = seq // CHUNK
    qw = C_HEADS * C_DK
    vw = C_HEADS * C_DV
    icol, irow = _gate_layouts(i_raw, C_HEADS)
    fcol, frow = _gate_layouts(f_raw, C_HEADS)
    pcol, prow = _param_layouts(i_bias, f_bias)
    ci = functools.partial(_chunk_index, n_ctx=n_ctx, n_all=n_all)

    def specs(d):
        gcol = pl.BlockSpec((1, bsz, 1, CHUNK, C_HEADS), lambda n: (d, 0, ci(d, n), 0, 0))
        grow = pl.BlockSpec((1, bsz, 1, C_HEADS, CHUNK), lambda n: (d, 0, ci(d, n), 0, 0))
        return [pl.BlockSpec((bsz, CHUNK, qw), lambda n: (0, ci(d, n), 0)),
                pl.BlockSpec((bsz, CHUNK, qw), lambda n: (0, ci(d, n), 1)),
                pl.BlockSpec((bsz, CHUNK, vw), lambda n: (0, ci(d, n), 2 * qw // vw)),
                gcol, grow, gcol, grow]

    out = jax.ShapeDtypeStruct((bsz, seq, vw), f32)
    n_chain = N_DIR * bsz * C_HEADS
    return pl.pallas_call(
        functools.partial(_mlstm_kernel, bsz=bsz),
        out_shape=(out,) * N_DIR,
        grid=(n_all,),
        in_specs=_both_directions(specs) + [pl.BlockSpec((N_DIR, 2, C_HEADS), lambda n: (0, 0, 0)),
                                            pl.BlockSpec((N_DIR, C_HEADS, 2), lambda n: (0, 0, 0))],
        out_specs=tuple(pl.BlockSpec((bsz, CHUNK, vw), lambda n, d=d: (0, ci(d, n), 0)) for d in range(N_DIR)),
        scratch_shapes=[pltpu.VMEM((n_chain, C_DK, C_DV), f32),
                        pltpu.VMEM((n_chain, 1, C_DK), f32),
                        pltpu.VMEM((n_chain, 1, 1), f32)],
        compiler_params=_params("arbitrary"),
    )(*([act, act, proj, icol, irow, fcol, frow] * N_DIR), pcol, prow)


def _merge_even_kernel(x_ref, af_ref, ar_ref, bf_ref, br_ref, za_ref, zb_ref, xs_ref, na_ref, nb_ref, dsk_ref,
                       w_ref, g_ref, lg_ref, lb_ref, o_ref):
    a = af_ref[...] + ar_ref[...]
    za = za_ref[...]
    na = na_ref[...]
    parts = []
    for h in range(A_HEADS):
        sl = slice(h * A_DV, (h + 1) * A_DV)
        parts.append(_rms(a[:, sl], na) * _silu(za[:, sl]))
    yb = (bf_ref[...] + br_ref[...] + dsk_ref[...] * xs_ref[...]) * _silu(zb_ref[...])
    gw = (B_HEADS * B_HEADDIM) // B_GROUPS
    nb = nb_ref[...]
    for g in range(B_GROUPS):
        sl = slice(g * gw, (g + 1) * gw)
        parts.append(_rms(yb[:, sl], nb[:, sl]))
    y = jnp.concatenate(parts, axis=1).astype(bf16)
    z = ALPHA * x_ref[...] + g_ref[0] * _dot(y, w_ref[...])
    o_ref[...] = _layer_norm(z, lg_ref[...], lb_ref[...])


def _merge_even(x, o_a, o_b, proj, act, norm_a, norm_b, d_skip, w_out, gate_t, ln_g, ln_b):
    rows, d = x.shape
    aw = A_HEADS * A_DV
    bw = B_HEADS * B_HEADDIM
    row = lambda w_, blk: pl.BlockSpec((ROW_TILE, w_), lambda i: (i, blk))
    vec = lambda w_: pl.BlockSpec((1, w_), lambda i: (0, 0))
    qkv = A_HEADS * (2 * A_DK + A_DV)
    return pl.pallas_call(
        _merge_even_kernel,
        out_shape=jax.ShapeDtypeStruct((rows, d), f32),
        grid=(rows // ROW_TILE,),
        in_specs=[row(d, 0), row(aw, 0), row(aw, 0), row(bw, 0), row(bw, 0),
                  row(aw, qkv // aw), row(bw, (qkv + aw) // bw), row(bw, qkv // bw),
                  vec(A_DV), vec(bw), vec(bw),
                  pl.BlockSpec((aw + bw, d), lambda i: (0, 0)),
                  pl.BlockSpec((1, 1, d), lambda i: (i, 0, 0)),
                  vec(d), vec(d)],
        out_specs=row(d, 0),
        compiler_params=_params("parallel"),
    )(x, *o_a, *o_b, proj, proj, act, norm_a.reshape(1, -1), norm_b.reshape(1, -1),
      jnp.repeat(d_skip, B_HEADDIM).reshape(1, -1), w_out, gate_t, ln_g.reshape(1, -1), ln_b.reshape(1, -1))


def _merge_odd_kernel(hf_ref, hr_ref, o_ref_in, nc_ref, w_ref, out_ref):
    hsum = hf_ref[...] + hr_ref[...]
    o = o_ref_in[...]
    nc = nc_ref[...]
    parts = []
    for h in range(C_HEADS):
        sl = slice(h * C_DV, (h + 1) * C_DV)
        parts.append(_rms(hsum[:, sl], nc) * _sigmoid(o[:, sl]))
    y = jnp.concatenate(parts, axis=1).astype(bf16)
    out_ref[...] = _dot(y, w_ref[...])


def _merge_odd(h, proj, norm_c, w_out, tiles_per_batch, ctx_tiles, bsz):
    vw = C_HEADS * C_DV
    lat_tiles = tiles_per_batch - ctx_tiles
    src = lambda i: i + ctx_tiles * (i // lat_tiles + 1)
    d = w_out.shape[1]
    return pl.pallas_call(
        _merge_odd_kernel,
        out_shape=jax.ShapeDtypeStruct((bsz * lat_tiles * ROW_TILE, d), f32),
        grid=(bsz * lat_tiles,),
        in_specs=[pl.BlockSpec((ROW_TILE, vw), lambda i: (src(i), 0)),
                  pl.BlockSpec((ROW_TILE, vw), lambda i: (src(i), 0)),
                  pl.BlockSpec((ROW_TILE, vw), lambda i: (src(i), 2 * C_HEADS * C_DK // vw + 1)),
                  pl.BlockSpec((1, C_DV), lambda i: (0, 0)),
                  pl.BlockSpec((vw, d), lambda i: (0, 0))],
        out_specs=pl.BlockSpec((ROW_TILE, d), lambda i: (i, 0)),
        compiler_params=_params("parallel"),
    )(*h, proj, norm_c.reshape(1, -1), w_out)


def _resid_ln_kernel(x_ref, y_ref, g_ref, lg_ref, lb_ref, o_ref, *, transposed):
    y = y_ref[...]
    if transposed:
        y = y.T
    o_ref[...] = _layer_norm(ALPHA * x_ref[...] + g_ref[0] * y, lg_ref[...], lb_ref[...])


def _resid_ln(x, y, gate_t, ln_g, ln_b, *, transposed):
    rows, d = x.shape
    y_spec = (pl.BlockSpec((d, ROW_TILE), lambda i: (0, i)) if transposed
              else pl.BlockSpec((ROW_TILE, d), lambda i: (i, 0)))
    return pl.pallas_call(
        functools.partial(_resid_ln_kernel, transposed=transposed),
        out_shape=jax.ShapeDtypeStruct((rows, d), f32),
        grid=(rows // ROW_TILE,),
        in_specs=[pl.BlockSpec((ROW_TILE, d), lambda i: (i, 0)), y_spec,
                  pl.BlockSpec((1, 1, d), lambda i: (i, 0, 0)),
                  pl.BlockSpec((1, d), lambda i: (0, 0)),
                  pl.BlockSpec((1, d), lambda i: (0, 0))],
        out_specs=pl.BlockSpec((ROW_TILE, d), lambda i: (i, 0)),
        compiler_params=_params("parallel"),
    )(x, y, gate_t, ln_g.reshape(1, -1), ln_b.reshape(1, -1))


def _oddeven_merge(lo, hi, r):
    step = r * 2
    if step < hi - lo:
        yield from _oddeven_merge(lo, hi, step)
        yield from _oddeven_merge(lo + r, hi, step)
        yield from [(i, i + r) for i in range(lo + r, hi - r, step)]
    else:
        yield (lo, lo + r)


def _oddeven_sort(lo, hi):
    if hi - lo >= 1:
        mid = lo + (hi - lo) // 2
        yield from _oddeven_sort(lo, mid)
        yield from _oddeven_sort(mid + 1, hi)
        yield from _oddeven_merge(lo, hi, 1)


def _exchange(a, i, j):
    a[i], a[j] = jnp.maximum(a[i], a[j]), jnp.minimum(a[i], a[j])


def _top_sorted(slabs):
    a = list(slabs)
    for i, j in _oddeven_sort(0, len(a) - 1):
        _exchange(a, i, j)
    for shift in (4, 2, 1):
        other = [pltpu.roll(x, shift, axis=0) for x in a]
        if len(a) < PEER_TOPK:
            a = a + other[::-1]
        else:
            a = [jnp.maximum(a[k], other[PEER_TOPK - 1 - k]) for k in range(PEER_TOPK)]
        dist = PEER_TOPK // 2
        while dist >= 1:
            for i in range(PEER_TOPK):
                if i & dist == 0:
                    _exchange(a, i, i + dist)
            dist //= 2
    return a


def _next_largest(slabs, kth):
    count = sum(jnp.where(s >= kth, 1.0, 0.0) for s in slabs)
    below = functools.reduce(jnp.maximum, [jnp.where(s < kth, s, -jnp.inf) for s in slabs])
    count = jnp.sum(count, axis=0, keepdims=True)
    below = jnp.max(below, axis=0, keepdims=True)
    return jnp.where(count > PEER_TOPK, kth, below)


def _top_values(s):
    slabs = [s[i:i + 8] for i in range(0, s.shape[0], 8)]
    if len(slabs) < 8:
        slabs = slabs + [jnp.full_like(slabs[0], -jnp.inf)] * (8 - len(slabs))
    top = [t[0:1] for t in _top_sorted(slabs)]
    return top + [_next_largest(slabs, top[-1])]


def _route_kernel(x_ref, sh_ref, sc_ref, wh_ref, wl_ref, kh_ref, kl_ref,
                  xm_ref, thr_ref, s2_ref, e1_ref, e2_ref, xh_ref, xl_ref, sc_scr, cand_ref):
    for t in range(ROUTE_TILE // ROW_TILE):
        rows = slice(t * ROW_TILE, (t + 1) * ROW_TILE)
        xm = x_ref[rows, :] * (1.0 + sc_ref[t]) + sh_ref[t]
        xm_ref[rows, :] = xm.astype(bf16)
        xh_ref[rows, :], xl_ref[rows, :] = _split(xm)
    half = PEER_DKEY // 2
    n_top = PEER_TOPK + 1
    pairs = [(p, r) for p in range(n_top) for r in range(n_top) if (p + 1) * (r + 1) <= n_top]
    cand_ref[...] = jnp.full(cand_ref.shape, -jnp.inf, f32)

    def scores(h, sc_scr):
        wsl = pl.ds(pl.multiple_of(h * PEER_DKEY, PEER_DKEY), PEER_DKEY)
        xh = xh_ref[...]
        q = _dot(xh, wh_ref[:, wsl])
        for c in range(2):
            qh = q[:, c * half:(c + 1) * half].astype(bf16)
            sc_scr[c] = _dot(kh_ref[h, c], qh, NT)

    def select(h, sc_scr):
        for lb in range(ROUTE_TILE // LANE):
            lanes = slice(lb * LANE, (lb + 1) * LANE)
            s1 = sc_scr[0, :, lanes]
            s2 = sc_scr[1, :, lanes]
            t1 = _top_values(s1)
            t2 = _top_values(s2)
            for i, (p, r) in enumerate(pairs):
                cand_ref[i:i + 1, lanes] = t1[p] + t2[r]
            best = _top_values(cand_ref[:, lanes])
            z = sum(jnp.exp(bv - best[0]) for bv in best[:PEER_TOPK])
            tau = 0.5 * (best[PEER_TOPK - 1] + best[PEER_TOPK])
            thr_ref[h, :, lanes] = tau - s1
            s2_ref[h, :, lanes] = s2
            e1_ref[h, :, lanes] = jnp.exp(s1 - t1[0]) / z
            e2_ref[h, :, lanes] = jnp.exp(s2 - t2[0])

    def head(h, carry):
        scores(h, sc_scr)
        select(h, sc_scr)
        return carry

    lax.fori_loop(0, PEER_HEADS, head, 0)


def _route(x, shift_t, scale_t, wq_hi, wq_lo, keys_hi, keys_lo):
    rows, d = x.shape
    nq = wq_hi.shape[1]
    half = PEER_DKEY // 2
    score = jax.ShapeDtypeStruct((PEER_HEADS, PEER_NKEYS, rows), f32)
    score_spec = pl.BlockSpec((PEER_HEADS, PEER_NKEYS, ROUTE_TILE), lambda i: (0, 0, i))
    key_spec = pl.BlockSpec((PEER_HEADS, 2, PEER_NKEYS, half), lambda i: (0, 0, 0, 0))
    mods = ROUTE_TILE // ROW_TILE
    return pl.pallas_call(
        _route_kernel,
        out_shape=(jax.ShapeDtypeStruct((rows, d), bf16), score, score, score, score),
        grid=(rows // ROUTE_TILE,),
        in_specs=[pl.BlockSpec((ROUTE_TILE, d), lambda i: (i, 0)),
                  pl.BlockSpec((mods, 1, d), lambda i: (i, 0, 0)),
                  pl.BlockSpec((mods, 1, d), lambda i: (i, 0, 0)),
                  pl.BlockSpec((d, nq), lambda i: (0, 0)),
                  pl.BlockSpec((d, nq), lambda i: (0, 0)),
                  key_spec, key_spec],
        out_specs=(pl.BlockSpec((ROUTE_TILE, d), lambda i: (i, 0)),
                   score_spec, score_spec, score_spec, score_spec),
        scratch_shapes=[pltpu.VMEM((ROUTE_TILE, d), bf16), pltpu.VMEM((ROUTE_TILE, d), bf16),
                        pltpu.VMEM((2, PEER_NKEYS, ROUTE_TILE), f32),
                        pltpu.VMEM((N_CAND_ROWS, ROUTE_TILE), f32)],
        compiler_params=_params("parallel"),
    )(x, shift_t, scale_t, wq_hi, wq_lo, keys_hi, keys_lo)


def _expert_kernel(xm_ref, u_ref, vt_ref, thr_ref, e1_ref, s2_ref, e2_ref, o_ref, a_scr, w_scr):
    j = pl.program_id(1)

    @pl.when(j == 0)
    def _():
        o_ref[...] = jnp.zeros(o_ref.shape, f32)

    a_scr[...] = _dot(u_ref[...], xm_ref[...], NT)
    sqrt_half = math.sqrt(0.5)
    row_chunks = PEER_NKEYS // GATE_ROWS
    tiles_per_group = (EXPERT_TOK // LANE) * row_chunks

    def tile(r, carry, i1_0):
        lanes = pl.ds(pl.multiple_of((r // row_chunks) * LANE, LANE), LANE)
        r0 = pl.multiple_of((r % row_chunks) * GATE_ROWS, GATE_ROWS)
        i2 = pl.ds(r0, GATE_ROWS)
        gate = [jnp.zeros((GATE_ROWS, LANE), f32) for _ in range(I1_TILE)]
        for h in range(PEER_HEADS):
            s2 = s2_ref[h, i2, lanes]
            e2 = e2_ref[h, i2, lanes]
            for il in range(I1_TILE):
                i1 = i1_0 + il
                w = e1_ref[h, i1:i1 + 1, lanes] * e2
                gate[il] = gate[il] + jnp.where(s2 >= thr_ref[h, i1:i1 + 1, lanes], w, 0.0)
        for il in range(I1_TILE):
            rows = pl.ds((i1_0 + il) * PEER_NKEYS + r0, GATE_ROWS)
            a = a_scr[rows, lanes]
            act = 0.5 * a * (1.0 + lax.erf(a * sqrt_half))
            w_scr[rows, lanes] = (gate[il] * act).astype(bf16)
        return carry

    for i1_0 in range(0, EXPERT_BLK // PEER_NKEYS, I1_TILE):
        lax.fori_loop(0, tiles_per_group, functools.partial(tile, i1_0=i1_0), 0)
    o_ref[...] += _dot(vt_ref[...], w_scr[...])


def _experts(xm, u_bf, vt_bf, thr, e1, s2, e2):
    rows, d = xm.shape
    n_exp = u_bf.shape[0]
    i1_per_blk = EXPERT_BLK // PEER_NKEYS
    sel = pl.BlockSpec((PEER_HEADS, i1_per_blk, EXPERT_TOK), lambda i, j: (0, j, i))
    full = pl.BlockSpec((PEER_HEADS, PEER_NKEYS, EXPERT_TOK), lambda i, j: (0, 0, i))
    return pl.pallas_call(
        _expert_kernel,
        out_shape=jax.ShapeDtypeStruct((d, rows), f32),
        grid=(rows // EXPERT_TOK, n_exp // EXPERT_BLK),
        in_specs=[pl.BlockSpec((EXPERT_TOK, d), lambda i, j: (i, 0)),
                  pl.BlockSpec((EXPERT_BLK, d), lambda i, j: (j, 0)),
                  pl.BlockSpec((d, EXPERT_BLK), lambda i, j: (0, j)),
                  sel, sel, full, full],
        out_specs=pl.BlockSpec((d, EXPERT_TOK), lambda i, j: (0, i)),
        scratch_shapes=[pltpu.VMEM((EXPERT_BLK, EXPERT_TOK), f32),
                        pltpu.VMEM((EXPERT_BLK, EXPERT_TOK), bf16)],
        compiler_params=_params("parallel", "arbitrary"),
    )(xm, u_bf, vt_bf, thr, e1, s2, e2)


def _peer_residual(x, shift_t, scale_t, gate_t, peer_q, peer_keys, peer_u, peer_v, ln_g, ln_b):
    wq_hi, wq_lo = _split(peer_q)
    keys_hi, keys_lo = _split(peer_keys)
    xm, thr, s2, e1, e2 = _route(x, shift_t, scale_t, wq_hi, wq_lo, keys_hi, keys_lo)
    y_t = _experts(xm, peer_u.astype(bf16), peer_v.T.astype(bf16), thr, e1, s2, e2)
    return _resid_ln(x, y_t, gate_t, ln_g, ln_b, transposed=True)


def _grid_sincos(n_tokens, d):
    n_rows = n_tokens // GRID_W
    n_freq = d // 4
    freq = jnp.exp(-math.log(POS_BASE) * jnp.arange(n_freq, dtype=f32) / n_freq)[None, :]
    row = jnp.arange(n_rows, dtype=f32)[:, None] * freq
    col = jnp.arange(GRID_W, dtype=f32)[:, None] * freq
    row_part = jnp.concatenate([jnp.sin(row), jnp.cos(row)], -1)[:, None, :]
    col_part = jnp.concatenate([jnp.sin(col), jnp.cos(col)], -1)[None, :, :]
    shape = (n_rows, GRID_W, 2 * n_freq)
    table = jnp.concatenate([jnp.broadcast_to(row_part, shape), jnp.broadcast_to(col_part, shape)], -1)
    return table.reshape(n_tokens, d)


def _tile_rows(mod, src):
    d = mod.shape[1] // 6
    rows = jnp.concatenate([jnp.broadcast_to(mod[r:r + 1], (n, mod.shape[1])) for r, n in src], axis=0)
    return [rows[:, None, i * d:(i + 1) * d] for i in range(6)]


def _pad_cols(w, n):
    return jnp.pad(w, ((0, 0), (0, n - w.shape[1])))


def kernel(x, c, ctx, c_ctx, l0_ada_w, l0_ada_b, l0_ln1_g, l0_ln1_b, l0_w_in, l0_conv_a, l0_a_log_a, l0_dt_bias_a, l0_norm_a, l0_conv_b, l0_conv_b_bias, l0_a_log_b, l0_dt_bias_b, l0_d_skip_b, l0_norm_b, l0_w_out, l0_ln2_g, l0_ln2_b, l0_peer_q, l0_peer_keys, l0_peer_u, l0_peer_v, l1_ada_w, l1_ada_b, l1_ln1_g, l1_ln1_b, l1_w_in, l1_conv_c, l1_i_bias, l1_f_bias, l1_norm_c, l1_w_out, l1_ln2_g, l1_ln2_b, l1_peer_q, l1_peer_keys, l1_peer_u, l1_peer_v):
    bsz, n_lat, d = x.shape
    n_ctx_tok = ctx.shape[1]
    rows_grid = n_lat // GRID_W
    seq = n_ctx_tok + n_lat
    n_ctx = n_ctx_tok // CHUNK
    tiles_b = seq // ROW_TILE
    ctx_tiles = n_ctx_tok // ROW_TILE
    lat_tiles = n_lat // ROW_TILE

    cond = jnp.concatenate([c, c_ctx[None, :], jnp.zeros((8 - bsz - 1, d), f32)], axis=0)
    src_all = [run for b in range(bsz) for run in ((bsz, ctx_tiles), (b, lat_tiles))]

    x_lat = x + _grid_sincos(n_lat, d).astype(x.dtype)
    xs_all = jnp.concatenate([ctx, x_lat], axis=1).reshape(bsz * seq, d)

    sh1, sc1, g1, sh2, sc2, g2 = _tile_rows(_ada(cond, l0_ada_w, l0_ada_b), src_all)
    qkv_w = A_HEADS * (2 * A_DK + A_DV)
    az = A_HEADS * A_DV
    ag = N_DIR * A_HEADS
    bd = B_HEADS * B_HEADDIM
    bx = bd + 2 * B_GROUPS * B_STATE
    bt = N_DIR * B_HEADS
    o0 = np.cumsum([0, qkv_w, az, ag, ag, bd, bx, bt])
    w_qkv, w_za, w_beta, w_dec, w_zb, w_xbc, w_dt = (l0_w_in[:, o0[i]:o0[i + 1]] for i in range(7))
    main_w = qkv_w + az + bd + bx
    w0 = _pad_cols(jnp.concatenate([w_qkv, w_za, w_zb, w_xbc, w_beta, w_dec, w_dt], axis=1), main_w + LANE)
    proj = _proj(xs_all, sh1, sc1, w0.astype(bf16)).reshape(bsz, seq, main_w + LANE)
    conv_w = jnp.concatenate([l0_conv_a, l0_conv_b], axis=1)
    conv_b = jnp.concatenate([jnp.zeros((qkv_w,), f32), l0_conv_b_bias]).reshape(1, -1)
    xbc_blk = (qkv_w + az + bd) // LANE
    col_blocks = list(range(qkv_w // LANE)) + list(range(xbc_blk, xbc_blk + bx // LANE))
    act = _conv_act(proj, conv_w, conv_b, col_blocks, seg0=n_ctx_tok,
                    n_norm=2 * A_HEADS, n_qscale=A_HEADS, qscale=A_DK ** -0.5)
    gates = proj[:, :, main_w:]
    o_a = _gdn(act, gates[..., :ag], gates[..., ag:2 * ag], l0_a_log_a, l0_dt_bias_a, n_ctx)
    o_b = _ssd(act, gates[..., 2 * ag:2 * ag + bt], l0_a_log_b, l0_dt_bias_b, n_ctx)
    x1 = _merge_even(xs_all, [o.reshape(bsz * seq, az) for o in o_a], [o.reshape(bsz * seq, bd) for o in o_b],
                     proj.reshape(bsz * seq, -1), act.reshape(bsz * seq, -1),
                     l0_norm_a, l0_norm_b, l0_d_skip_b, l0_w_out.astype(bf16), g1, l0_ln1_g, l0_ln1_b)
    x2 = _peer_residual(x1, sh2, sc2, g2, l0_peer_q, l0_peer_keys, l0_peer_u, l0_peer_v, l0_ln2_g, l0_ln2_b)

    sh1, sc1, g1, sh2, sc2, g2 = _tile_rows(_ada(cond, l1_ada_w, l1_ada_b), src_all)
    x2 = x2.reshape(bsz, seq, d)
    x2_ctx, x2_lat = x2[:, :n_ctx_tok], x2[:, n_ctx_tok:]
    lat_cm = x2_lat.reshape(bsz, rows_grid, GRID_W, d).transpose(0, 2, 1, 3).reshape(bsz, n_lat, d)
    xs1 = jnp.concatenate([x2_ctx, lat_cm], axis=1).reshape(bsz * seq, d)
    qk_w = 2 * C_HEADS * C_DK
    vw = C_HEADS * C_DV
    w1 = _pad_cols(l1_w_in, qk_w + 2 * vw + LANE)
    proj1 = _proj(xs1, sh1, sc1, w1.astype(bf16)).reshape(bsz, seq, -1)
    act1 = _conv_act(proj1, l1_conv_c, jnp.zeros((1, qk_w), f32), list(range(qk_w // LANE)), seg0=n_ctx_tok,
                     n_norm=0, n_qscale=C_HEADS, qscale=C_DK ** -0.5)
    gates1 = proj1[:, :, qk_w + 2 * vw:]
    cg = N_DIR * C_HEADS
    h1 = _mlstm(act1, proj1, gates1[..., :cg], gates1[..., cg:2 * cg], l1_i_bias, l1_f_bias, n_ctx)
    y_cm = _merge_odd([h.reshape(bsz * seq, vw) for h in h1], proj1.reshape(bsz * seq, -1), l1_norm_c,
                      l1_w_out.astype(bf16), tiles_b, ctx_tiles, bsz)
    y_lat = y_cm.reshape(bsz, GRID_W, rows_grid, d).transpose(0, 2, 1, 3).reshape(bsz * n_lat, d)
    lat_sel = lambda t: t.reshape(bsz, tiles_b, 1, d)[:, ctx_tiles:].reshape(bsz * lat_tiles, 1, d)
    x3 = _resid_ln(x2_lat.reshape(bsz * n_lat, d), y_lat, lat_sel(g1), l1_ln1_g, l1_ln1_b, transposed=False)
    x4 = _peer_residual(x3, lat_sel(sh2), lat_sel(sc2), lat_sel(g2), l1_peer_q, l1_peer_keys, l1_peer_u,
                        l1_peer_v, l1_ln2_g, l1_ln2_b)
    return x4.reshape(bsz, n_lat, d)
```

```python
import functools
import math

import numpy as np
import jax
import jax.numpy as jnp
from jax import lax
from jax.experimental import pallas as pl
from jax.experimental.pallas import tpu as pltpu

f32 = jnp.float32
bf16 = jnp.bfloat16

D_MODEL = 1024
GRID_W = 64
CHUNK = 64
CONV_W = 5
POS_BASE = 10000.0
EPS = 1e-6
N_DIR = 2
DEPTH = 2
ALPHA = (2 * DEPTH) ** 0.25

A_HEADS, A_DK, A_DV = 4, 128, 128
B_HEADS, B_HEADDIM, B_GROUPS, B_STATE = 8, 64, 2, 128
C_HEADS, C_DK, C_DV = 4, 128, 256
PEER_HEADS, PEER_NKEYS, PEER_DKEY, PEER_TOPK = 8, 128, 256, 16

LANE = 128
ROW_TILE = 256
ROUTE_TILE = 512
EXPERT_TOK = 512
EXPERT_BLK = 2048
I1_TILE = 8
GATE_ROWS = 32
VMEM_LIMIT = 56 * 1024 * 1024
N_CAND_ROWS = -(-sum((PEER_TOPK + 1) // (p + 1) for p in range(PEER_TOPK + 1)) // 8) * 8

NN = (((1,), (0,)), ((), ()))
NT = (((1,), (1,)), ((), ()))


def _params(*sem):
    return pltpu.CompilerParams(dimension_semantics=sem, vmem_limit_bytes=VMEM_LIMIT)


def _dot(a, b, dims=NN):
    return lax.dot_general(a, b, dims, preferred_element_type=f32)


def _split(a):
    hi = a.astype(bf16)
    return hi, (a - hi.astype(f32)).astype(bf16)


def _dot3(a, b, dims=NN):
    ah, al = _split(a)
    bh, bl = _split(b)
    return _dot(ah, bh, dims) + (_dot(ah, bl, dims) + _dot(al, bh, dims))


def _sigmoid(x):
    return 1.0 / (1.0 + jnp.exp(-x))


def _silu(x):
    return x * _sigmoid(x)


def _softplus(x):
    return jnp.maximum(x, 0.0) + jnp.log1p(jnp.exp(-jnp.abs(x)))


def _layer_norm(z, g, b):
    mu = jnp.mean(z, -1, keepdims=True)
    zc = z - mu
    var = jnp.mean(zc * zc, -1, keepdims=True)
    return zc * lax.rsqrt(var + EPS) * g + b


def _rms(x, g):
    return x * lax.rsqrt(jnp.mean(x * x, -1, keepdims=True) + EPS) * g


def _chunk_masks(d):
    row = lax.broadcasted_iota(jnp.int32, (CHUNK, CHUNK), 0)
    col = lax.broadcasted_iota(jnp.int32, (CHUNK, CHUNK), 1)
    diff = row - col if d == 0 else col - row
    return diff >= 0, diff > 0, diff <= 0, row == col


def _cum(mask_incl, mask_incl_t, x_col, x_row):
    c_col = jnp.sum(jnp.where(mask_incl, x_row, 0.0), axis=1, keepdims=True)
    c_row = jnp.sum(jnp.where(mask_incl_t, x_col, 0.0), axis=0, keepdims=True)
    tot = jnp.sum(x_col, axis=0, keepdims=True)
    return c_col, c_row, tot


def _ada_kernel(c_ref, w_ref, b_ref, o_ref):
    o_ref[...] = _dot3(_silu(c_ref[...]), w_ref[...]) + b_ref[...]


def _ada(cond, w, b):
    rows, d = cond.shape
    n = w.shape[1]
    return pl.pallas_call(
        _ada_kernel,
        out_shape=jax.ShapeDtypeStruct((rows, n), f32),
        grid=(n // d,),
        in_specs=[pl.BlockSpec((rows, d), lambda j: (0, 0)),
                  pl.BlockSpec((d, d), lambda j: (0, j)),
                  pl.BlockSpec((1, d), lambda j: (0, j))],
        out_specs=pl.BlockSpec((rows, d), lambda j: (0, j)),
        compiler_params=_params("parallel"),
    )(cond, w, b.reshape(1, n))


def _proj_kernel(x_ref, sh_ref, sc_ref, w_ref, o_ref):
    xm = x_ref[...] * (1.0 + sc_ref[0]) + sh_ref[0]
    o_ref[...] = _dot(xm.astype(bf16), w_ref[...])


def _proj(x, shift_t, scale_t, w):
    rows, d = x.shape
    n = w.shape[1]
    return pl.pallas_call(
        _proj_kernel,
        out_shape=jax.ShapeDtypeStruct((rows, n), f32),
        grid=(rows // ROW_TILE,),
        in_specs=[pl.BlockSpec((ROW_TILE, d), lambda i: (i, 0)),
                  pl.BlockSpec((1, 1, d), lambda i: (i, 0, 0)),
                  pl.BlockSpec((1, 1, d), lambda i: (i, 0, 0)),
                  pl.BlockSpec((d, n), lambda i: (0, 0))],
        out_specs=pl.BlockSpec((ROW_TILE, n), lambda i: (i, 0)),
        compiler_params=_params("parallel"),
    )(x, shift_t, scale_t, w)


def _conv_kernel(x_ref, w_ref, b_ref, o_ref, pad_ref, *, seq, seg0, n_norm, n_qscale, qscale):
    j = pl.program_id(1)
    halo = 8
    pad_ref[0:halo, :] = jnp.zeros((halo, LANE), f32)
    pad_ref[halo + seq:2 * halo + seq, :] = jnp.zeros((halo, LANE), f32)
    pad_ref[halo:halo + seq, :] = x_ref[0]
    w = w_ref[...]
    bias = b_ref[...]
    scale = jnp.where(j < n_qscale, qscale, 1.0).astype(f32)
    norm_on = jnp.where(j < n_norm, 1.0, 0.0).astype(f32)
    rows = ROW_TILE

    def body(c, carry):
        r0 = pl.multiple_of(c * rows, rows)
        win = pad_ref[pl.ds(r0, rows + 2 * halo), :]
        t = r0 + lax.broadcasted_iota(jnp.int32, (rows, LANE), 0)
        acc = bias + w[2:3] * win[halo:halo + rows]
        for k in (0, 1, 3, 4):
            off = k - CONV_W // 2
            tap = win[halo + off:halo + off + rows]
            same_segment = ((t + off) >= seg0) == (t >= seg0)
            acc = acc + w[k:k + 1] * jnp.where(same_segment, tap, 0.0)
        y = _silu(acc)
        yn = y * lax.rsqrt(jnp.sum(y * y, -1, keepdims=True) + EPS)
        y = (norm_on * yn + (1.0 - norm_on) * y) * scale
        o_ref[0, pl.ds(r0, rows), :] = y
        return carry

    lax.fori_loop(0, seq // rows, body, 0)


def _conv_act(x, w, b, col_blocks, *, seg0, n_norm, n_qscale, qscale):
    bsz, seq, _ = x.shape
    n_out = len(col_blocks)
    first_gap = next((i for i, cb in enumerate(col_blocks) if cb != i), n_out)
    gap = col_blocks[first_gap] - first_gap if first_gap < n_out else 0
    in_map = lambda bi, j: (bi, 0, jnp.where(j < first_gap, j, j + gap))
    kern = functools.partial(_conv_kernel, seq=seq, seg0=seg0, n_norm=n_norm, n_qscale=n_qscale, qscale=qscale)
    return pl.pallas_call(
        kern,
        out_shape=jax.ShapeDtypeStruct((bsz, seq, n_out * LANE), f32),
        grid=(bsz, n_out),
        in_specs=[pl.BlockSpec((1, seq, LANE), in_map),
                  pl.BlockSpec((CONV_W, LANE), lambda bi, j: (0, j)),
                  pl.BlockSpec((1, LANE), lambda bi, j: (0, j))],
        out_specs=pl.BlockSpec((1, seq, LANE), lambda bi, j: (bi, 0, j)),
        scratch_shapes=[pltpu.VMEM((seq + 16, LANE), f32)],
        compiler_params=_params("parallel", "parallel"),
    )(x, w, b)


def _chunk_index(d, n, n_ctx, n_all):
    return n if d == 0 else jnp.where(n < n_ctx, n_ctx - 1 - n, n_all + n_ctx - 1 - n)


def _both_directions(make_specs):
    return [spec for d in range(N_DIR) for spec in make_specs(d)]


def _gate_layouts(raw, heads):
    bsz, seq, _ = raw.shape
    r = raw.reshape(bsz, seq // CHUNK, CHUNK, N_DIR, heads)
    return r.transpose(3, 0, 1, 2, 4), r.transpose(3, 0, 1, 4, 2)


def _param_layouts(*ps):
    return jnp.stack(ps, axis=1).astype(f32), jnp.stack(ps, axis=2).astype(f32)


def _gdn_kernel(*refs, bsz):
    n_in = 7
    ins = [refs[d * n_in:(d + 1) * n_in] for d in range(N_DIR)]
    pcol_ref, prow_ref = refs[N_DIR * n_in:N_DIR * n_in + 2]
    o_refs = refs[N_DIR * n_in + 2:N_DIR * n_in + 2 + N_DIR]
    st_ref = refs[-1]
    n = pl.program_id(0)

    @pl.when(n == 0)
    def _():
        st_ref[...] = jnp.zeros(st_ref.shape, f32)

    masks = [_chunk_masks(d) for d in range(N_DIR)]
    eye = masks[0][3].astype(f32)
    row = lax.broadcasted_iota(jnp.int32, (CHUNK, CHUNK), 0)
    col = lax.broadcasted_iota(jnp.int32, (CHUNK, CHUNK), 1)
    blocks = [(row >> sh) == (col >> sh) for sh in (3, 4, 5, 6)]
    off_masks = [blocks[lvl + 1] & ~blocks[lvl] for lvl in range(3)]
    chains = [(d, b, h) for d in range(N_DIR) for b in range(bsz) for h in range(A_HEADS)]
    idx = range(len(chains))
    pc = [pcol_ref[d] for d in range(N_DIR)]
    pr = [prow_ref[d] for d in range(N_DIR)]
    beta_col = [[_sigmoid(ins[d][3][0, b, 0]) for b in range(bsz)] for d in range(N_DIR)]
    g_col = [[-jnp.exp(pc[d][0:1, :]) * _softplus(ins[d][5][0, b, 0] + pc[d][1:2, :]) for b in range(bsz)]
             for d in range(N_DIR)]
    g_row = [[-jnp.exp(pr[d][:, 0:1]) * _softplus(ins[d][6][0, b, 0] + pr[d][:, 1:2]) for b in range(bsz)]
             for d in range(N_DIR)]
    sl = [slice(h * A_DK, (h + 1) * A_DK) for _, _, h in chains]
    q = [ins[d][0][b, :, sl[c]] for c, (d, b, _) in enumerate(chains)]
    k = [ins[d][1][b, :, sl[c]] for c, (d, b, _) in enumerate(chains)]
    v = [ins[d][2][b, :, sl[c]] for c, (d, b, _) in enumerate(chains)]
    cums = [_cum(masks[d][0], masks[d][2], g_col[d][b][:, h:h + 1], g_row[d][b][h:h + 1, :]) for d, b, h in chains]
    decay = [jnp.exp(jnp.where(masks[d][0], cums[c][0] - cums[c][1], -jnp.inf)) for c, (d, _, _) in enumerate(chains)]
    b_col = [beta_col[d][b][:, h:h + 1] for d, b, h in chains]
    kb = [k[c] * b_col[c] for c in idx]
    l_mat = [_dot(kb[c], k[c], NT) * jnp.where(masks[d][1], decay[c], 0.0) for c, (d, _, _) in enumerate(chains)]
    nil = [-jnp.where(blocks[0], l_mat[c], 0.0) for c in idx]
    t_inv = [eye + nil[c] for c in idx]
    for _ in range(2):
        nil = [_dot(nil[c], nil[c]) for c in idx]
        t_inv = [t_inv[c] + _dot(t_inv[c], nil[c]) for c in idx]
    for lvl in range(3):
        left = [_dot(t_inv[c], jnp.where(off_masks[lvl], l_mat[c], 0.0)) for c in idx]
        t_inv = [t_inv[c] - _dot(left[c], t_inv[c]) for c in idx]
    e_col = [jnp.exp(gc) for gc, _, _ in cums]
    uw = [_dot(t_inv[c], jnp.concatenate([v[c] * b_col[c], kb[c] * e_col[c]], axis=1)) for c in idx]
    qk = [_dot(q[c], k[c], NT) * decay[c] for c in idx]
    k_dec_t = [(k[c] * jnp.exp(cums[c][2] - cums[c][0])).T for c in idx]
    s = [st_ref[c] for c in idx]
    wq = [_dot(jnp.concatenate([uw[c][:, A_DV:], q[c] * e_col[c]], axis=0), s[c]) for c in idx]
    v_new = [uw[c][:, :A_DV] - wq[c][:CHUNK] for c in idx]
    out = [wq[c][CHUNK:] + _dot(qk[c], v_new[c]) for c in idx]
    s_new = [jnp.exp(cums[c][2]) * s[c] + _dot(k_dec_t[c], v_new[c]) for c in idx]
    for c, (d, b, _) in enumerate(chains):
        o_refs[d][b, :, sl[c]] = out[c]
        st_ref[c] = s_new[c]


def _gdn(act, beta_raw, decay_raw, a_log, dt_bias, n_ctx):
    bsz, seq, _ = act.shape
    n_all = seq // CHUNK
    hw = A_HEADS * A_DK
    bcol, brow = _gate_layouts(beta_raw, A_HEADS)
    dcol, drow = _gate_layouts(decay_raw, A_HEADS)
    pcol, prow = _param_layouts(a_log, dt_bias)
    ci = functools.partial(_chunk_index, n_ctx=n_ctx, n_all=n_all)

    def specs(d):
        tok = lambda blk: pl.BlockSpec((bsz, CHUNK, hw), lambda n: (0, ci(d, n), blk))
        gcol = pl.BlockSpec((1, bsz, 1, CHUNK, A_HEADS), lambda n: (d, 0, ci(d, n), 0, 0))
        grow = pl.BlockSpec((1, bsz, 1, A_HEADS, CHUNK), lambda n: (d, 0, ci(d, n), 0, 0))
        return [tok(0), tok(1), tok(2), gcol, grow, gcol, grow]

    out = jax.ShapeDtypeStruct((bsz, seq, hw), f32)
    return pl.pallas_call(
        functools.partial(_gdn_kernel, bsz=bsz),
        out_shape=(out,) * N_DIR,
        grid=(n_all,),
        in_specs=_both_directions(specs) + [pl.BlockSpec((N_DIR, 2, A_HEADS), lambda n: (0, 0, 0)),
                                            pl.BlockSpec((N_DIR, A_HEADS, 2), lambda n: (0, 0, 0))],
        out_specs=tuple(pl.BlockSpec((bsz, CHUNK, hw), lambda n, d=d: (0, ci(d, n), 0)) for d in range(N_DIR)),
        scratch_shapes=[pltpu.VMEM((N_DIR * bsz * A_HEADS, A_DK, A_DV), f32)],
        compiler_params=_params("arbitrary"),
    )(*([act, act, act, bcol, brow, dcol, drow] * N_DIR), pcol, prow)


def _ssd_kernel(*refs, bsz):
    n_in = 5
    ins = [refs[d * n_in:(d + 1) * n_in] for d in range(N_DIR)]
    pcol_ref, prow_ref = refs[N_DIR * n_in:N_DIR * n_in + 2]
    o_refs = refs[N_DIR * n_in + 2:N_DIR * n_in + 2 + N_DIR]
    st_ref = refs[-1]
    n = pl.program_id(0)

    @pl.when(n == 0)
    def _():
        st_ref[...] = jnp.zeros(st_ref.shape, f32)

    masks = [_chunk_masks(d) for d in range(N_DIR)]
    rep = B_HEADS // B_GROUPS
    streams = [(d, b) for d in range(N_DIR) for b in range(bsz)]
    groups = [(s, g) for s in range(len(streams)) for g in range(B_GROUPS)]
    heads = [(s, h) for s in range(len(streams)) for h in range(B_HEADS)]
    gsl = [slice(g * B_STATE, (g + 1) * B_STATE) for _, g in groups]
    hsl = [slice(h * B_HEADDIM, (h + 1) * B_HEADDIM) for _, h in heads]
    pc = [pcol_ref[d] for d, _ in streams]
    pr = [prow_ref[d] for d, _ in streams]
    dt_col = [_softplus(ins[d][3][0, b, 0] + pc[s][1:2, :]) for s, (d, b) in enumerate(streams)]
    dt_row = [_softplus(ins[d][4][0, b, 0] + pr[s][:, 1:2]) for s, (d, b) in enumerate(streams)]
    da_col = [dt_col[s] * (-jnp.exp(pc[s][0:1, :])) for s in range(len(streams))]
    da_row = [dt_row[s] * (-jnp.exp(pr[s][:, 0:1])) for s in range(len(streams))]
    bm = [ins[streams[s][0]][1][streams[s][1], :, gsl[i]] for i, (s, _) in enumerate(groups)]
    cm = [ins[streams[s][0]][2][streams[s][1], :, gsl[i]] for i, (s, _) in enumerate(groups)]
    hs = [st_ref[i] for i in range(len(groups))]
    cb = [_dot(cm[i], bm[i], NT) for i in range(len(groups))]
    y_off = [_dot(cm[i], hs[i]) for i in range(len(groups))]
    bm_t = [bm[i].T for i in range(len(groups))]
    m_incl = [masks[streams[s][0]][0] for s, _ in heads]
    m_incl_t = [masks[streams[s][0]][2] for s, _ in heads]
    cums = [_cum(m_incl[j], m_incl_t[j], da_col[s][:, h:h + 1], da_row[s][h:h + 1, :]) for j, (s, h) in enumerate(heads)]
    seg = [jnp.exp(jnp.where(m_incl[j], cums[j][0] - cums[j][1], -jnp.inf)) for j in range(len(heads))]
    xdt = [ins[streams[s][0]][0][streams[s][1], :, hsl[j]] * dt_col[s][:, h:h + 1] for j, (s, h) in enumerate(heads)]
    y_diag = [_dot(cb[j // rep] * seg[j], xdt[j]) for j in range(len(heads))]
    for j, (s, h) in enumerate(heads):
        r = h % rep
        d, b = streams[s]
        o_refs[d][b, :, hsl[j]] = y_diag[j] + jnp.exp(cums[j][0]) * y_off[j // rep][:, r * B_HEADDIM:(r + 1) * B_HEADDIM]
    xdt_end = [xdt[j] * jnp.exp(cums[j][2] - cums[j][0]) for j in range(len(heads))]
    keep = [jnp.broadcast_to(jnp.exp(cums[j][2]), (1, B_HEADDIM)) for j in range(len(heads))]
    upd = [_dot(bm_t[i], jnp.concatenate(xdt_end[i * rep:(i + 1) * rep], axis=1)) for i in range(len(groups))]
    for i in range(len(groups)):
        st_ref[i] = jnp.concatenate(keep[i * rep:(i + 1) * rep], axis=1) * hs[i] + upd[i]


def _ssd(act, dt_raw, a_log, dt_bias, n_ctx):
    bsz, seq, _ = act.shape
    n_all = seq // CHUNK
    dinner = B_HEADS * B_HEADDIM
    gw = B_GROUPS * B_STATE
    tcol, trow = _gate_layouts(dt_raw, B_HEADS)
    pcol, prow = _param_layouts(a_log, dt_bias)
    ci = functools.partial(_chunk_index, n_ctx=n_ctx, n_all=n_all)
    x_off = (A_HEADS * (2 * A_DK + A_DV)) // dinner
    bm_off = (A_HEADS * (2 * A_DK + A_DV) + dinner) // gw

    def specs(d):
        return [pl.BlockSpec((bsz, CHUNK, dinner), lambda n: (0, ci(d, n), x_off)),
                pl.BlockSpec((bsz, CHUNK, gw), lambda n: (0, ci(d, n), bm_off)),
                pl.BlockSpec((bsz, CHUNK, gw), lambda n: (0, ci(d, n), bm_off + 1)),
                pl.BlockSpec((1, bsz, 1, CHUNK, B_HEADS), lambda n: (d, 0, ci(d, n), 0, 0)),
                pl.BlockSpec((1, bsz, 1, B_HEADS, CHUNK), lambda n: (d, 0, ci(d, n), 0, 0))]

    out = jax.ShapeDtypeStruct((bsz, seq, dinner), f32)
    return pl.pallas_call(
        functools.partial(_ssd_kernel, bsz=bsz),
        out_shape=(out,) * N_DIR,
        grid=(n_all,),
        in_specs=_both_directions(specs) + [pl.BlockSpec((N_DIR, 2, B_HEADS), lambda n: (0, 0, 0)),
                                            pl.BlockSpec((N_DIR, B_HEADS, 2), lambda n: (0, 0, 0))],
        out_specs=tuple(pl.BlockSpec((bsz, CHUNK, dinner), lambda n, d=d: (0, ci(d, n), 0)) for d in range(N_DIR)),
        scratch_shapes=[pltpu.VMEM((N_DIR * bsz * B_GROUPS, B_STATE, (B_HEADS // B_GROUPS) * B_HEADDIM), f32)],
        compiler_params=_params("arbitrary"),
    )(*([act, act, act, tcol, trow] * N_DIR), pcol, prow)


def _mlstm_kernel(*refs, bsz):
    n_in = 7
    ins = [refs[d * n_in:(d + 1) * n_in] for d in range(N_DIR)]
    pcol_ref, prow_ref = refs[N_DIR * n_in:N_DIR * n_in + 2]
    o_refs = refs[N_DIR * n_in + 2:N_DIR * n_in + 2 + N_DIR]
    c_ref, n_ref, m_ref = refs[-3:]
    n = pl.program_id(0)

    @pl.when(n == 0)
    def _():
        c_ref[...] = jnp.zeros(c_ref.shape, f32)
        n_ref[...] = jnp.zeros(n_ref.shape, f32)
        m_ref[...] = jnp.zeros(m_ref.shape, f32)

    masks = [_chunk_masks(d) for d in range(N_DIR)]
    streams = [(d, b) for d in range(N_DIR) for b in range(bsz)]
    chains = [(s, h) for s in range(len(streams)) for h in range(C_HEADS)]
    idx = range(len(chains))
    pc = [pcol_ref[d] for d, _ in streams]
    pr = [prow_ref[d] for d, _ in streams]
    li_col = [ins[d][3][0, b, 0] + pc[s][0:1, :] for s, (d, b) in enumerate(streams)]
    li_row = [ins[d][4][0, b, 0] + pr[s][:, 0:1] for s, (d, b) in enumerate(streams)]
    lf_col = [-_softplus(-(ins[d][5][0, b, 0] + pc[s][1:2, :])) for s, (d, b) in enumerate(streams)]
    lf_row = [-_softplus(-(ins[d][6][0, b, 0] + pr[s][:, 1:2])) for s, (d, b) in enumerate(streams)]
    q = [ins[streams[s][0]][0][streams[s][1], :, h * C_DK:(h + 1) * C_DK] for s, h in chains]
    k = [ins[streams[s][0]][1][streams[s][1], :, h * C_DK:(h + 1) * C_DK] for s, h in chains]
    v = [ins[streams[s][0]][2][streams[s][1], :, h * C_DV:(h + 1) * C_DV] for s, h in chains]
    m_incl = [masks[streams[s][0]][0] for s, _ in chains]
    m_incl_t = [masks[streams[s][0]][2] for s, _ in chains]
    cums = [_cum(m_incl[c], m_incl_t[c], lf_col[s][:, h:h + 1], lf_row[s][h:h + 1, :]) for c, (s, h) in enumerate(chains)]
    logw_intra = [jnp.where(m_incl[c], cums[c][0] - cums[c][1] + li_row[s][h:h + 1, :], -jnp.inf)
                  for c, (s, h) in enumerate(chains)]
    m_prev = [m_ref[c] for c in idx]
    logw_inter = [cums[c][0] + m_prev[c] for c in idx]
    m_t = [jnp.maximum(logw_inter[c], jnp.max(logw_intra[c], axis=1, keepdims=True)) for c in idx]
    qk = [_dot(q[c], k[c], NT) for c in idx]
    c_mem = [c_ref[c] for c in idx]
    n_mem = [n_ref[c] for c in idx]
    qc = [_dot(q[c], c_mem[c]) for c in idx]
    s = [qk[c] * jnp.exp(logw_intra[c] - m_t[c]) for c in idx]
    w_inter = [jnp.exp(logw_inter[c] - m_t[c]) for c in idx]
    num = [_dot(s[c], v[c]) + w_inter[c] * qc[c] for c in idx]
    den = [jnp.sum(s[c], axis=1, keepdims=True) + w_inter[c] * jnp.sum(q[c] * n_mem[c], axis=1, keepdims=True)
           for c in idx]
    for c, (s, h) in enumerate(chains):
        d, b = streams[s]
        o_refs[d][b, :, h * C_DV:(h + 1) * C_DV] = num[c] / jnp.maximum(jnp.abs(den[c]), jnp.exp(-m_t[c]))
    logw_end = [cums[c][2] - cums[c][0] + li_col[s][:, h:h + 1] for c, (s, h) in enumerate(chains)]
    m_new = [jnp.maximum(cums[c][2] + m_prev[c], jnp.max(logw_end[c], axis=0, keepdims=True)) for c in idx]
    kw = [k[c] * jnp.exp(logw_end[c] - m_new[c]) for c in idx]
    keep = [jnp.exp(cums[c][2] + m_prev[c] - m_new[c]) for c in idx]
    kv = [_dot(kw[c].T, v[c]) for c in idx]
    for c in idx:
        c_ref[c] = keep[c] * c_mem[c] + kv[c]
        n_ref[c] = keep[c] * n_mem[c] + jnp.sum(kw[c], axis=0, keepdims=True)
        m_ref[c] = m_new[c]


def _mlstm(act, proj, i_raw, f_raw, i_bias, f_bias, n_ctx):
    bsz, seq, _ = act.shape
    n_all = seq // CHUNK
    qw = C_HEADS * C_DK
    vw = C_HEADS * C_DV
    icol, irow = _gate_layouts(i_raw, C_HEADS)
    fcol, frow = _gate_layouts(f_raw, C_HEADS)
    pcol, prow = _param_layouts(i_bias, f_bias)
    ci = functools.partial(_chunk_index, n_ctx=n_ctx, n_all=n_all)

    def specs(d):
        gcol = pl.BlockSpec((1, bsz, 1, CHUNK, C_HEADS), lambda n: (d, 0, ci(d, n), 0, 0))
        grow = pl.BlockSpec((1, bsz, 1, C_HEADS, CHUNK), lambda n: (d, 0, ci(d, n), 0, 0))
        return [pl.BlockSpec((bsz, CHUNK, qw), lambda n: (0, ci(d, n), 0)),
                pl.BlockSpec((bsz, CHUNK, qw), lambda n: (0, ci(d, n), 1)),
                pl.BlockSpec((bsz, CHUNK, vw), lambda n: (0, ci(d, n), 2 * qw // vw)),
                gcol, grow, gcol, grow]

    out = jax.ShapeDtypeStruct((bsz, seq, vw), f32)
    n_chain = N_DIR * bsz * C_HEADS
    return pl.pallas_call(
        functools.partial(_mlstm_kernel, bsz=bsz),
        out_shape=(out,) * N_DIR,
        grid=(n_all,),
        in_specs=_both_directions(specs) + [pl.BlockSpec((N_DIR, 2, C_HEADS), lambda n: (0, 0, 0)),
                                            pl.BlockSpec((N_DIR, C_HEADS, 2), lambda n: (0, 0, 0))],
        out_specs=tuple(pl.BlockSpec((bsz, CHUNK, vw), lambda n, d=d: (0, ci(d, n), 0)) for d in range(N_DIR)),
        scratch_shapes=[pltpu.VMEM((n_chain, C_DK, C_DV), f32),
                        pltpu.VMEM((n_chain, 1, C_DK), f32),
                        pltpu.VMEM((n_chain, 1, 1), f32)],
        compiler_params=_params("arbitrary"),
    )(*([act, act, proj, icol, irow, fcol, frow] * N_DIR), pcol, prow)


def _merge_even_kernel(x_ref, af_ref, ar_ref, bf_ref, br_ref, za_ref, zb_ref, xs_ref, na_ref, nb_ref, dsk_ref,
                       w_ref, g_ref, lg_ref, lb_ref, o_ref):
    a = af_ref[...] + ar_ref[...]
    za = za_ref[...]
    na = na_ref[...]
    parts = []
    for h in range(A_HEADS):
        sl = slice(h * A_DV, (h + 1) * A_DV)
        parts.append(_rms(a[:, sl], na) * _silu(za[:, sl]))
    yb = (bf_ref[...] + br_ref[...] + dsk_ref[...] * xs_ref[...]) * _silu(zb_ref[...])
    gw = (B_HEADS * B_HEADDIM) // B_GROUPS
    nb = nb_ref[...]
    for g in range(B_GROUPS):
        sl = slice(g * gw, (g + 1) * gw)
        parts.append(_rms(yb[:, sl], nb[:, sl]))
    y = jnp.concatenate(parts, axis=1).astype(bf16)
    z = ALPHA * x_ref[...] + g_ref[0] * _dot(y, w_ref[...])
    o_ref[...] = _layer_norm(z, lg_ref[...], lb_ref[...])


def _merge_even(x, o_a, o_b, proj, act, norm_a, norm_b, d_skip, w_out, gate_t, ln_g, ln_b):
    rows, d = x.shape
    aw = A_HEADS * A_DV
    bw = B_HEADS * B_HEADDIM
    row = lambda w_, blk: pl.BlockSpec((ROW_TILE, w_), lambda i: (i, blk))
    vec = lambda w_: pl.BlockSpec((1, w_), lambda i: (0, 0))
    qkv = A_HEADS * (2 * A_DK + A_DV)
    return pl.pallas_call(
        _merge_even_kernel,
        out_shape=jax.ShapeDtypeStruct((rows, d), f32),
        grid=(rows // ROW_TILE,),
        in_specs=[row(d, 0), row(aw, 0), row(aw, 0), row(bw, 0), row(bw, 0),
                  row(aw, qkv // aw), row(bw, (qkv + aw) // bw), row(bw, qkv // bw),
                  vec(A_DV), vec(bw), vec(bw),
                  pl.BlockSpec((aw + bw, d), lambda i: (0, 0)),
                  pl.BlockSpec((1, 1, d), lambda i: (i, 0, 0)),
                  vec(d), vec(d)],
        out_specs=row(d, 0),
        compiler_params=_params("parallel"),
    )(x, *o_a, *o_b, proj, proj, act, norm_a.reshape(1, -1), norm_b.reshape(1, -1),
      jnp.repeat(d_skip, B_HEADDIM).reshape(1, -1), w_out, gate_t, ln_g.reshape(1, -1), ln_b.reshape(1, -1))


def _merge_odd_kernel(hf_ref, hr_ref, o_ref_in, nc_ref, w_ref, out_ref):
    hsum = hf_ref[...] + hr_ref[...]
    o = o_ref_in[...]
    nc = nc_ref[...]
    parts = []
    for h in range(C_HEADS):
        sl = slice(h * C_DV, (h + 1) * C_DV)
        parts.append(_rms(hsum[:, sl], nc) * _sigmoid(o[:, sl]))
    y = jnp.concatenate(parts, axis=1).astype(bf16)
    out_ref[...] = _dot(y, w_ref[...])


def _merge_odd(h, proj, norm_c, w_out, tiles_per_batch, ctx_tiles, bsz):
    vw = C_HEADS * C_DV
    lat_tiles = tiles_per_batch - ctx_tiles
    src = lambda i: i + ctx_tiles * (i // lat_tiles + 1)
    d = w_out.shape[1]
    return pl.pallas_call(
        _merge_odd_kernel,
        out_shape=jax.ShapeDtypeStruct((bsz * lat_tiles * ROW_TILE, d), f32),
        grid=(bsz * lat_tiles,),
        in_specs=[pl.BlockSpec((ROW_TILE, vw), lambda i: (src(i), 0)),
                  pl.BlockSpec((ROW_TILE, vw), lambda i: (src(i), 0)),
                  pl.BlockSpec((ROW_TILE, vw), lambda i: (src(i), 2 * C_HEADS * C_DK // vw + 1)),
                  pl.BlockSpec((1, C_DV), lambda i: (0, 0)),
                  pl.BlockSpec((vw, d), lambda i: (0, 0))],
        out_specs=pl.BlockSpec((ROW_TILE, d), lambda i: (i, 0)),
        compiler_params=_params("parallel"),
    )(*h, proj, norm_c.reshape(1, -1), w_out)


def _resid_ln_kernel(x_ref, y_ref, g_ref, lg_ref, lb_ref, o_ref, *, transposed):
    y = y_ref[...]
    if transposed:
        y = y.T
    o_ref[...] = _layer_norm(ALPHA * x_ref[...] + g_ref[0] * y, lg_ref[...], lb_ref[...])


def _resid_ln(x, y, gate_t, ln_g, ln_b, *, transposed):
    rows, d = x.shape
    y_spec = (pl.BlockSpec((d, ROW_TILE), lambda i: (0, i)) if transposed
              else pl.BlockSpec((ROW_TILE, d), lambda i: (i, 0)))
    return pl.pallas_call(
        functools.partial(_resid_ln_kernel, transposed=transposed),
        out_shape=jax.ShapeDtypeStruct((rows, d), f32),
        grid=(rows // ROW_TILE,),
        in_specs=[pl.BlockSpec((ROW_TILE, d), lambda i: (i, 0)), y_spec,
                  pl.BlockSpec((1, 1, d), lambda i: (i, 0, 0)),
                  pl.BlockSpec((1, d), lambda i: (0, 0)),
                  pl.BlockSpec((1, d), lambda i: (0, 0))],
        out_specs=pl.BlockSpec((ROW_TILE, d), lambda i: (i, 0)),
        compiler_params=_params("parallel"),
    )(x, y, gate_t, ln_g.reshape(1, -1), ln_b.reshape(1, -1))


def _oddeven_merge(lo, hi, r):
    step = r * 2
    if step < hi - lo:
        yield from _oddeven_merge(lo, hi, step)
        yield from _oddeven_merge(lo + r, hi, step)
        yield from [(i, i + r) for i in range(lo + r, hi - r, step)]
    else:
        yield (lo, lo + r)


def _oddeven_sort(lo, hi):
    if hi - lo >= 1:
        mid = lo + (hi - lo) // 2
        yield from _oddeven_sort(lo, mid)
        yield from _oddeven_sort(mid + 1, hi)
        yield from _oddeven_merge(lo, hi, 1)


def _exchange(a, i, j):
    a[i], a[j] = jnp.maximum(a[i], a[j]), jnp.minimum(a[i], a[j])


def _top_sorted(slabs):
    a = list(slabs)
    for i, j in _oddeven_sort(0, len(a) - 1):
        _exchange(a, i, j)
    for shift in (4, 2, 1):
        other = [pltpu.roll(x, shift, axis=0) for x in a]
        if len(a) < PEER_TOPK:
            a = a + other[::-1]
        else:
            a = [jnp.maximum(a[k], other[PEER_TOPK - 1 - k]) for k in range(PEER_TOPK)]
        dist = PEER_TOPK // 2
        while dist >= 1:
            for i in range(PEER_TOPK):
                if i & dist == 0:
                    _exchange(a, i, i + dist)
            dist //= 2
    return a


def _next_largest(slabs, kth):
    count = sum(jnp.where(s >= kth, 1.0, 0.0) for s in slabs)
    below = functools.reduce(jnp.maximum, [jnp.where(s < kth, s, -jnp.inf) for s in slabs])
    count = jnp.sum(count, axis=0, keepdims=True)
    below = jnp.max(below, axis=0, keepdims=True)
    return jnp.where(count > PEER_TOPK, kth, below)


def _top_values(s):
    slabs = [s[i:i + 8] for i in range(0, s.shape[0], 8)]
    if len(slabs) < 8:
        slabs = slabs + [jnp.full_like(slabs[0], -jnp.inf)] * (8 - len(slabs))
    top = [t[0:1] for t in _top_sorted(slabs)]
    return top + [_next_largest(slabs, top[-1])]


def _route_kernel(x_ref, sh_ref, sc_ref, w_ref, k_ref, xm_ref, thr_ref, s2_ref, e1_ref, e2_ref, sc_scr, cand_ref):
    for t in range(ROUTE_TILE // ROW_TILE):
        rows = slice(t * ROW_TILE, (t + 1) * ROW_TILE)
        xm_ref[rows, :] = (x_ref[rows, :] * (1.0 + sc_ref[t]) + sh_ref[t]).astype(bf16)
    half = PEER_DKEY // 2
    n_top = PEER_TOPK + 1
    pairs = [(p, r) for p in range(n_top) for r in range(n_top) if (p + 1) * (r + 1) <= n_top]
    cand_ref[...] = jnp.full(cand_ref.shape, -jnp.inf, f32)

    def scores(h, sc_scr):
        wsl = pl.ds(pl.multiple_of(h * PEER_DKEY, PEER_DKEY), PEER_DKEY)
        q = _dot(xm_ref[...], w_ref[:, wsl])
        for c in range(2):
            sc_scr[c] = _dot(k_ref[h, c], q[:, c * half:(c + 1) * half].astype(bf16), NT)

    def select(h, sc_scr):
        for lb in range(ROUTE_TILE // LANE):
            lanes = slice(lb * LANE, (lb + 1) * LANE)
            s1 = sc_scr[0, :, lanes]
            s2 = sc_scr[1, :, lanes]
            t1 = _top_values(s1)
            t2 = _top_values(s2)
            for i, (p, r) in enumerate(pairs):
                cand_ref[i:i + 1, lanes] = t1[p] + t2[r]
            best = _top_values(cand_ref[:, lanes])
            z = sum(jnp.exp(bv - best[0]) for bv in best[:PEER_TOPK])
            tau = 0.5 * (best[PEER_TOPK - 1] + best[PEER_TOPK])
            thr_ref[h, :, lanes] = tau - s1
            s2_ref[h, :, lanes] = s2
            e1_ref[h, :, lanes] = jnp.exp(s1 - t1[0]) / z
            e2_ref[h, :, lanes] = jnp.exp(s2 - t2[0])

    def head(h, carry):
        scores(h, sc_scr)
        select(h, sc_scr)
        return carry

    lax.fori_loop(0, PEER_HEADS, head, 0)


def _route(x, shift_t, scale_t, wq, keys):
    rows, d = x.shape
    nq = wq.shape[1]
    half = PEER_DKEY // 2
    score = jax.ShapeDtypeStruct((PEER_HEADS, PEER_NKEYS, rows), f32)
    score_spec = pl.BlockSpec((PEER_HEADS, PEER_NKEYS, ROUTE_TILE), lambda i: (0, 0, i))
    key_spec = pl.BlockSpec((PEER_HEADS, 2, PEER_NKEYS, half), lambda i: (0, 0, 0, 0))
    mods = ROUTE_TILE // ROW_TILE
    return pl.pallas_call(
        _route_kernel,
        out_shape=(jax.ShapeDtypeStruct((rows, d), bf16), score, score, score, score),
        grid=(rows // ROUTE_TILE,),
        in_specs=[pl.BlockSpec((ROUTE_TILE, d), lambda i: (i, 0)),
                  pl.BlockSpec((mods, 1, d), lambda i: (i, 0, 0)),
                  pl.BlockSpec((mods, 1, d), lambda i: (i, 0, 0)),
                  pl.BlockSpec((d, nq), lambda i: (0, 0)),
                  key_spec],
        out_specs=(pl.BlockSpec((ROUTE_TILE, d), lambda i: (i, 0)),
                   score_spec, score_spec, score_spec, score_spec),
        scratch_shapes=[pltpu.VMEM((2, PEER_NKEYS, ROUTE_TILE), f32),
                        pltpu.VMEM((N_CAND_ROWS, ROUTE_TILE), f32)],
        compiler_params=_params("parallel"),
    )(x, shift_t, scale_t, wq, keys)


def _expert_kernel(xm_ref, u_ref, vt_ref, thr_ref, e1_ref, s2_ref, e2_ref, o_ref, a_scr, w_scr):
    j = pl.program_id(1)

    @pl.when(j == 0)
    def _():
        o_ref[...] = jnp.zeros(o_ref.shape, f32)

    a_scr[...] = _dot(u_ref[...], xm_ref[...], NT)
    sqrt_half = math.sqrt(0.5)
    row_chunks = PEER_NKEYS // GATE_ROWS
    tiles_per_group = (EXPERT_TOK // LANE) * row_chunks

    def tile(r, carry, i1_0):
        lanes = pl.ds(pl.multiple_of((r // row_chunks) * LANE, LANE), LANE)
        r0 = pl.multiple_of((r % row_chunks) * GATE_ROWS, GATE_ROWS)
        i2 = pl.ds(r0, GATE_ROWS)
        gate = [jnp.zeros((GATE_ROWS, LANE), f32) for _ in range(I1_TILE)]
        for h in range(PEER_HEADS):
            s2 = s2_ref[h, i2, lanes]
            e2 = e2_ref[h, i2, lanes]
            for il in range(I1_TILE):
                i1 = i1_0 + il
                w = e1_ref[h, i1:i1 + 1, lanes] * e2
                gate[il] = gate[il] + jnp.where(s2 >= thr_ref[h, i1:i1 + 1, lanes], w, 0.0)
        for il in range(I1_TILE):
            rows = pl.ds((i1_0 + il) * PEER_NKEYS + r0, GATE_ROWS)
            a = a_scr[rows, lanes]
            act = 0.5 * a * (1.0 + lax.erf(a * sqrt_half))
            w_scr[rows, lanes] = (gate[il] * act).astype(bf16)
        return carry

    for i1_0 in range(0, EXPERT_BLK // PEER_NKEYS, I1_TILE):
        lax.fori_loop(0, tiles_per_group, functools.partial(tile, i1_0=i1_0), 0)
    o_ref[...] += _dot(vt_ref[...], w_scr[...])


def _experts(xm, u_bf, vt_bf, thr, e1, s2, e2):
    rows, d = xm.shape
    n_exp = u_bf.shape[0]
    i1_per_blk = EXPERT_BLK // PEER_NKEYS
    sel = pl.BlockSpec((PEER_HEADS, i1_per_blk, EXPERT_TOK), lambda i, j: (0, j, i))
    full = pl.BlockSpec((PEER_HEADS, PEER_NKEYS, EXPERT_TOK), lambda i, j: (0, 0, i))
    return pl.pallas_call(
        _expert_kernel,
        out_shape=jax.ShapeDtypeStruct((d, rows), f32),
        grid=(rows // EXPERT_TOK, n_exp // EXPERT_BLK),
        in_specs=[pl.BlockSpec((EXPERT_TOK, d), lambda i, j: (i, 0)),
                  pl.BlockSpec((EXPERT_BLK, d), lambda i, j: (j, 0)),
                  pl.BlockSpec((d, EXPERT_BLK), lambda i, j: (0, j)),
                  sel, sel, full, full],
        out_specs=pl.BlockSpec((d, EXPERT_TOK), lambda i, j: (0, i)),
        scratch_shapes=[pltpu.VMEM((EXPERT_BLK, EXPERT_TOK), f32),
                        pltpu.VMEM((EXPERT_BLK, EXPERT_TOK), bf16)],
        compiler_params=_params("parallel", "arbitrary"),
    )(xm, u_bf, vt_bf, thr, e1, s2, e2)


def _peer_residual(x, shift_t, scale_t, gate_t, peer_q, peer_keys, peer_u, peer_v, ln_g, ln_b):
    xm, thr, s2, e1, e2 = _route(x, shift_t, scale_t, peer_q.astype(bf16), peer_keys.astype(bf16))
    y_t = _experts(xm, peer_u.astype(bf16), peer_v.T.astype(bf16), thr, e1, s2, e2)
    return _resid_ln(x, y_t, gate_t, ln_g, ln_b, transposed=True)


def _grid_sincos(n_tokens, d):
    n_rows = n_tokens // GRID_W
    n_freq = d // 4
    freq = jnp.exp(-math.log(POS_BASE) * jnp.arange(n_freq, dtype=f32) / n_freq)[None, :]
    row = jnp.arange(n_rows, dtype=f32)[:, None] * freq
    col = jnp.arange(GRID_W, dtype=f32)[:, None] * freq
    row_part = jnp.concatenate([jnp.sin(row), jnp.cos(row)], -1)[:, None, :]
    col_part = jnp.concatenate([jnp.sin(col), jnp.cos(col)], -1)[None, :, :]
    shape = (n_rows, GRID_W, 2 * n_freq)
    table = jnp.concatenate([jnp.broadcast_to(row_part, shape), jnp.broadcast_to(col_part, shape)], -1)
    return table.reshape(n_tokens, d)


def _tile_rows(mod, src):
    d = mod.shape[1] // 6
    rows = jnp.concatenate([jnp.broadcast_to(mod[r:r + 1], (n, mod.shape[1])) for r, n in src], axis=0)
    return [rows[:, None, i * d:(i + 1) * d] for i in range(6)]


def _pad_cols(w, n):
    return jnp.pad(w, ((0, 0), (0, n - w.shape[1])))


def kernel(x, c, ctx, c_ctx, l0_ada_w, l0_ada_b, l0_ln1_g, l0_ln1_b, l0_w_in, l0_conv_a, l0_a_log_a, l0_dt_bias_a, l0_norm_a, l0_conv_b, l0_conv_b_bias, l0_a_log_b, l0_dt_bias_b, l0_d_skip_b, l0_norm_b, l0_w_out, l0_ln2_g, l0_ln2_b, l0_peer_q, l0_peer_keys, l0_peer_u, l0_peer_v, l1_ada_w, l1_ada_b, l1_ln1_g, l1_ln1_b, l1_w_in, l1_conv_c, l1_i_bias, l1_f_bias, l1_norm_c, l1_w_out, l1_ln2_g, l1_ln2_b, l1_peer_q, l1_peer_keys, l1_peer_u, l1_peer_v):
    bsz, n_lat, d = x.shape
    n_ctx_tok = ctx.shape[1]
    rows_grid = n_lat // GRID_W
    seq = n_ctx_tok + n_lat
    n_ctx = n_ctx_tok // CHUNK
    tiles_b = seq // ROW_TILE
    ctx_tiles = n_ctx_tok // ROW_TILE
    lat_tiles = n_lat // ROW_TILE

    cond = jnp.concatenate([c, c_ctx[None, :], jnp.zeros((8 - bsz - 1, d), f32)], axis=0)
    src_all = [run for b in range(bsz) for run in ((bsz, ctx_tiles), (b, lat_tiles))]

    x_lat = x + _grid_sincos(n_lat, d).astype(x.dtype)
    xs_all = jnp.concatenate([ctx, x_lat], axis=1).reshape(bsz * seq, d)

    sh1, sc1, g1, sh2, sc2, g2 = _tile_rows(_ada(cond, l0_ada_w, l0_ada_b), src_all)
    qkv_w = A_HEADS * (2 * A_DK + A_DV)
    az = A_HEADS * A_DV
    ag = N_DIR * A_HEADS
    bd = B_HEADS * B_HEADDIM
    bx = bd + 2 * B_GROUPS * B_STATE
    bt = N_DIR * B_HEADS
    o0 = np.cumsum([0, qkv_w, az, ag, ag, bd, bx, bt])
    w_qkv, w_za, w_beta, w_dec, w_zb, w_xbc, w_dt = (l0_w_in[:, o0[i]:o0[i + 1]] for i in range(7))
    main_w = qkv_w + az + bd + bx
    w0 = _pad_cols(jnp.concatenate([w_qkv, w_za, w_zb, w_xbc, w_beta, w_dec, w_dt], axis=1), main_w + LANE)
    proj = _proj(xs_all, sh1, sc1, w0.astype(bf16)).reshape(bsz, seq, main_w + LANE)
    conv_w = jnp.concatenate([l0_conv_a, l0_conv_b], axis=1)
    conv_b = jnp.concatenate([jnp.zeros((qkv_w,), f32), l0_conv_b_bias]).reshape(1, -1)
    xbc_blk = (qkv_w + az + bd) // LANE
    col_blocks = list(range(qkv_w // LANE)) + list(range(xbc_blk, xbc_blk + bx // LANE))
    act = _conv_act(proj, conv_w, conv_b, col_blocks, seg0=n_ctx_tok,
                    n_norm=2 * A_HEADS, n_qscale=A_HEADS, qscale=A_DK ** -0.5)
    gates = proj[:, :, main_w:]
    o_a = _gdn(act, gates[..., :ag], gates[..., ag:2 * ag], l0_a_log_a, l0_dt_bias_a, n_ctx)
    o_b = _ssd(act, gates[..., 2 * ag:2 * ag + bt], l0_a_log_b, l0_dt_bias_b, n_ctx)
    x1 = _merge_even(xs_all, [o.reshape(bsz * seq, az) for o in o_a], [o.reshape(bsz * seq, bd) for o in o_b],
                     proj.reshape(bsz * seq, -1), act.reshape(bsz * seq, -1),
                     l0_norm_a, l0_norm_b, l0_d_skip_b, l0_w_out.astype(bf16), g1, l0_ln1_g, l0_ln1_b)
    x2 = _peer_residual(x1, sh2, sc2, g2, l0_peer_q, l0_peer_keys, l0_peer_u, l0_peer_v, l0_ln2_g, l0_ln2_b)

    sh1, sc1, g1, sh2, sc2, g2 = _tile_rows(_ada(cond, l1_ada_w, l1_ada_b), src_all)
    x2 = x2.reshape(bsz, seq, d)
    x2_ctx, x2_lat = x2[:, :n_ctx_tok], x2[:, n_ctx_tok:]
    lat_cm = x2_lat.reshape(bsz, rows_grid, GRID_W, d).transpose(0, 2, 1, 3).reshape(bsz, n_lat, d)
    xs1 = jnp.concatenate([x2_ctx, lat_cm], axis=1).reshape(bsz * seq, d)
    qk_w = 2 * C_HEADS * C_DK
    vw = C_HEADS * C_DV
    w1 = _pad_cols(l1_w_in, qk_w + 2 * vw + LANE)
    proj1 = _proj(xs1, sh1, sc1, w1.astype(bf16)).reshape(bsz, seq, -1)
    act1 = _conv_act(proj1, l1_conv_c, jnp.zeros((1, qk_w), f32), list(range(qk_w // LANE)), seg0=n_ctx_tok,
                     n_norm=0, n_qscale=C_HEADS, qscale=C_DK ** -0.5)
    gates1 = proj1[:, :, qk_w + 2 * vw:]
    cg = N_DIR * C_HEADS
    h1 = _mlstm(act1, proj1, gates1[..., :cg], gates1[..., cg:2 * cg], l1_i_bias, l1_f_bias, n_ctx)
    y_cm = _merge_odd([h.reshape(bsz * seq, vw) for h in h1], proj1.reshape(bsz * seq, -1), l1_norm_c,
                      l1_w_out.astype(bf16), tiles_b, ctx_tiles, bsz)
    y_lat = y_cm.reshape(bsz, GRID_W, rows_grid, d).transpose(0, 2, 1, 3).reshape(bsz * n_lat, d)
    lat_sel = lambda t: t.reshape(bsz, tiles_b, 1, d)[:, ctx_tiles:].reshape(bsz * lat_tiles, 1, d)
    x3 = _resid_ln(x2_lat.reshape(bsz * n_lat, d), y_lat, lat_sel(g1), l1_ln1_g, l1_ln1_b, transposed=False)
    x4 = _peer_residual(x3, lat_sel(sh2), lat_sel(sc2), lat_sel(g2), l1_peer_q, l1_peer_keys, l1_peer_u,
                        l1_peer_v, l1_ln2_g, l1_ln2_b)
    return x4.reshape(bsz, n_lat, d)
```

```python
import functools
import math
from typing import Callable, NamedTuple

import numpy as np
import jax
import jax.numpy as jnp
from jax import lax
from jax.experimental import pallas as pl
from jax.experimental.pallas import tpu as pltpu

f32 = jnp.float32
bf16 = jnp.bfloat16

D_MODEL = 1024
GRID_W = 64
CHUNK = 64
CONV_W = 5
POS_BASE = 10000.0
EPS = 1e-6
N_DIR = 2
DEPTH = 2
ALPHA = (2 * DEPTH) ** 0.25

A_HEADS, A_DK, A_DV = 4, 128, 128
B_HEADS, B_HEADDIM, B_GROUPS, B_STATE = 8, 64, 2, 128
C_HEADS, C_DK, C_DV = 4, 128, 256
PEER_HEADS, PEER_NKEYS, PEER_DKEY, PEER_TOPK = 8, 128, 256, 16

LANE = 128
ROW_TILE = 256
ROUTE_TILE = 512
EXPERT_TOK = 512
EXPERT_BLK = 2048
I1_TILE = 8
GATE_ROWS = 32
VMEM_LIMIT = 56 * 1024 * 1024
N_CAND_ROWS = -(-sum((PEER_TOPK + 1) // (p + 1) for p in range(PEER_TOPK + 1)) // 8) * 8

NN = (((1,), (0,)), ((), ()))
NT = (((1,), (1,)), ((), ()))


def _params(*sem):
    return pltpu.CompilerParams(dimension_semantics=sem, vmem_limit_bytes=VMEM_LIMIT)


def _dot(a, b, dims=NN):
    return lax.dot_general(a, b, dims, preferred_element_type=f32)


def _split(a):
    hi = a.astype(bf16)
    return hi, (a - hi.astype(f32)).astype(bf16)


def _dot3(a, b, dims=NN):
    ah, al = _split(a)
    bh, bl = _split(b)
    return _dot(ah, bh, dims) + (_dot(ah, bl, dims) + _dot(al, bh, dims))


def _sigmoid(x):
    return 1.0 / (1.0 + jnp.exp(-x))


def _silu(x):
    return x * _sigmoid(x)


def _softplus(x):
    return jnp.maximum(x, 0.0) + jnp.log1p(jnp.exp(-jnp.abs(x)))


def _layer_norm(z, g, b):
    mu = jnp.mean(z, -1, keepdims=True)
    zc = z - mu
    var = jnp.mean(zc * zc, -1, keepdims=True)
    return zc * lax.rsqrt(var + EPS) * g + b


def _rms(x, g):
    return x * lax.rsqrt(jnp.mean(x * x, -1, keepdims=True) + EPS) * g


def _chunk_masks(d):
    row = lax.broadcasted_iota(jnp.int32, (CHUNK, CHUNK), 0)
    col = lax.broadcasted_iota(jnp.int32, (CHUNK, CHUNK), 1)
    diff = row - col if d == 0 else col - row
    return diff >= 0, diff > 0, diff <= 0, row == col


def _cum(mask_incl, mask_incl_t, x_col, x_row):
    c_col = jnp.sum(jnp.where(mask_incl, x_row, 0.0), axis=1, keepdims=True)
    c_row = jnp.sum(jnp.where(mask_incl_t, x_col, 0.0), axis=0, keepdims=True)
    tot = jnp.sum(x_col, axis=0, keepdims=True)
    return c_col, c_row, tot


def _ada_kernel(c_ref, w_ref, b_ref, o_ref):
    o_ref[...] = _dot3(_silu(c_ref[...]), w_ref[...]) + b_ref[...]


def _ada(cond, w, b):
    rows, d = cond.shape
    n = w.shape[1]
    return pl.pallas_call(
        _ada_kernel,
        out_shape=jax.ShapeDtypeStruct((rows, n), f32),
        grid=(n // d,),
        in_specs=[pl.BlockSpec((rows, d), lambda j: (0, 0)),
                  pl.BlockSpec((d, d), lambda j: (0, j)),
                  pl.BlockSpec((1, d), lambda j: (0, j))],
        out_specs=pl.BlockSpec((rows, d), lambda j: (0, j)),
        compiler_params=_params("parallel"),
    )(cond, w, b.reshape(1, n))


def _proj_kernel(x_ref, sh_ref, sc_ref, w_ref, o_ref):
    xm = x_ref[...] * (1.0 + sc_ref[0]) + sh_ref[0]
    o_ref[...] = _dot(xm.astype(bf16), w_ref[...])


def _proj(x, shift_t, scale_t, w):
    rows, d = x.shape
    n = w.shape[1]
    return pl.pallas_call(
        _proj_kernel,
        out_shape=jax.ShapeDtypeStruct((rows, n), f32),
        grid=(rows // ROW_TILE,),
        in_specs=[pl.BlockSpec((ROW_TILE, d), lambda i: (i, 0)),
                  pl.BlockSpec((1, 1, d), lambda i: (i, 0, 0)),
                  pl.BlockSpec((1, 1, d), lambda i: (i, 0, 0)),
                  pl.BlockSpec((d, n), lambda i: (0, 0))],
        out_specs=pl.BlockSpec((ROW_TILE, n), lambda i: (i, 0)),
        compiler_params=_params("parallel"),
    )(x, shift_t, scale_t, w)


def _conv_kernel(x_ref, w_ref, b_ref, o_ref, pad_ref, *, seq, seg0, n_norm, n_qscale, qscale):
    j = pl.program_id(1)
    halo = 8
    pad_ref[0:halo, :] = jnp.zeros((halo, LANE), f32)
    pad_ref[halo + seq:2 * halo + seq, :] = jnp.zeros((halo, LANE), f32)
    pad_ref[halo:halo + seq, :] = x_ref[0]
    w = w_ref[...]
    bias = b_ref[...]
    scale = jnp.where(j < n_qscale, qscale, 1.0).astype(f32)
    norm_on = jnp.where(j < n_norm, 1.0, 0.0).astype(f32)
    rows = ROW_TILE

    def body(c, carry):
        r0 = pl.multiple_of(c * rows, rows)
        win = pad_ref[pl.ds(r0, rows + 2 * halo), :]
        t = r0 + lax.broadcasted_iota(jnp.int32, (rows, LANE), 0)
        acc = bias + w[2:3] * win[halo:halo + rows]
        for k in (0, 1, 3, 4):
            off = k - CONV_W // 2
            tap = win[halo + off:halo + off + rows]
            same_segment = ((t + off) >= seg0) == (t >= seg0)
            acc = acc + w[k:k + 1] * jnp.where(same_segment, tap, 0.0)
        y = _silu(acc)
        yn = y * lax.rsqrt(jnp.sum(y * y, -1, keepdims=True) + EPS)
        y = (norm_on * yn + (1.0 - norm_on) * y) * scale
        o_ref[0, pl.ds(r0, rows), :] = y
        return carry

    lax.fori_loop(0, seq // rows, body, 0)


def _conv_act(x, w, b, col_blocks, *, seg0, n_norm, n_qscale, qscale):
    bsz, seq, _ = x.shape
    n_out = len(col_blocks)
    first_gap = next((i for i, cb in enumerate(col_blocks) if cb != i), n_out)
    gap = col_blocks[first_gap] - first_gap if first_gap < n_out else 0
    in_map = lambda bi, j: (bi, 0, jnp.where(j < first_gap, j, j + gap))
    kern = functools.partial(_conv_kernel, seq=seq, seg0=seg0, n_norm=n_norm, n_qscale=n_qscale, qscale=qscale)
    return pl.pallas_call(
        kern,
        out_shape=jax.ShapeDtypeStruct((bsz, seq, n_out * LANE), f32),
        grid=(bsz, n_out),
        in_specs=[pl.BlockSpec((1, seq, LANE), in_map),
                  pl.BlockSpec((CONV_W, LANE), lambda bi, j: (0, j)),
                  pl.BlockSpec((1, LANE), lambda bi, j: (0, j))],
        out_specs=pl.BlockSpec((1, seq, LANE), lambda bi, j: (bi, 0, j)),
        scratch_shapes=[pltpu.VMEM((seq + 16, LANE), f32)],
        compiler_params=_params("parallel", "parallel"),
    )(x, w, b)


def _chunk_index(d, n, n_ctx, n_all):
    return n if d == 0 else jnp.where(n < n_ctx, n_ctx - 1 - n, n_all + n_ctx - 1 - n)


def _both_directions(make_specs):
    return [spec for d in range(N_DIR) for spec in make_specs(d)]


class _ScanPart(NamedTuple):
    kernel: Callable
    operands: list
    in_specs: list
    out_shapes: list
    out_specs: list
    scratch_shapes: list


def _run_scans(parts, n_steps):
    def fused(*refs):
        n_in = sum(len(p.operands) for p in parts)
        n_out = sum(len(p.out_shapes) for p in parts)
        ins, outs, scr = refs[:n_in], refs[n_in:n_in + n_out], refs[n_in + n_out:]

        @pl.when(pl.program_id(0) == 0)
        def _():
            for state in scr:
                state[...] = jnp.zeros(state.shape, f32)

        for p in parts:
            p.kernel(*ins[:len(p.operands)], *outs[:len(p.out_shapes)], *scr[:len(p.scratch_shapes)])
            ins, outs, scr = ins[len(p.operands):], outs[len(p.out_shapes):], scr[len(p.scratch_shapes):]

    flat = pl.pallas_call(
        fused,
        out_shape=tuple(s for p in parts for s in p.out_shapes),
        grid=(n_steps,),
        in_specs=[s for p in parts for s in p.in_specs],
        out_specs=tuple(s for p in parts for s in p.out_specs),
        scratch_shapes=[s for p in parts for s in p.scratch_shapes],
        compiler_params=_params("arbitrary"),
    )(*[o for p in parts for o in p.operands])
    results = []
    for p in parts:
        results.append(flat[:len(p.out_shapes)])
        flat = flat[len(p.out_shapes):]
    return results


def _gate_layouts(raw, heads):
    bsz, seq, _ = raw.shape
    r = raw.reshape(bsz, seq // CHUNK, CHUNK, N_DIR, heads)
    return r.transpose(3, 0, 1, 2, 4), r.transpose(3, 0, 1, 4, 2)


def _param_layouts(*ps):
    return jnp.stack(ps, axis=1).astype(f32), jnp.stack(ps, axis=2).astype(f32)


def _gdn_kernel(*refs, bsz):
    n_in = 7
    ins = [refs[d * n_in:(d + 1) * n_in] for d in range(N_DIR)]
    pcol_ref, prow_ref = refs[N_DIR * n_in:N_DIR * n_in + 2]
    o_refs = refs[N_DIR * n_in + 2:N_DIR * n_in + 2 + N_DIR]
    st_ref = refs[-1]
    masks = [_chunk_masks(d) for d in range(N_DIR)]
    eye = masks[0][3].astype(f32)
    row = lax.broadcasted_iota(jnp.int32, (CHUNK, CHUNK), 0)
    col = lax.broadcasted_iota(jnp.int32, (CHUNK, CHUNK), 1)
    blocks = [(row >> sh) == (col >> sh) for sh in (3, 4, 5, 6)]
    off_masks = [blocks[lvl + 1] & ~blocks[lvl] for lvl in range(3)]
    chains = [(d, b, h) for d in range(N_DIR) for b in range(bsz) for h in range(A_HEADS)]
    idx = range(len(chains))
    pc = [pcol_ref[d] for d in range(N_DIR)]
    pr = [prow_ref[d] for d in range(N_DIR)]
    beta_col = [[_sigmoid(ins[d][3][0, b, 0]) for b in range(bsz)] for d in range(N_DIR)]
    g_col = [[-jnp.exp(pc[d][0:1, :]) * _softplus(ins[d][5][0, b, 0] + pc[d][1:2, :]) for b in range(bsz)]
             for d in range(N_DIR)]
    g_row = [[-jnp.exp(pr[d][:, 0:1]) * _softplus(ins[d][6][0, b, 0] + pr[d][:, 1:2]) for b in range(bsz)]
             for d in range(N_DIR)]
    sl = [slice(h * A_DK, (h + 1) * A_DK) for _, _, h in chains]
    q = [ins[d][0][b, :, sl[c]] for c, (d, b, _) in enumerate(chains)]
    k = [ins[d][1][b, :, sl[c]] for c, (d, b, _) in enumerate(chains)]
    v = [ins[d][2][b, :, sl[c]] for c, (d, b, _) in enumerate(chains)]
    cums = [_cum(masks[d][0], masks[d][2], g_col[d][b][:, h:h + 1], g_row[d][b][h:h + 1, :]) for d, b, h in chains]
    decay = [jnp.exp(jnp.where(masks[d][0], cums[c][0] - cums[c][1], -jnp.inf)) for c, (d, _, _) in enumerate(chains)]
    b_col = [beta_col[d][b][:, h:h + 1] for d, b, h in chains]
    kb = [k[c] * b_col[c] for c in idx]
    l_mat = [_dot(kb[c], k[c], NT) * jnp.where(masks[d][1], decay[c], 0.0) for c, (d, _, _) in enumerate(chains)]
    nil = [-jnp.where(blocks[0], l_mat[c], 0.0) for c in idx]
    t_inv = [eye + nil[c] for c in idx]
    for _ in range(2):
        nil = [_dot(nil[c], nil[c]) for c in idx]
        t_inv = [t_inv[c] + _dot(t_inv[c], nil[c]) for c in idx]
    for lvl in range(3):
        left = [_dot(t_inv[c], jnp.where(off_masks[lvl], l_mat[c], 0.0)) for c in idx]
        t_inv = [t_inv[c] - _dot(left[c], t_inv[c]) for c in idx]
    e_col = [jnp.exp(gc) for gc, _, _ in cums]
    uw = [_dot(t_inv[c], jnp.concatenate([v[c] * b_col[c], kb[c] * e_col[c]], axis=1)) for c in idx]
    qk = [_dot(q[c], k[c], NT) * decay[c] for c in idx]
    k_dec_t = [(k[c] * jnp.exp(cums[c][2] - cums[c][0])).T for c in idx]
    s = [st_ref[c] for c in idx]
    wq = [_dot(jnp.concatenate([uw[c][:, A_DV:], q[c] * e_col[c]], axis=0), s[c]) for c in idx]
    v_new = [uw[c][:, :A_DV] - wq[c][:CHUNK] for c in idx]
    out = [wq[c][CHUNK:] + _dot(qk[c], v_new[c]) for c in idx]
    s_new = [jnp.exp(cums[c][2]) * s[c] + _dot(k_dec_t[c], v_new[c]) for c in idx]
    for c, (d, b, _) in enumerate(chains):
        o_refs[d][b, :, sl[c]] = out[c]
        st_ref[c] = s_new[c]


def _gdn(act, beta_raw, decay_raw, a_log, dt_bias, n_ctx):
    bsz, seq, _ = act.shape
    n_all = seq // CHUNK
    hw = A_HEADS * A_DK
    bcol, brow = _gate_layouts(beta_raw, A_HEADS)
    dcol, drow = _gate_layouts(decay_raw, A_HEADS)
    pcol, prow = _param_layouts(a_log, dt_bias)
    ci = functools.partial(_chunk_index, n_ctx=n_ctx, n_all=n_all)

    def specs(d):
        tok = lambda blk: pl.BlockSpec((bsz, CHUNK, hw), lambda n: (0, ci(d, n), blk))
        gcol = pl.BlockSpec((1, bsz, 1, CHUNK, A_HEADS), lambda n: (d, 0, ci(d, n), 0, 0))
        grow = pl.BlockSpec((1, bsz, 1, A_HEADS, CHUNK), lambda n: (d, 0, ci(d, n), 0, 0))
        return [tok(0), tok(1), tok(2), gcol, grow, gcol, grow]

    out = jax.ShapeDtypeStruct((bsz, seq, hw), f32)
    return _ScanPart(
        kernel=functools.partial(_gdn_kernel, bsz=bsz),
        operands=[act, act, act, bcol, brow, dcol, drow] * N_DIR + [pcol, prow],
        in_specs=_both_directions(specs) + [pl.BlockSpec((N_DIR, 2, A_HEADS), lambda n: (0, 0, 0)),
                                            pl.BlockSpec((N_DIR, A_HEADS, 2), lambda n: (0, 0, 0))],
        out_shapes=[out] * N_DIR,
        out_specs=[pl.BlockSpec((bsz, CHUNK, hw), lambda n, d=d: (0, ci(d, n), 0)) for d in range(N_DIR)],
        scratch_shapes=[pltpu.VMEM((N_DIR * bsz * A_HEADS, A_DK, A_DV), f32)])


def _ssd_kernel(*refs, bsz):
    n_in = 5
    ins = [refs[d * n_in:(d + 1) * n_in] for d in range(N_DIR)]
    pcol_ref, prow_ref = refs[N_DIR * n_in:N_DIR * n_in + 2]
    o_refs = refs[N_DIR * n_in + 2:N_DIR * n_in + 2 + N_DIR]
    st_ref = refs[-1]
    masks = [_chunk_masks(d) for d in range(N_DIR)]
    rep = B_HEADS // B_GROUPS
    streams = [(d, b) for d in range(N_DIR) for b in range(bsz)]
    groups = [(s, g) for s in range(len(streams)) for g in range(B_GROUPS)]
    heads = [(s, h) for s in range(len(streams)) for h in range(B_HEADS)]
    gsl = [slice(g * B_STATE, (g + 1) * B_STATE) for _, g in groups]
    hsl = [slice(h * B_HEADDIM, (h + 1) * B_HEADDIM) for _, h in heads]
    pc = [pcol_ref[d] for d, _ in streams]
    pr = [prow_ref[d] for d, _ in streams]
    dt_col = [_softplus(ins[d][3][0, b, 0] + pc[s][1:2, :]) for s, (d, b) in enumerate(streams)]
    dt_row = [_softplus(ins[d][4][0, b, 0] + pr[s][:, 1:2]) for s, (d, b) in enumerate(streams)]
    da_col = [dt_col[s] * (-jnp.exp(pc[s][0:1, :])) for s in range(len(streams))]
    da_row = [dt_row[s] * (-jnp.exp(pr[s][:, 0:1])) for s in range(len(streams))]
    bm = [ins[streams[s][0]][1][streams[s][1], :, gsl[i]] for i, (s, _) in enumerate(groups)]
    cm = [ins[streams[s][0]][2][streams[s][1], :, gsl[i]] for i, (s, _) in enumerate(groups)]
    hs = [st_ref[i] for i in range(len(groups))]
    cb = [_dot(cm[i], bm[i], NT) for i in range(len(groups))]
    y_off = [_dot(cm[i], hs[i]) for i in range(len(groups))]
    bm_t = [bm[i].T for i in range(len(groups))]
    m_incl = [masks[streams[s][0]][0] for s, _ in heads]
    m_incl_t = [masks[streams[s][0]][2] for s, _ in heads]
    cums = [_cum(m_incl[j], m_incl_t[j], da_col[s][:, h:h + 1], da_row[s][h:h + 1, :]) for j, (s, h) in enumerate(heads)]
    seg = [jnp.exp(jnp.where(m_incl[j], cums[j][0] - cums[j][1], -jnp.inf)) for j in range(len(heads))]
    xdt = [ins[streams[s][0]][0][streams[s][1], :, hsl[j]] * dt_col[s][:, h:h + 1] for j, (s, h) in enumerate(heads)]
    y_diag = [_dot(cb[j // rep] * seg[j], xdt[j]) for j in range(len(heads))]
    for j, (s, h) in enumerate(heads):
        r = h % rep
        d, b = streams[s]
        o_refs[d][b, :, hsl[j]] = y_diag[j] + jnp.exp(cums[j][0]) * y_off[j // rep][:, r * B_HEADDIM:(r + 1) * B_HEADDIM]
    xdt_end = [xdt[j] * jnp.exp(cums[j][2] - cums[j][0]) for j in range(len(heads))]
    keep = [jnp.broadcast_to(jnp.exp(cums[j][2]), (1, B_HEADDIM)) for j in range(len(heads))]
    upd = [_dot(bm_t[i], jnp.concatenate(xdt_end[i * rep:(i + 1) * rep], axis=1)) for i in range(len(groups))]
    for i in range(len(groups)):
        st_ref[i] = jnp.concatenate(keep[i * rep:(i + 1) * rep], axis=1) * hs[i] + upd[i]


def _ssd(act, dt_raw, a_log, dt_bias, n_ctx):
    bsz, seq, _ = act.shape
    n_all = seq // CHUNK
    dinner = B_HEADS * B_HEADDIM
    gw = B_GROUPS * B_STATE
    tcol, trow = _gate_layouts(dt_raw, B_HEADS)
    pcol, prow = _param_layouts(a_log, dt_bias)
    ci = functools.partial(_chunk_index, n_ctx=n_ctx, n_all=n_all)
    x_off = (A_HEADS * (2 * A_DK + A_DV)) // dinner
    bm_off = (A_HEADS * (2 * A_DK + A_DV) + dinner) // gw

    def specs(d):
        return [pl.BlockSpec((bsz, CHUNK, dinner), lambda n: (0, ci(d, n), x_off)),
                pl.BlockSpec((bsz, CHUNK, gw), lambda n: (0, ci(d, n), bm_off)),
                pl.BlockSpec((bsz, CHUNK, gw), lambda n: (0, ci(d, n), bm_off + 1)),
                pl.BlockSpec((1, bsz, 1, CHUNK, B_HEADS), lambda n: (d, 0, ci(d, n), 0, 0)),
                pl.BlockSpec((1, bsz, 1, B_HEADS, CHUNK), lambda n: (d, 0, ci(d, n), 0, 0))]

    out = jax.ShapeDtypeStruct((bsz, seq, dinner), f32)
    return _ScanPart(
        kernel=functools.partial(_ssd_kernel, bsz=bsz),
        operands=[act, act, act, tcol, trow] * N_DIR + [pcol, prow],
        in_specs=_both_directions(specs) + [pl.BlockSpec((N_DIR, 2, B_HEADS), lambda n: (0, 0, 0)),
                                            pl.BlockSpec((N_DIR, B_HEADS, 2), lambda n: (0, 0, 0))],
        out_shapes=[out] * N_DIR,
        out_specs=[pl.BlockSpec((bsz, CHUNK, dinner), lambda n, d=d: (0, ci(d, n), 0)) for d in range(N_DIR)],
        scratch_shapes=[pltpu.VMEM((N_DIR * bsz * B_GROUPS, B_STATE, (B_HEADS // B_GROUPS) * B_HEADDIM), f32)])


def _mlstm_kernel(*refs, bsz):
    n_in = 7
    ins = [refs[d * n_in:(d + 1) * n_in] for d in range(N_DIR)]
    pcol_ref, prow_ref = refs[N_DIR * n_in:N_DIR * n_in + 2]
    o_refs = refs[N_DIR * n_in + 2:N_DIR * n_in + 2 + N_DIR]
    c_ref, n_ref, m_ref = refs[-3:]
    masks = [_chunk_masks(d) for d in range(N_DIR)]
    streams = [(d, b) for d in range(N_DIR) for b in range(bsz)]
    chains = [(s, h) for s in range(len(streams)) for h in range(C_HEADS)]
    idx = range(len(chains))
    pc = [pcol_ref[d] for d, _ in streams]
    pr = [prow_ref[d] for d, _ in streams]
    li_col = [ins[d][3][0, b, 0] + pc[s][0:1, :] for s, (d, b) in enumerate(streams)]
    li_row = [ins[d][4][0, b, 0] + pr[s][:, 0:1] for s, (d, b) in enumerate(streams)]
    lf_col = [-_softplus(-(ins[d][5][0, b, 0] + pc[s][1:2, :])) for s, (d, b) in enumerate(streams)]
    lf_row = [-_softplus(-(ins[d][6][0, b, 0] + pr[s][:, 1:2])) for s, (d, b) in enumerate(streams)]
    q = [ins[streams[s][0]][0][streams[s][1], :, h * C_DK:(h + 1) * C_DK] for s, h in chains]
    k = [ins[streams[s][0]][1][streams[s][1], :, h * C_DK:(h + 1) * C_DK] for s, h in chains]
    v = [ins[streams[s][0]][2][streams[s][1], :, h * C_DV:(h + 1) * C_DV] for s, h in chains]
    m_incl = [masks[streams[s][0]][0] for s, _ in chains]
    m_incl_t = [masks[streams[s][0]][2] for s, _ in chains]
    cums = [_cum(m_incl[c], m_incl_t[c], lf_col[s][:, h:h + 1], lf_row[s][h:h + 1, :]) for c, (s, h) in enumerate(chains)]
    logw_intra = [jnp.where(m_incl[c], cums[c][0] - cums[c][1] + li_row[s][h:h + 1, :], -jnp.inf)
                  for c, (s, h) in enumerate(chains)]
    m_prev = [m_ref[c] for c in idx]
    logw_inter = [cums[c][0] + m_prev[c] for c in idx]
    m_t = [jnp.maximum(logw_inter[c], jnp.max(logw_intra[c], axis=1, keepdims=True)) for c in idx]
    qk = [_dot(q[c], k[c], NT) for c in idx]
    c_mem = [c_ref[c] for c in idx]
    n_mem = [n_ref[c] for c in idx]
    qc = [_dot(q[c], c_mem[c]) for c in idx]
    s = [qk[c] * jnp.exp(logw_intra[c] - m_t[c]) for c in idx]
    w_inter = [jnp.exp(logw_inter[c] - m_t[c]) for c in idx]
    num = [_dot(s[c], v[c]) + w_inter[c] * qc[c] for c in idx]
    den = [jnp.sum(s[c], axis=1, keepdims=True) + w_inter[c] * jnp.sum(q[c] * n_mem[c], axis=1, keepdims=True)
           for c in idx]
    for c, (s, h) in enumerate(chains):
        d, b = streams[s]
        o_refs[d][b, :, h * C_DV:(h + 1) * C_DV] = num[c] / jnp.maximum(jnp.abs(den[c]), jnp.exp(-m_t[c]))
    logw_end = [cums[c][2] - cums[c][0] + li_col[s][:, h:h + 1] for c, (s, h) in enumerate(chains)]
    m_new = [jnp.maximum(cums[c][2] + m_prev[c], jnp.max(logw_end[c], axis=0, keepdims=True)) for c in idx]
    kw = [k[c] * jnp.exp(logw_end[c] - m_new[c]) for c in idx]
    keep = [jnp.exp(cums[c][2] + m_prev[c] - m_new[c]) for c in idx]
    kv = [_dot(kw[c].T, v[c]) for c in idx]
    for c in idx:
        c_ref[c] = keep[c] * c_mem[c] + kv[c]
        n_ref[c] = keep[c] * n_mem[c] + jnp.sum(kw[c], axis=0, keepdims=True)
        m_ref[c] = m_new[c]


def _mlstm(act, proj, i_raw, f_raw, i_bias, f_bias, n_ctx):
    bsz, seq, _ = act.shape
    n_all = seq // CHUNK
    qw = C_HEADS * C_DK
    vw = C_HEADS * C_DV
    icol, irow = _gate_layouts(i_raw, C_HEADS)
    fcol, frow = _gate_layouts(f_raw, C_HEADS)
    pcol, prow = _param_layouts(i_bias, f_bias)
    ci = functools.partial(_chunk_index, n_ctx=n_ctx, n_all=n_all)

    def specs(d):
        gcol = pl.BlockSpec((1, bsz, 1, CHUNK, C_HEADS), lambda n: (d, 0, ci(d, n), 0, 0))
        grow = pl.BlockSpec((1, bsz, 1, C_HEADS, CHUNK), lambda n: (d, 0, ci(d, n), 0, 0))
        return [pl.BlockSpec((bsz, CHUNK, qw), lambda n: (0, ci(d, n), 0)),
                pl.BlockSpec((bsz, CHUNK, qw), lambda n: (0, ci(d, n), 1)),
                pl.BlockSpec((bsz, CHUNK, vw), lambda n: (0, ci(d, n), 2 * qw // vw)),
                gcol, grow, gcol, grow]

    out = jax.ShapeDtypeStruct((bsz, seq, vw), f32)
    n_chain = N_DIR * bsz * C_HEADS
    return _ScanPart(
        kernel=functools.partial(_mlstm_kernel, bsz=bsz),
        operands=[act, act, proj, icol, irow, fcol, frow] * N_DIR + [pcol, prow],
        in_specs=_both_directions(specs) + [pl.BlockSpec((N_DIR, 2, C_HEADS), lambda n: (0, 0, 0)),
                                            pl.BlockSpec((N_DIR, C_HEADS, 2), lambda n: (0, 0, 0))],
        out_shapes=[out] * N_DIR,
        out_specs=[pl.BlockSpec((bsz, CHUNK, vw), lambda n, d=d: (0, ci(d, n), 0)) for d in range(N_DIR)],
        scratch_shapes=[pltpu.VMEM((n_chain, C_DK, C_DV), f32),
                        pltpu.VMEM((n_chain, 1, C_DK), f32),
                        pltpu.VMEM((n_chain, 1, 1), f32)])


def _merge_even_kernel(x_ref, af_ref, ar_ref, bf_ref, br_ref, za_ref, zb_ref, xs_ref, na_ref, nb_ref, dsk_ref,
                       w_ref, g_ref, lg_ref, lb_ref, o_ref):
    a = af_ref[...] + ar_ref[...]
    za = za_ref[...]
    na = na_ref[...]
    parts = []
    for h in range(A_HEADS):
        sl = slice(h * A_DV, (h + 1) * A_DV)
        parts.append(_rms(a[:, sl], na) * _silu(za[:, sl]))
    yb = (bf_ref[...] + br_ref[...] + dsk_ref[...] * xs_ref[...]) * _silu(zb_ref[...])
    gw = (B_HEADS * B_HEADDIM) // B_GROUPS
    nb = nb_ref[...]
    for g in range(B_GROUPS):
        sl = slice(g * gw, (g + 1) * gw)
        parts.append(_rms(yb[:, sl], nb[:, sl]))
    y = jnp.concatenate(parts, axis=1).astype(bf16)
    z = ALPHA * x_ref[...] + g_ref[0] * _dot(y, w_ref[...])
    o_ref[...] = _layer_norm(z, lg_ref[...], lb_ref[...])


def _merge_even(x, o_a, o_b, proj, act, norm_a, norm_b, d_skip, w_out, gate_t, ln_g, ln_b):
    rows, d = x.shape
    aw = A_HEADS * A_DV
    bw = B_HEADS * B_HEADDIM
    row = lambda w_, blk: pl.BlockSpec((ROW_TILE, w_), lambda i: (i, blk))
    vec = lambda w_: pl.BlockSpec((1, w_), lambda i: (0, 0))
    qkv = A_HEADS * (2 * A_DK + A_DV)
    return pl.pallas_call(
        _merge_even_kernel,
        out_shape=jax.ShapeDtypeStruct((rows, d), f32),
        grid=(rows // ROW_TILE,),
        in_specs=[row(d, 0), row(aw, 0), row(aw, 0), row(bw, 0), row(bw, 0),
                  row(aw, qkv // aw), row(bw, (qkv + aw) // bw), row(bw, qkv // bw),
                  vec(A_DV), vec(bw), vec(bw),
                  pl.BlockSpec((aw + bw, d), lambda i: (0, 0)),
                  pl.BlockSpec((1, 1, d), lambda i: (i, 0, 0)),
                  vec(d), vec(d)],
        out_specs=row(d, 0),
        compiler_params=_params("parallel"),
    )(x, *o_a, *o_b, proj, proj, act, norm_a.reshape(1, -1), norm_b.reshape(1, -1),
      jnp.repeat(d_skip, B_HEADDIM).reshape(1, -1), w_out, gate_t, ln_g.reshape(1, -1), ln_b.reshape(1, -1))


def _merge_odd_kernel(hf_ref, hr_ref, o_ref_in, nc_ref, w_ref, out_ref):
    hsum = hf_ref[...] + hr_ref[...]
    o = o_ref_in[...]
    nc = nc_ref[...]
    parts = []
    for h in range(C_HEADS):
        sl = slice(h * C_DV, (h + 1) * C_DV)
        parts.append(_rms(hsum[:, sl], nc) * _sigmoid(o[:, sl]))
    y = jnp.concatenate(parts, axis=1).astype(bf16)
    out_ref[...] = _dot(y, w_ref[...])


def _merge_odd(h, proj, norm_c, w_out, tiles_per_batch, ctx_tiles, bsz):
    vw = C_HEADS * C_DV
    lat_tiles = tiles_per_batch - ctx_tiles
    src = lambda i: i + ctx_tiles * (i // lat_tiles + 1)
    d = w_out.shape[1]
    return pl.pallas_call(
        _merge_odd_kernel,
        out_shape=jax.ShapeDtypeStruct((bsz * lat_tiles * ROW_TILE, d), f32),
        grid=(bsz * lat_tiles,),
        in_specs=[pl.BlockSpec((ROW_TILE, vw), lambda i: (src(i), 0)),
                  pl.BlockSpec((ROW_TILE, vw), lambda i: (src(i), 0)),
                  pl.BlockSpec((ROW_TILE, vw), lambda i: (src(i), 2 * C_HEADS * C_DK // vw + 1)),
                  pl.BlockSpec((1, C_DV), lambda i: (0, 0)),
                  pl.BlockSpec((vw, d), lambda i: (0, 0))],
        out_specs=pl.BlockSpec((ROW_TILE, d), lambda i: (i, 0)),
        compiler_params=_params("parallel"),
    )(*h, proj, norm_c.reshape(1, -1), w_out)


def _resid_ln_kernel(x_ref, y_ref, g_ref, lg_ref, lb_ref, o_ref, *, transposed):
    y = y_ref[...]
    if transposed:
        y = y.T
    o_ref[...] = _layer_norm(ALPHA * x_ref[...] + g_ref[0] * y, lg_ref[...], lb_ref[...])


def _resid_ln(x, y, gate_t, ln_g, ln_b, *, transposed):
    rows, d = x.shape
    y_spec = (pl.BlockSpec((d, ROW_TILE), lambda i: (0, i)) if transposed
              else pl.BlockSpec((ROW_TILE, d), lambda i: (i, 0)))
    return pl.pallas_call(
        functools.partial(_resid_ln_kernel, transposed=transposed),
        out_shape=jax.ShapeDtypeStruct((rows, d), f32),
        grid=(rows // ROW_TILE,),
        in_specs=[pl.BlockSpec((ROW_TILE, d), lambda i: (i, 0)), y_spec,
                  pl.BlockSpec((1, 1, d), lambda i: (i, 0, 0)),
                  pl.BlockSpec((1, d), lambda i: (0, 0)),
                  pl.BlockSpec((1, d), lambda i: (0, 0))],
        out_specs=pl.BlockSpec((ROW_TILE, d), lambda i: (i, 0)),
        compiler_params=_params("parallel"),
    )(x, y, gate_t, ln_g.reshape(1, -1), ln_b.reshape(1, -1))


def _oddeven_merge(lo, hi, r):
    step = r * 2
    if step < hi - lo:
        yield from _oddeven_merge(lo, hi, step)
        yield from _oddeven_merge(lo + r, hi, step)
        yield from [(i, i + r) for i in range(lo + r, hi - r, step)]
    else:
        yield (lo, lo + r)


def _oddeven_sort(lo, hi):
    if hi - lo >= 1:
        mid = lo + (hi - lo) // 2
        yield from _oddeven_sort(lo, mid)
        yield from _oddeven_sort(mid + 1, hi)
        yield from _oddeven_merge(lo, hi, 1)


def _exchange(a, i, j):
    a[i], a[j] = jnp.maximum(a[i], a[j]), jnp.minimum(a[i], a[j])


def _top_sorted(slabs):
    a = list(slabs)
    for i, j in _oddeven_sort(0, len(a) - 1):
        _exchange(a, i, j)
    for shift in (4, 2, 1):
        other = [pltpu.roll(x, shift, axis=0) for x in a]
        if len(a) < PEER_TOPK:
            a = a + other[::-1]
        else:
            a = [jnp.maximum(a[k], other[PEER_TOPK - 1 - k]) for k in range(PEER_TOPK)]
        dist = PEER_TOPK // 2
        while dist >= 1:
            for i in range(PEER_TOPK):
                if i & dist == 0:
                    _exchange(a, i, i + dist)
            dist //= 2
    return a


def _next_largest(slabs, kth):
    count = sum(jnp.where(s >= kth, 1.0, 0.0) for s in slabs)
    below = functools.reduce(jnp.maximum, [jnp.where(s < kth, s, -jnp.inf) for s in slabs])
    count = jnp.sum(count, axis=0, keepdims=True)
    below = jnp.max(below, axis=0, keepdims=True)
    return jnp.where(count > PEER_TOPK, kth, below)


def _top_values(s):
    slabs = [s[i:i + 8] for i in range(0, s.shape[0], 8)]
    if len(slabs) < 8:
        slabs = slabs + [jnp.full_like(slabs[0], -jnp.inf)] * (8 - len(slabs))
    top = [t[0:1] for t in _top_sorted(slabs)]
    return top + [_next_largest(slabs, top[-1])]


def _route_kernel(x_ref, sh_ref, sc_ref, w_ref, k_ref, xm_ref, thr_ref, s2_ref, e1_ref, e2_ref, sc_scr, cand_ref):
    for t in range(ROUTE_TILE // ROW_TILE):
        rows = slice(t * ROW_TILE, (t + 1) * ROW_TILE)
        xm_ref[rows, :] = (x_ref[rows, :] * (1.0 + sc_ref[t]) + sh_ref[t]).astype(bf16)
    half = PEER_DKEY // 2
    n_top = PEER_TOPK + 1
    pairs = [(p, r) for p in range(n_top) for r in range(n_top) if (p + 1) * (r + 1) <= n_top]
    cand_ref[...] = jnp.full(cand_ref.shape, -jnp.inf, f32)

    def scores(h, sc_scr):
        wsl = pl.ds(pl.multiple_of(h * PEER_DKEY, PEER_DKEY), PEER_DKEY)
        q = _dot(xm_ref[...], w_ref[:, wsl])
        for c in range(2):
            sc_scr[c] = _dot(k_ref[h, c], q[:, c * half:(c + 1) * half].astype(bf16), NT)

    def select(h, sc_scr):
        for lb in range(ROUTE_TILE // LANE):
            lanes = slice(lb * LANE, (lb + 1) * LANE)
            s1 = sc_scr[0, :, lanes]
            s2 = sc_scr[1, :, lanes]
            t1 = _top_values(s1)
            t2 = _top_values(s2)
            for i, (p, r) in enumerate(pairs):
                cand_ref[i:i + 1, lanes] = t1[p] + t2[r]
            best = _top_values(cand_ref[:, lanes])
            z = sum(jnp.exp(bv - best[0]) for bv in best[:PEER_TOPK])
            tau = 0.5 * (best[PEER_TOPK - 1] + best[PEER_TOPK])
            thr_ref[h, :, lanes] = tau - s1
            s2_ref[h, :, lanes] = s2
            e1_ref[h, :, lanes] = jnp.exp(s1 - t1[0]) / z
            e2_ref[h, :, lanes] = jnp.exp(s2 - t2[0])

    def head(h, carry):
        scores(h, sc_scr)
        select(h, sc_scr)
        return carry

    lax.fori_loop(0, PEER_HEADS, head, 0)


def _route(x, shift_t, scale_t, wq, keys):
    rows, d = x.shape
    nq = wq.shape[1]
    half = PEER_DKEY // 2
    score = jax.ShapeDtypeStruct((PEER_HEADS, PEER_NKEYS, rows), f32)
    score_spec = pl.BlockSpec((PEER_HEADS, PEER_NKEYS, ROUTE_TILE), lambda i: (0, 0, i))
    key_spec = pl.BlockSpec((PEER_HEADS, 2, PEER_NKEYS, half), lambda i: (0, 0, 0, 0))
    mods = ROUTE_TILE // ROW_TILE
    return pl.pallas_call(
        _route_kernel,
        out_shape=(jax.ShapeDtypeStruct((rows, d), bf16), score, score, score, score),
        grid=(rows // ROUTE_TILE,),
        in_specs=[pl.BlockSpec((ROUTE_TILE, d), lambda i: (i, 0)),
                  pl.BlockSpec((mods, 1, d), lambda i: (i, 0, 0)),
                  pl.BlockSpec((mods, 1, d), lambda i: (i, 0, 0)),
                  pl.BlockSpec((d, nq), lambda i: (0, 0)),
                  key_spec],
        out_specs=(pl.BlockSpec((ROUTE_TILE, d), lambda i: (i, 0)),
                   score_spec, score_spec, score_spec, score_spec),
        scratch_shapes=[pltpu.VMEM((2, PEER_NKEYS, ROUTE_TILE), f32),
                        pltpu.VMEM((N_CAND_ROWS, ROUTE_TILE), f32)],
        compiler_params=_params("parallel"),
    )(x, shift_t, scale_t, wq, keys)


def _expert_kernel(xm_ref, u_ref, vt_ref, thr_ref, e1_ref, s2_ref, e2_ref, o_ref, a_scr, w_scr):
    j = pl.program_id(1)

    @pl.when(j == 0)
    def _():
        o_ref[...] = jnp.zeros(o_ref.shape, f32)

    a_scr[...] = _dot(u_ref[...], xm_ref[...], NT)
    sqrt_half = math.sqrt(0.5)
    row_chunks = PEER_NKEYS // GATE_ROWS
    tiles_per_group = (EXPERT_TOK // LANE) * row_chunks

    def tile(r, carry, i1_0):
        lanes = pl.ds(pl.multiple_of((r // row_chunks) * LANE, LANE), LANE)
        r0 = pl.multiple_of((r % row_chunks) * GATE_ROWS, GATE_ROWS)
        i2 = pl.ds(r0, GATE_ROWS)
        gate = [jnp.zeros((GATE_ROWS, LANE), f32) for _ in range(I1_TILE)]
        for h in range(PEER_HEADS):
            s2 = s2_ref[h, i2, lanes]
            e2 = e2_ref[h, i2, lanes]
            for il in range(I1_TILE):
                i1 = i1_0 + il
                w = e1_ref[h, i1:i1 + 1, lanes] * e2
                gate[il] = gate[il] + jnp.where(s2 >= thr_ref[h, i1:i1 + 1, lanes], w, 0.0)
        for il in range(I1_TILE):
            rows = pl.ds((i1_0 + il) * PEER_NKEYS + r0, GATE_ROWS)
            a = a_scr[rows, lanes]
            act = 0.5 * a * (1.0 + lax.erf(a * sqrt_half))
            w_scr[rows, lanes] = (gate[il] * act).astype(bf16)
        return carry

    for i1_0 in range(0, EXPERT_BLK // PEER_NKEYS, I1_TILE):
        lax.fori_loop(0, tiles_per_group, functools.partial(tile, i1_0=i1_0), 0)
    o_ref[...] += _dot(vt_ref[...], w_scr[...])


def _experts(xm, u_bf, vt_bf, thr, e1, s2, e2):
    rows, d = xm.shape
    n_exp = u_bf.shape[0]
    i1_per_blk = EXPERT_BLK // PEER_NKEYS
    sel = pl.BlockSpec((PEER_HEADS, i1_per_blk, EXPERT_TOK), lambda i, j: (0, j, i))
    full = pl.BlockSpec((PEER_HEADS, PEER_NKEYS, EXPERT_TOK), lambda i, j: (0, 0, i))
    return pl.pallas_call(
        _expert_kernel,
        out_shape=jax.ShapeDtypeStruct((d, rows), f32),
        grid=(rows // EXPERT_TOK, n_exp // EXPERT_BLK),
        in_specs=[pl.BlockSpec((EXPERT_TOK, d), lambda i, j: (i, 0)),
                  pl.BlockSpec((EXPERT_BLK, d), lambda i, j: (j, 0)),
                  pl.BlockSpec((d, EXPERT_BLK), lambda i, j: (0, j)),
                  sel, sel, full, full],
        out_specs=pl.BlockSpec((d, EXPERT_TOK), lambda i, j: (0, i)),
        scratch_shapes=[pltpu.VMEM((EXPERT_BLK, EXPERT_TOK), f32),
                        pltpu.VMEM((EXPERT_BLK, EXPERT_TOK), bf16)],
        compiler_params=_params("parallel", "arbitrary"),
    )(xm, u_bf, vt_bf, thr, e1, s2, e2)


def _peer_residual(x, shift_t, scale_t, gate_t, peer_q, peer_keys, peer_u, peer_v, ln_g, ln_b):
    xm, thr, s2, e1, e2 = _route(x, shift_t, scale_t, peer_q.astype(bf16), peer_keys.astype(bf16))
    y_t = _experts(xm, peer_u.astype(bf16), peer_v.T.astype(bf16), thr, e1, s2, e2)
    return _resid_ln(x, y_t, gate_t, ln_g, ln_b, transposed=True)


def _grid_sincos(n_tokens, d):
    n_rows = n_tokens // GRID_W
    n_freq = d // 4
    freq = jnp.exp(-math.log(POS_BASE) * jnp.arange(n_freq, dtype=f32) / n_freq)[None, :]
    row = jnp.arange(n_rows, dtype=f32)[:, None] * freq
    col = jnp.arange(GRID_W, dtype=f32)[:, None] * freq
    row_part = jnp.concatenate([jnp.sin(row), jnp.cos(row)], -1)[:, None, :]
    col_part = jnp.concatenate([jnp.sin(col), jnp.cos(col)], -1)[None, :, :]
    shape = (n_rows, GRID_W, 2 * n_freq)
    table = jnp.concatenate([jnp.broadcast_to(row_part, shape), jnp.broadcast_to(col_part, shape)], -1)
    return table.reshape(n_tokens, d)


def _tile_rows(mod, src):
    d = mod.shape[1] // 6
    rows = jnp.concatenate([jnp.broadcast_to(mod[r:r + 1], (n, mod.shape[1])) for r, n in src], axis=0)
    return [rows[:, None, i * d:(i + 1) * d] for i in range(6)]


def _pad_cols(w, n):
    return jnp.pad(w, ((0, 0), (0, n - w.shape[1])))


def kernel(x, c, ctx, c_ctx, l0_ada_w, l0_ada_b, l0_ln1_g, l0_ln1_b, l0_w_in, l0_conv_a, l0_a_log_a, l0_dt_bias_a, l0_norm_a, l0_conv_b, l0_conv_b_bias, l0_a_log_b, l0_dt_bias_b, l0_d_skip_b, l0_norm_b, l0_w_out, l0_ln2_g, l0_ln2_b, l0_peer_q, l0_peer_keys, l0_peer_u, l0_peer_v, l1_ada_w, l1_ada_b, l1_ln1_g, l1_ln1_b, l1_w_in, l1_conv_c, l1_i_bias, l1_f_bias, l1_norm_c, l1_w_out, l1_ln2_g, l1_ln2_b, l1_peer_q, l1_peer_keys, l1_peer_u, l1_peer_v):
    bsz, n_lat, d = x.shape
    n_ctx_tok = ctx.shape[1]
    rows_grid = n_lat // GRID_W
    seq = n_ctx_tok + n_lat
    n_ctx = n_ctx_tok // CHUNK
    tiles_b = seq // ROW_TILE
    ctx_tiles = n_ctx_tok // ROW_TILE
    lat_tiles = n_lat // ROW_TILE

    cond = jnp.concatenate([c, c_ctx[None, :], jnp.zeros((8 - bsz - 1, d), f32)], axis=0)
    src_all = [run for b in range(bsz) for run in ((bsz, ctx_tiles), (b, lat_tiles))]

    x_lat = x + _grid_sincos(n_lat, d).astype(x.dtype)
    xs_all = jnp.concatenate([ctx, x_lat], axis=1).reshape(bsz * seq, d)

    sh1, sc1, g1, sh2, sc2, g2 = _tile_rows(_ada(cond, l0_ada_w, l0_ada_b), src_all)
    qkv_w = A_HEADS * (2 * A_DK + A_DV)
    az = A_HEADS * A_DV
    ag = N_DIR * A_HEADS
    bd = B_HEADS * B_HEADDIM
    bx = bd + 2 * B_GROUPS * B_STATE
    bt = N_DIR * B_HEADS
    o0 = np.cumsum([0, qkv_w, az, ag, ag, bd, bx, bt])
    w_qkv, w_za, w_beta, w_dec, w_zb, w_xbc, w_dt = (l0_w_in[:, o0[i]:o0[i + 1]] for i in range(7))
    main_w = qkv_w + az + bd + bx
    w0 = _pad_cols(jnp.concatenate([w_qkv, w_za, w_zb, w_xbc, w_beta, w_dec, w_dt], axis=1), main_w + LANE)
    proj = _proj(xs_all, sh1, sc1, w0.astype(bf16)).reshape(bsz, seq, main_w + LANE)
    conv_w = jnp.concatenate([l0_conv_a, l0_conv_b], axis=1)
    conv_b = jnp.concatenate([jnp.zeros((qkv_w,), f32), l0_conv_b_bias]).reshape(1, -1)
    xbc_blk = (qkv_w + az + bd) // LANE
    col_blocks = list(range(qkv_w // LANE)) + list(range(xbc_blk, xbc_blk + bx // LANE))
    act = _conv_act(proj, conv_w, conv_b, col_blocks, seg0=n_ctx_tok,
                    n_norm=2 * A_HEADS, n_qscale=A_HEADS, qscale=A_DK ** -0.5)
    gates = proj[:, :, main_w:]
    o_a, o_b = _run_scans([_gdn(act, gates[..., :ag], gates[..., ag:2 * ag], l0_a_log_a, l0_dt_bias_a, n_ctx),
                           _ssd(act, gates[..., 2 * ag:2 * ag + bt], l0_a_log_b, l0_dt_bias_b, n_ctx)],
                          seq // CHUNK)
    x1 = _merge_even(xs_all, [o.reshape(bsz * seq, az) for o in o_a], [o.reshape(bsz * seq, bd) for o in o_b],
                     proj.reshape(bsz * seq, -1), act.reshape(bsz * seq, -1),
                     l0_norm_a, l0_norm_b, l0_d_skip_b, l0_w_out.astype(bf16), g1, l0_ln1_g, l0_ln1_b)
    x2 = _peer_residual(x1, sh2, sc2, g2, l0_peer_q, l0_peer_keys, l0_peer_u, l0_peer_v, l0_ln2_g, l0_ln2_b)

    sh1, sc1, g1, sh2, sc2, g2 = _tile_rows(_ada(cond, l1_ada_w, l1_ada_b), src_all)
    x2 = x2.reshape(bsz, seq, d)
    x2_ctx, x2_lat = x2[:, :n_ctx_tok], x2[:, n_ctx_tok:]
    lat_cm = x2_lat.reshape(bsz, rows_grid, GRID_W, d).transpose(0, 2, 1, 3).reshape(bsz, n_lat, d)
    xs1 = jnp.concatenate([x2_ctx, lat_cm], axis=1).reshape(bsz * seq, d)
    qk_w = 2 * C_HEADS * C_DK
    vw = C_HEADS * C_DV
    w1 = _pad_cols(l1_w_in, qk_w + 2 * vw + LANE)
    proj1 = _proj(xs1, sh1, sc1, w1.astype(bf16)).reshape(bsz, seq, -1)
    act1 = _conv_act(proj1, l1_conv_c, jnp.zeros((1, qk_w), f32), list(range(qk_w // LANE)), seg0=n_ctx_tok,
                     n_norm=0, n_qscale=C_HEADS, qscale=C_DK ** -0.5)
    gates1 = proj1[:, :, qk_w + 2 * vw:]
    cg = N_DIR * C_HEADS
    (h1,) = _run_scans([_mlstm(act1, proj1, gates1[..., :cg], gates1[..., cg:2 * cg], l1_i_bias, l1_f_bias, n_ctx)],
                       seq // CHUNK)
    y_cm = _merge_odd([h.reshape(bsz * seq, vw) for h in h1], proj1.reshape(bsz * seq, -1), l1_norm_c,
                      l1_w_out.astype(bf16), tiles_b, ctx_tiles, bsz)
    y_lat = y_cm.reshape(bsz, GRID_W, rows_grid, d).transpose(0, 2, 1, 3).reshape(bsz * n_lat, d)
    lat_sel = lambda t: t.reshape(bsz, tiles_b, 1, d)[:, ctx_tiles:].reshape(bsz * lat_tiles, 1, d)
    x3 = _resid_ln(x2_lat.reshape(bsz * n_lat, d), y_lat, lat_sel(g1), l1_ln1_g, l1_ln1_b, transposed=False)
    x4 = _peer_residual(x3, lat_sel(sh2), lat_sel(sc2), lat_sel(g2), l1_peer_q, l1_peer_keys, l1_peer_u,
                        l1_peer_v, l1_ln2_g, l1_ln2_b)
    return x4.reshape(bsz, n_lat, d)
```

```python
import functools
import math
from typing import Callable, NamedTuple

import numpy as np
import jax
import jax.numpy as jnp
from jax import lax
from jax.experimental import pallas as pl
from jax.experimental.pallas import tpu as pltpu

f32 = jnp.float32
bf16 = jnp.bfloat16

D_MODEL = 1024
GRID_W = 64
CHUNK = 64
CONV_W = 5
POS_BASE = 10000.0
EPS = 1e-6
N_DIR = 2
DEPTH = 2
ALPHA = (2 * DEPTH) ** 0.25

A_HEADS, A_DK, A_DV = 4, 128, 128
B_HEADS, B_HEADDIM, B_GROUPS, B_STATE = 8, 64, 2, 128
C_HEADS, C_DK, C_DV = 4, 128, 256
PEER_HEADS, PEER_NKEYS, PEER_DKEY, PEER_TOPK = 8, 128, 256, 16

LANE = 128
ROW_TILE = 256
ROUTE_TILE = 512
EXPERT_TOK = 512
EXPERT_BLK = 2048
I1_TILE = 4
GATE_ROWS = 64
VMEM_LIMIT = 56 * 1024 * 1024
N_CAND_ROWS = -(-sum((PEER_TOPK + 1) // (p + 1) for p in range(PEER_TOPK + 1)) // 8) * 8

NN = (((1,), (0,)), ((), ()))
NT = (((1,), (1,)), ((), ()))


def _params(*sem):
    return pltpu.CompilerParams(dimension_semantics=sem, vmem_limit_bytes=VMEM_LIMIT)


def _dot(a, b, dims=NN):
    return lax.dot_general(a, b, dims, preferred_element_type=f32)


def _split(a):
    hi = a.astype(bf16)
    return hi, (a - hi.astype(f32)).astype(bf16)


def _dot3(a, b, dims=NN):
    ah, al = _split(a)
    bh, bl = _split(b)
    return _dot(ah, bh, dims) + (_dot(ah, bl, dims) + _dot(al, bh, dims))


def _sigmoid(x):
    return 1.0 / (1.0 + jnp.exp(-x))


def _silu(x):
    return x * _sigmoid(x)


def _softplus(x):
    return jnp.maximum(x, 0.0) + jnp.log1p(jnp.exp(-jnp.abs(x)))


def _layer_norm(z, g, b):
    mu = jnp.mean(z, -1, keepdims=True)
    zc = z - mu
    var = jnp.mean(zc * zc, -1, keepdims=True)
    return zc * lax.rsqrt(var + EPS) * g + b


def _rms(x, g):
    return x * lax.rsqrt(jnp.mean(x * x, -1, keepdims=True) + EPS) * g


def _chunk_masks(d):
    row = lax.broadcasted_iota(jnp.int32, (CHUNK, CHUNK), 0)
    col = lax.broadcasted_iota(jnp.int32, (CHUNK, CHUNK), 1)
    diff = row - col if d == 0 else col - row
    return diff >= 0, diff > 0, diff <= 0, row == col


def _cum(mask_incl, mask_incl_t, x_col, x_row):
    c_col = jnp.sum(jnp.where(mask_incl, x_row, 0.0), axis=1, keepdims=True)
    c_row = jnp.sum(jnp.where(mask_incl_t, x_col, 0.0), axis=0, keepdims=True)
    tot = jnp.sum(x_col, axis=0, keepdims=True)
    return c_col, c_row, tot


def _ada_kernel(c_ref, w_ref, b_ref, o_ref):
    o_ref[...] = _dot3(_silu(c_ref[...]), w_ref[...]) + b_ref[...]


def _ada(cond, w, b):
    rows, d = cond.shape
    n = w.shape[1]
    return pl.pallas_call(
        _ada_kernel,
        out_shape=jax.ShapeDtypeStruct((rows, n), f32),
        grid=(n // d,),
        in_specs=[pl.BlockSpec((rows, d), lambda j: (0, 0)),
                  pl.BlockSpec((d, d), lambda j: (0, j)),
                  pl.BlockSpec((1, d), lambda j: (0, j))],
        out_specs=pl.BlockSpec((rows, d), lambda j: (0, j)),
        compiler_params=_params("parallel"),
    )(cond, w, b.reshape(1, n))


def _proj_kernel(x_ref, sh_ref, sc_ref, w_ref, o_ref):
    xm = x_ref[...] * (1.0 + sc_ref[0]) + sh_ref[0]
    o_ref[...] = _dot(xm.astype(bf16), w_ref[...])


def _proj(x, shift_t, scale_t, w):
    rows, d = x.shape
    n = w.shape[1]
    return pl.pallas_call(
        _proj_kernel,
        out_shape=jax.ShapeDtypeStruct((rows, n), f32),
        grid=(rows // ROW_TILE,),
        in_specs=[pl.BlockSpec((ROW_TILE, d), lambda i: (i, 0)),
                  pl.BlockSpec((1, 1, d), lambda i: (i, 0, 0)),
                  pl.BlockSpec((1, 1, d), lambda i: (i, 0, 0)),
                  pl.BlockSpec((d, n), lambda i: (0, 0))],
        out_specs=pl.BlockSpec((ROW_TILE, n), lambda i: (i, 0)),
        compiler_params=_params("parallel"),
    )(x, shift_t, scale_t, w)


def _conv_kernel(x_ref, w_ref, b_ref, o_ref, pad_ref, *, seq, seg0, n_norm, n_qscale, qscale):
    j = pl.program_id(1)
    halo = 8
    pad_ref[0:halo, :] = jnp.zeros((halo, LANE), f32)
    pad_ref[halo + seq:2 * halo + seq, :] = jnp.zeros((halo, LANE), f32)
    pad_ref[halo:halo + seq, :] = x_ref[0]
    w = w_ref[...]
    bias = b_ref[...]
    scale = jnp.where(j < n_qscale, qscale, 1.0).astype(f32)
    norm_on = jnp.where(j < n_norm, 1.0, 0.0).astype(f32)
    rows = ROW_TILE

    def body(c, carry):
        r0 = pl.multiple_of(c * rows, rows)
        win = pad_ref[pl.ds(r0, rows + 2 * halo), :]
        t = r0 + lax.broadcasted_iota(jnp.int32, (rows, LANE), 0)
        acc = bias + w[2:3] * win[halo:halo + rows]
        for k in (0, 1, 3, 4):
            off = k - CONV_W // 2
            tap = win[halo + off:halo + off + rows]
            same_segment = ((t + off) >= seg0) == (t >= seg0)
            acc = acc + w[k:k + 1] * jnp.where(same_segment, tap, 0.0)
        y = _silu(acc)
        yn = y * lax.rsqrt(jnp.sum(y * y, -1, keepdims=True) + EPS)
        y = (norm_on * yn + (1.0 - norm_on) * y) * scale
        o_ref[0, pl.ds(r0, rows), :] = y
        return carry

    lax.fori_loop(0, seq // rows, body, 0)


def _conv_act(x, w, b, col_blocks, *, seg0, n_norm, n_qscale, qscale):
    bsz, seq, _ = x.shape
    n_out = len(col_blocks)
    first_gap = next((i for i, cb in enumerate(col_blocks) if cb != i), n_out)
    gap = col_blocks[first_gap] - first_gap if first_gap < n_out else 0
    in_map = lambda bi, j: (bi, 0, jnp.where(j < first_gap, j, j + gap))
    kern = functools.partial(_conv_kernel, seq=seq, seg0=seg0, n_norm=n_norm, n_qscale=n_qscale, qscale=qscale)
    return pl.pallas_call(
        kern,
        out_shape=jax.ShapeDtypeStruct((bsz, seq, n_out * LANE), f32),
        grid=(bsz, n_out),
        in_specs=[pl.BlockSpec((1, seq, LANE), in_map),
                  pl.BlockSpec((CONV_W, LANE), lambda bi, j: (0, j)),
                  pl.BlockSpec((1, LANE), lambda bi, j: (0, j))],
        out_specs=pl.BlockSpec((1, seq, LANE), lambda bi, j: (bi, 0, j)),
        scratch_shapes=[pltpu.VMEM((seq + 16, LANE), f32)],
        compiler_params=_params("parallel", "parallel"),
    )(x, w, b)


def _chunk_index(d, n, n_ctx, n_all):
    return n if d == 0 else jnp.where(n < n_ctx, n_ctx - 1 - n, n_all + n_ctx - 1 - n)


def _both_directions(make_specs):
    return [spec for d in range(N_DIR) for spec in make_specs(d)]


class _ScanPart(NamedTuple):
    kernel: Callable
    operands: list
    in_specs: list
    out_shapes: list
    out_specs: list
    scratch_shapes: list


def _run_scans(parts, n_steps):
    def fused(*refs):
        n_in = sum(len(p.operands) for p in parts)
        n_out = sum(len(p.out_shapes) for p in parts)
        ins, outs, scr = refs[:n_in], refs[n_in:n_in + n_out], refs[n_in + n_out:]

        @pl.when(pl.program_id(0) == 0)
        def _():
            for state in scr:
                state[...] = jnp.zeros(state.shape, f32)

        for p in parts:
            p.kernel(*ins[:len(p.operands)], *outs[:len(p.out_shapes)], *scr[:len(p.scratch_shapes)])
            ins, outs, scr = ins[len(p.operands):], outs[len(p.out_shapes):], scr[len(p.scratch_shapes):]

    flat = pl.pallas_call(
        fused,
        out_shape=tuple(s for p in parts for s in p.out_shapes),
        grid=(n_steps,),
        in_specs=[s for p in parts for s in p.in_specs],
        out_specs=tuple(s for p in parts for s in p.out_specs),
        scratch_shapes=[s for p in parts for s in p.scratch_shapes],
        compiler_params=_params("arbitrary"),
    )(*[o for p in parts for o in p.operands])
    results = []
    for p in parts:
        results.append(flat[:len(p.out_shapes)])
        flat = flat[len(p.out_shapes):]
    return results


def _gate_layouts(raw, heads):
    bsz, seq, _ = raw.shape
    r = raw.reshape(bsz, seq // CHUNK, CHUNK, N_DIR, heads)
    return r.transpose(3, 0, 1, 2, 4), r.transpose(3, 0, 1, 4, 2)


def _param_layouts(*ps):
    return jnp.stack(ps, axis=1).astype(f32), jnp.stack(ps, axis=2).astype(f32)


def _gdn_kernel(*refs, bsz):
    n_in = 7
    ins = [refs[d * n_in:(d + 1) * n_in] for d in range(N_DIR)]
    pcol_ref, prow_ref = refs[N_DIR * n_in:N_DIR * n_in + 2]
    o_refs = refs[N_DIR * n_in + 2:N_DIR * n_in + 2 + N_DIR]
    st_ref = refs[-1]
    masks = [_chunk_masks(d) for d in range(N_DIR)]
    eye = masks[0][3].astype(f32)
    row = lax.broadcasted_iota(jnp.int32, (CHUNK, CHUNK), 0)
    col = lax.broadcasted_iota(jnp.int32, (CHUNK, CHUNK), 1)
    blocks = [(row >> sh) == (col >> sh) for sh in (3, 4, 5, 6)]
    off_masks = [blocks[lvl + 1] & ~blocks[lvl] for lvl in range(3)]
    chains = [(d, b, h) for d in range(N_DIR) for b in range(bsz) for h in range(A_HEADS)]
    idx = range(len(chains))
    pc = [pcol_ref[d] for d in range(N_DIR)]
    pr = [prow_ref[d] for d in range(N_DIR)]
    beta_col = [[_sigmoid(ins[d][3][0, b, 0]) for b in range(bsz)] for d in range(N_DIR)]
    g_col = [[-jnp.exp(pc[d][0:1, :]) * _softplus(ins[d][5][0, b, 0] + pc[d][1:2, :]) for b in range(bsz)]
             for d in range(N_DIR)]
    g_row = [[-jnp.exp(pr[d][:, 0:1]) * _softplus(ins[d][6][0, b, 0] + pr[d][:, 1:2]) for b in range(bsz)]
             for d in range(N_DIR)]
    sl = [slice(h * A_DK, (h + 1) * A_DK) for _, _, h in chains]
    q = [ins[d][0][b, :, sl[c]] for c, (d, b, _) in enumerate(chains)]
    k = [ins[d][1][b, :, sl[c]] for c, (d, b, _) in enumerate(chains)]
    v = [ins[d][2][b, :, sl[c]] for c, (d, b, _) in enumerate(chains)]
    cums = [_cum(masks[d][0], masks[d][2], g_col[d][b][:, h:h + 1], g_row[d][b][h:h + 1, :]) for d, b, h in chains]
    decay = [jnp.exp(jnp.where(masks[d][0], cums[c][0] - cums[c][1], -jnp.inf)) for c, (d, _, _) in enumerate(chains)]
    b_col = [beta_col[d][b][:, h:h + 1] for d, b, h in chains]
    kb = [k[c] * b_col[c] for c in idx]
    l_mat = [_dot(kb[c], k[c], NT) * jnp.where(masks[d][1], decay[c], 0.0) for c, (d, _, _) in enumerate(chains)]
    nil = [-jnp.where(blocks[0], l_mat[c], 0.0) for c in idx]
    t_inv = [eye + nil[c] for c in idx]
    for _ in range(2):
        nil = [_dot(nil[c], nil[c]) for c in idx]
        t_inv = [t_inv[c] + _dot(t_inv[c], nil[c]) for c in idx]
    for lvl in range(3):
        left = [_dot(t_inv[c], jnp.where(off_masks[lvl], l_mat[c], 0.0)) for c in idx]
        t_inv = [t_inv[c] - _dot(left[c], t_inv[c]) for c in idx]
    e_col = [jnp.exp(gc) for gc, _, _ in cums]
    uw = [_dot(t_inv[c], jnp.concatenate([v[c] * b_col[c], kb[c] * e_col[c]], axis=1)) for c in idx]
    qk = [_dot(q[c], k[c], NT) * decay[c] for c in idx]
    k_dec_t = [(k[c] * jnp.exp(cums[c][2] - cums[c][0])).T for c in idx]
    s = [st_ref[c] for c in idx]
    wq = [_dot(jnp.concatenate([uw[c][:, A_DV:], q[c] * e_col[c]], axis=0), s[c]) for c in idx]
    v_new = [uw[c][:, :A_DV] - wq[c][:CHUNK] for c in idx]
    out = [wq[c][CHUNK:] + _dot(qk[c], v_new[c]) for c in idx]
    s_new = [jnp.exp(cums[c][2]) * s[c] + _dot(k_dec_t[c], v_new[c]) for c in idx]
    for c, (d, b, _) in enumerate(chains):
        o_refs[d][b, :, sl[c]] = out[c]
        st_ref[c] = s_new[c]


def _gdn(act, beta_raw, decay_raw, a_log, dt_bias, n_ctx):
    bsz, seq, _ = act.shape
    n_all = seq // CHUNK
    hw = A_HEADS * A_DK
    bcol, brow = _gate_layouts(beta_raw, A_HEADS)
    dcol, drow = _gate_layouts(decay_raw, A_HEADS)
    pcol, prow = _param_layouts(a_log, dt_bias)
    ci = functools.partial(_chunk_index, n_ctx=n_ctx, n_all=n_all)

    def specs(d):
        tok = lambda blk: pl.BlockSpec((bsz, CHUNK, hw), lambda n: (0, ci(d, n), blk))
        gcol = pl.BlockSpec((1, bsz, 1, CHUNK, A_HEADS), lambda n: (d, 0, ci(d, n), 0, 0))
        grow = pl.BlockSpec((1, bsz, 1, A_HEADS, CHUNK), lambda n: (d, 0, ci(d, n), 0, 0))
        return [tok(0), tok(1), tok(2), gcol, grow, gcol, grow]

    out = jax.ShapeDtypeStruct((bsz, seq, hw), f32)
    return _ScanPart(
        kernel=functools.partial(_gdn_kernel, bsz=bsz),
        operands=[act, act, act, bcol, brow, dcol, drow] * N_DIR + [pcol, prow],
        in_specs=_both_directions(specs) + [pl.BlockSpec((N_DIR, 2, A_HEADS), lambda n: (0, 0, 0)),
                                            pl.BlockSpec((N_DIR, A_HEADS, 2), lambda n: (0, 0, 0))],
        out_shapes=[out] * N_DIR,
        out_specs=[pl.BlockSpec((bsz, CHUNK, hw), lambda n, d=d: (0, ci(d, n), 0)) for d in range(N_DIR)],
        scratch_shapes=[pltpu.VMEM((N_DIR * bsz * A_HEADS, A_DK, A_DV), f32)])


def _ssd_kernel(*refs, bsz):
    n_in = 5
    ins = [refs[d * n_in:(d + 1) * n_in] for d in range(N_DIR)]
    pcol_ref, prow_ref = refs[N_DIR * n_in:N_DIR * n_in + 2]
    o_refs = refs[N_DIR * n_in + 2:N_DIR * n_in + 2 + N_DIR]
    st_ref = refs[-1]
    masks = [_chunk_masks(d) for d in range(N_DIR)]
    rep = B_HEADS // B_GROUPS
    streams = [(d, b) for d in range(N_DIR) for b in range(bsz)]
    groups = [(s, g) for s in range(len(streams)) for g in range(B_GROUPS)]
    heads = [(s, h) for s in range(len(streams)) for h in range(B_HEADS)]
    gsl = [slice(g * B_STATE, (g + 1) * B_STATE) for _, g in groups]
    hsl = [slice(h * B_HEADDIM, (h + 1) * B_HEADDIM) for _, h in heads]
    pc = [pcol_ref[d] for d, _ in streams]
    pr = [prow_ref[d] for d, _ in streams]
    dt_col = [_softplus(ins[d][3][0, b, 0] + pc[s][1:2, :]) for s, (d, b) in enumerate(streams)]
    dt_row = [_softplus(ins[d][4][0, b, 0] + pr[s][:, 1:2]) for s, (d, b) in enumerate(streams)]
    da_col = [dt_col[s] * (-jnp.exp(pc[s][0:1, :])) for s in range(len(streams))]
    da_row = [dt_row[s] * (-jnp.exp(pr[s][:, 0:1])) for s in range(len(streams))]
    bm = [ins[streams[s][0]][1][streams[s][1], :, gsl[i]] for i, (s, _) in enumerate(groups)]
    cm = [ins[streams[s][0]][2][streams[s][1], :, gsl[i]] for i, (s, _) in enumerate(groups)]
    hs = [st_ref[i] for i in range(len(groups))]
    cb = [_dot(cm[i], bm[i], NT) for i in range(len(groups))]
    y_off = [_dot(cm[i], hs[i]) for i in range(len(groups))]
    bm_t = [bm[i].T for i in range(len(groups))]
    m_incl = [masks[streams[s][0]][0] for s, _ in heads]
    m_incl_t = [masks[streams[s][0]][2] for s, _ in heads]
    cums = [_cum(m_incl[j], m_incl_t[j], da_col[s][:, h:h + 1], da_row[s][h:h + 1, :]) for j, (s, h) in enumerate(heads)]
    seg = [jnp.exp(jnp.where(m_incl[j], cums[j][0] - cums[j][1], -jnp.inf)) for j in range(len(heads))]
    xdt = [ins[streams[s][0]][0][streams[s][1], :, hsl[j]] * dt_col[s][:, h:h + 1] for j, (s, h) in enumerate(heads)]
    y_diag = [_dot(cb[j // rep] * seg[j], xdt[j]) for j in range(len(heads))]
    for j, (s, h) in enumerate(heads):
        r = h % rep
        d, b = streams[s]
        o_refs[d][b, :, hsl[j]] = y_diag[j] + jnp.exp(cums[j][0]) * y_off[j // rep][:, r * B_HEADDIM:(r + 1) * B_HEADDIM]
    xdt_end = [xdt[j] * jnp.exp(cums[j][2] - cums[j][0]) for j in range(len(heads))]
    keep = [jnp.broadcast_to(jnp.exp(cums[j][2]), (1, B_HEADDIM)) for j in range(len(heads))]
    upd = [_dot(bm_t[i], jnp.concatenate(xdt_end[i * rep:(i + 1) * rep], axis=1)) for i in range(len(groups))]
    for i in range(len(groups)):
        st_ref[i] = jnp.concatenate(keep[i * rep:(i + 1) * rep], axis=1) * hs[i] + upd[i]


def _ssd(act, dt_raw, a_log, dt_bias, n_ctx):
    bsz, seq, _ = act.shape
    n_all = seq // CHUNK
    dinner = B_HEADS * B_HEADDIM
    gw = B_GROUPS * B_STATE
    tcol, trow = _gate_layouts(dt_raw, B_HEADS)
    pcol, prow = _param_layouts(a_log, dt_bias)
    ci = functools.partial(_chunk_index, n_ctx=n_ctx, n_all=n_all)
    x_off = (A_HEADS * (2 * A_DK + A_DV)) // dinner
    bm_off = (A_HEADS * (2 * A_DK + A_DV) + dinner) // gw

    def specs(d):
        return [pl.BlockSpec((bsz, CHUNK, dinner), lambda n: (0, ci(d, n), x_off)),
                pl.BlockSpec((bsz, CHUNK, gw), lambda n: (0, ci(d, n), bm_off)),
                pl.BlockSpec((bsz, CHUNK, gw), lambda n: (0, ci(d, n), bm_off + 1)),
                pl.BlockSpec((1, bsz, 1, CHUNK, B_HEADS), lambda n: (d, 0, ci(d, n), 0, 0)),
                pl.BlockSpec((1, bsz, 1, B_HEADS, CHUNK), lambda n: (d, 0, ci(d, n), 0, 0))]

    out = jax.ShapeDtypeStruct((bsz, seq, dinner), f32)
    return _ScanPart(
        kernel=functools.partial(_ssd_kernel, bsz=bsz),
        operands=[act, act, act, tcol, trow] * N_DIR + [pcol, prow],
        in_specs=_both_directions(specs) + [pl.BlockSpec((N_DIR, 2, B_HEADS), lambda n: (0, 0, 0)),
                                            pl.BlockSpec((N_DIR, B_HEADS, 2), lambda n: (0, 0, 0))],
        out_shapes=[out] * N_DIR,
        out_specs=[pl.BlockSpec((bsz, CHUNK, dinner), lambda n, d=d: (0, ci(d, n), 0)) for d in range(N_DIR)],
        scratch_shapes=[pltpu.VMEM((N_DIR * bsz * B_GROUPS, B_STATE, (B_HEADS // B_GROUPS) * B_HEADDIM), f32)])


def _mlstm_kernel(*refs, bsz):
    n_in = 7
    ins = [refs[d * n_in:(d + 1) * n_in] for d in range(N_DIR)]
    pcol_ref, prow_ref = refs[N_DIR * n_in:N_DIR * n_in + 2]
    o_refs = refs[N_DIR * n_in + 2:N_DIR * n_in + 2 + N_DIR]
    c_ref, n_ref, m_ref = refs[-3:]
    masks = [_chunk_masks(d) for d in range(N_DIR)]
    streams = [(d, b) for d in range(N_DIR) for b in range(bsz)]
    chains = [(s, h) for s in range(len(streams)) for h in range(C_HEADS)]
    idx = range(len(chains))
    pc = [pcol_ref[d] for d, _ in streams]
    pr = [prow_ref[d] for d, _ in streams]
    li_col = [ins[d][3][0, b, 0] + pc[s][0:1, :] for s, (d, b) in enumerate(streams)]
    li_row = [ins[d][4][0, b, 0] + pr[s][:, 0:1] for s, (d, b) in enumerate(streams)]
    lf_col = [-_softplus(-(ins[d][5][0, b, 0] + pc[s][1:2, :])) for s, (d, b) in enumerate(streams)]
    lf_row = [-_softplus(-(ins[d][6][0, b, 0] + pr[s][:, 1:2])) for s, (d, b) in enumerate(streams)]
    q = [ins[streams[s][0]][0][streams[s][1], :, h * C_DK:(h + 1) * C_DK] for s, h in chains]
    k = [ins[streams[s][0]][1][streams[s][1], :, h * C_DK:(h + 1) * C_DK] for s, h in chains]
    v = [ins[streams[s][0]][2][streams[s][1], :, h * C_DV:(h + 1) * C_DV] for s, h in chains]
    m_incl = [masks[streams[s][0]][0] for s, _ in chains]
    m_incl_t = [masks[streams[s][0]][2] for s, _ in chains]
    cums = [_cum(m_incl[c], m_incl_t[c], lf_col[s][:, h:h + 1], lf_row[s][h:h + 1, :]) for c, (s, h) in enumerate(chains)]
    logw_intra = [jnp.where(m_incl[c], cums[c][0] - cums[c][1] + li_row[s][h:h + 1, :], -jnp.inf)
                  for c, (s, h) in enumerate(chains)]
    m_prev = [m_ref[c] for c in idx]
    logw_inter = [cums[c][0] + m_prev[c] for c in idx]
    m_t = [jnp.maximum(logw_inter[c], jnp.max(logw_intra[c], axis=1, keepdims=True)) for c in idx]
    qk = [_dot(q[c], k[c], NT) for c in idx]
    c_mem = [c_ref[c] for c in idx]
    n_mem = [n_ref[c] for c in idx]
    qc = [_dot(q[c], c_mem[c]) for c in idx]
    s = [qk[c] * jnp.exp(logw_intra[c] - m_t[c]) for c in idx]
    w_inter = [jnp.exp(logw_inter[c] - m_t[c]) for c in idx]
    num = [_dot(s[c], v[c]) + w_inter[c] * qc[c] for c in idx]
    den = [jnp.sum(s[c], axis=1, keepdims=True) + w_inter[c] * jnp.sum(q[c] * n_mem[c], axis=1, keepdims=True)
           for c in idx]
    for c, (s, h) in enumerate(chains):
        d, b = streams[s]
        o_refs[d][b, :, h * C_DV:(h + 1) * C_DV] = num[c] / jnp.maximum(jnp.abs(den[c]), jnp.exp(-m_t[c]))
    logw_end = [cums[c][2] - cums[c][0] + li_col[s][:, h:h + 1] for c, (s, h) in enumerate(chains)]
    m_new = [jnp.maximum(cums[c][2] + m_prev[c], jnp.max(logw_end[c], axis=0, keepdims=True)) for c in idx]
    kw = [k[c] * jnp.exp(logw_end[c] - m_new[c]) for c in idx]
    keep = [jnp.exp(cums[c][2] + m_prev[c] - m_new[c]) for c in idx]
    kv = [_dot(kw[c].T, v[c]) for c in idx]
    for c in idx:
        c_ref[c] = keep[c] * c_mem[c] + kv[c]
        n_ref[c] = keep[c] * n_mem[c] + jnp.sum(kw[c], axis=0, keepdims=True)
        m_ref[c] = m_new[c]


def _mlstm(act, proj, i_raw, f_raw, i_bias, f_bias, n_ctx):
    bsz, seq, _ = act.shape
    n_all = seq // CHUNK
    qw = C_HEADS * C_DK
    vw = C_HEADS * C_DV
    icol, irow = _gate_layouts(i_raw, C_HEADS)
    fcol, frow = _gate_layouts(f_raw, C_HEADS)
    pcol, prow = _param_layouts(i_bias, f_bias)
    ci = functools.partial(_chunk_index, n_ctx=n_ctx, n_all=n_all)

    def specs(d):
        gcol = pl.BlockSpec((1, bsz, 1, CHUNK, C_HEADS), lambda n: (d, 0, ci(d, n), 0, 0))
        grow = pl.BlockSpec((1, bsz, 1, C_HEADS, CHUNK), lambda n: (d, 0, ci(d, n), 0, 0))
        return [pl.BlockSpec((bsz, CHUNK, qw), lambda n: (0, ci(d, n), 0)),
                pl.BlockSpec((bsz, CHUNK, qw), lambda n: (0, ci(d, n), 1)),
                pl.BlockSpec((bsz, CHUNK, vw), lambda n: (0, ci(d, n), 2 * qw // vw)),
                gcol, grow, gcol, grow]

    out = jax.ShapeDtypeStruct((bsz, seq, vw), f32)
    n_chain = N_DIR * bsz * C_HEADS
    return _ScanPart(
        kernel=functools.partial(_mlstm_kernel, bsz=bsz),
        operands=[act, act, proj, icol, irow, fcol, frow] * N_DIR + [pcol, prow],
        in_specs=_both_directions(specs) + [pl.BlockSpec((N_DIR, 2, C_HEADS), lambda n: (0, 0, 0)),
                                            pl.BlockSpec((N_DIR, C_HEADS, 2), lambda n: (0, 0, 0))],
        out_shapes=[out] * N_DIR,
        out_specs=[pl.BlockSpec((bsz, CHUNK, vw), lambda n, d=d: (0, ci(d, n), 0)) for d in range(N_DIR)],
        scratch_shapes=[pltpu.VMEM((n_chain, C_DK, C_DV), f32),
                        pltpu.VMEM((n_chain, 1, C_DK), f32),
                        pltpu.VMEM((n_chain, 1, 1), f32)])


def _merge_even_kernel(x_ref, af_ref, ar_ref, bf_ref, br_ref, za_ref, zb_ref, xs_ref, na_ref, nb_ref, dsk_ref,
                       w_ref, g_ref, lg_ref, lb_ref, o_ref):
    a = af_ref[...] + ar_ref[...]
    za = za_ref[...]
    na = na_ref[...]
    parts = []
    for h in range(A_HEADS):
        sl = slice(h * A_DV, (h + 1) * A_DV)
        parts.append(_rms(a[:, sl], na) * _silu(za[:, sl]))
    yb = (bf_ref[...] + br_ref[...] + dsk_ref[...] * xs_ref[...]) * _silu(zb_ref[...])
    gw = (B_HEADS * B_HEADDIM) // B_GROUPS
    nb = nb_ref[...]
    for g in range(B_GROUPS):
        sl = slice(g * gw, (g + 1) * gw)
        parts.append(_rms(yb[:, sl], nb[:, sl]))
    y = jnp.concatenate(parts, axis=1).astype(bf16)
    z = ALPHA * x_ref[...] + g_ref[0] * _dot(y, w_ref[...])
    o_ref[...] = _layer_norm(z, lg_ref[...], lb_ref[...])


def _merge_even(x, o_a, o_b, proj, act, norm_a, norm_b, d_skip, w_out, gate_t, ln_g, ln_b):
    rows, d = x.shape
    aw = A_HEADS * A_DV
    bw = B_HEADS * B_HEADDIM
    row = lambda w_, blk: pl.BlockSpec((ROW_TILE, w_), lambda i: (i, blk))
    vec = lambda w_: pl.BlockSpec((1, w_), lambda i: (0, 0))
    qkv = A_HEADS * (2 * A_DK + A_DV)
    return pl.pallas_call(
        _merge_even_kernel,
        out_shape=jax.ShapeDtypeStruct((rows, d), f32),
        grid=(rows // ROW_TILE,),
        in_specs=[row(d, 0), row(aw, 0), row(aw, 0), row(bw, 0), row(bw, 0),
                  row(aw, qkv // aw), row(bw, (qkv + aw) // bw), row(bw, qkv // bw),
                  vec(A_DV), vec(bw), vec(bw),
                  pl.BlockSpec((aw + bw, d), lambda i: (0, 0)),
                  pl.BlockSpec((1, 1, d), lambda i: (i, 0, 0)),
                  vec(d), vec(d)],
        out_specs=row(d, 0),
        compiler_params=_params("parallel"),
    )(x, *o_a, *o_b, proj, proj, act, norm_a.reshape(1, -1), norm_b.reshape(1, -1),
      jnp.repeat(d_skip, B_HEADDIM).reshape(1, -1), w_out, gate_t, ln_g.reshape(1, -1), ln_b.reshape(1, -1))


def _merge_odd_kernel(hf_ref, hr_ref, o_ref_in, nc_ref, w_ref, out_ref):
    hsum = hf_ref[...] + hr_ref[...]
    o = o_ref_in[...]
    nc = nc_ref[...]
    parts = []
    for h in range(C_HEADS):
        sl = slice(h * C_DV, (h + 1) * C_DV)
        parts.append(_rms(hsum[:, sl], nc) * _sigmoid(o[:, sl]))
    y = jnp.concatenate(parts, axis=1).astype(bf16)
    out_ref[...] = _dot(y, w_ref[...])


def _merge_odd(h, proj, norm_c, w_out, tiles_per_batch, ctx_tiles, bsz):
    vw = C_HEADS * C_DV
    lat_tiles = tiles_per_batch - ctx_tiles
    src = lambda i: i + ctx_tiles * (i // lat_tiles + 1)
    d = w_out.shape[1]
    return pl.pallas_call(
        _merge_odd_kernel,
        out_shape=jax.ShapeDtypeStruct((bsz * lat_tiles * ROW_TILE, d), f32),
        grid=(bsz * lat_tiles,),
        in_specs=[pl.BlockSpec((ROW_TILE, vw), lambda i: (src(i), 0)),
                  pl.BlockSpec((ROW_TILE, vw), lambda i: (src(i), 0)),
                  pl.BlockSpec((ROW_TILE, vw), lambda i: (src(i), 2 * C_HEADS * C_DK // vw + 1)),
                  pl.BlockSpec((1, C_DV), lambda i: (0, 0)),
                  pl.BlockSpec((vw, d), lambda i: (0, 0))],
        out_specs=pl.BlockSpec((ROW_TILE, d), lambda i: (i, 0)),
        compiler_params=_params("parallel"),
    )(*h, proj, norm_c.reshape(1, -1), w_out)


def _resid_ln_kernel(x_ref, y_ref, g_ref, lg_ref, lb_ref, o_ref):
    o_ref[...] = _layer_norm(ALPHA * x_ref[...] + g_ref[0] * y_ref[...], lg_ref[...], lb_ref[...])


def _resid_ln(x, y, gate_t, ln_g, ln_b):
    rows, d = x.shape
    return pl.pallas_call(
        _resid_ln_kernel,
        out_shape=jax.ShapeDtypeStruct((rows, d), f32),
        grid=(rows // ROW_TILE,),
        in_specs=[pl.BlockSpec((ROW_TILE, d), lambda i: (i, 0)), pl.BlockSpec((ROW_TILE, d), lambda i: (i, 0)),
                  pl.BlockSpec((1, 1, d), lambda i: (i, 0, 0)),
                  pl.BlockSpec((1, d), lambda i: (0, 0)),
                  pl.BlockSpec((1, d), lambda i: (0, 0))],
        out_specs=pl.BlockSpec((ROW_TILE, d), lambda i: (i, 0)),
        compiler_params=_params("parallel"),
    )(x, y, gate_t, ln_g.reshape(1, -1), ln_b.reshape(1, -1))


def _oddeven_merge(lo, hi, r):
    step = r * 2
    if step < hi - lo:
        yield from _oddeven_merge(lo, hi, step)
        yield from _oddeven_merge(lo + r, hi, step)
        yield from [(i, i + r) for i in range(lo + r, hi - r, step)]
    else:
        yield (lo, lo + r)


def _oddeven_sort(lo, hi):
    if hi - lo >= 1:
        mid = lo + (hi - lo) // 2
        yield from _oddeven_sort(lo, mid)
        yield from _oddeven_sort(mid + 1, hi)
        yield from _oddeven_merge(lo, hi, 1)


def _exchange(a, i, j):
    a[i], a[j] = jnp.maximum(a[i], a[j]), jnp.minimum(a[i], a[j])


def _top_sorted(slabs):
    a = list(slabs)
    for i, j in _oddeven_sort(0, len(a) - 1):
        _exchange(a, i, j)
    for shift in (4, 2, 1):
        other = [pltpu.roll(x, shift, axis=0) for x in a]
        if len(a) < PEER_TOPK:
            a = a + other[::-1]
        else:
            a = [jnp.maximum(a[k], other[PEER_TOPK - 1 - k]) for k in range(PEER_TOPK)]
        dist = PEER_TOPK // 2
        while dist >= 1:
            for i in range(PEER_TOPK):
                if i & dist == 0:
                    _exchange(a, i, i + dist)
            dist //= 2
    return a


def _next_largest(slabs, kth):
    count = sum(jnp.where(s >= kth, 1.0, 0.0) for s in slabs)
    below = functools.reduce(jnp.maximum, [jnp.where(s < kth, s, -jnp.inf) for s in slabs])
    count = jnp.sum(count, axis=0, keepdims=True)
    below = jnp.max(below, axis=0, keepdims=True)
    return jnp.where(count > PEER_TOPK, kth, below)


def _top_values(s):
    slabs = [s[i:i + 8] for i in range(0, s.shape[0], 8)]
    if len(slabs) < 8:
        slabs = slabs + [jnp.full_like(slabs[0], -jnp.inf)] * (8 - len(slabs))
    top = [t[0:1] for t in _top_sorted(slabs)]
    return top + [_next_largest(slabs, top[-1])]


def _route_kernel(x_ref, sh_ref, sc_ref, w_ref, k_ref, xm_ref, thr_ref, s2_ref, e1_ref, e2_ref, sc_scr, cand_ref):
    for t in range(ROUTE_TILE // ROW_TILE):
        rows = slice(t * ROW_TILE, (t + 1) * ROW_TILE)
        xm_ref[rows, :] = (x_ref[rows, :] * (1.0 + sc_ref[t]) + sh_ref[t]).astype(bf16)
    half = PEER_DKEY // 2
    n_top = PEER_TOPK + 1
    pairs = [(p, r) for p in range(n_top) for r in range(n_top) if (p + 1) * (r + 1) <= n_top]
    cand_ref[...] = jnp.full(cand_ref.shape, -jnp.inf, f32)

    def scores(h, sc_scr):
        wsl = pl.ds(pl.multiple_of(h * PEER_DKEY, PEER_DKEY), PEER_DKEY)
        q = _dot(xm_ref[...], w_ref[:, wsl])
        for c in range(2):
            sc_scr[c] = _dot(k_ref[h, c], q[:, c * half:(c + 1) * half].astype(bf16), NT)

    def select(h, sc_scr):
        for lb in range(ROUTE_TILE // LANE):
            lanes = slice(lb * LANE, (lb + 1) * LANE)
            s1 = sc_scr[0, :, lanes]
            s2 = sc_scr[1, :, lanes]
            t1 = _top_values(s1)
            t2 = _top_values(s2)
            for i, (p, r) in enumerate(pairs):
                cand_ref[i:i + 1, lanes] = t1[p] + t2[r]
            best = _top_values(cand_ref[:, lanes])
            z = sum(jnp.exp(bv - best[0]) for bv in best[:PEER_TOPK])
            tau = 0.5 * (best[PEER_TOPK - 1] + best[PEER_TOPK])
            thr_ref[h, :, lanes] = tau - s1
            s2_ref[h, :, lanes] = s2
            e1_ref[h, :, lanes] = jnp.exp(s1 - t1[0]) / z
            e2_ref[h, :, lanes] = jnp.exp(s2 - t2[0])

    def head(h, carry):
        scores(h, sc_scr)
        select(h, sc_scr)
        return carry

    lax.fori_loop(0, PEER_HEADS, head, 0)


def _route(x, shift_t, scale_t, wq, keys):
    rows, d = x.shape
    nq = wq.shape[1]
    half = PEER_DKEY // 2
    score = jax.ShapeDtypeStruct((PEER_HEADS, PEER_NKEYS, rows), f32)
    score_spec = pl.BlockSpec((PEER_HEADS, PEER_NKEYS, ROUTE_TILE), lambda i: (0, 0, i))
    key_spec = pl.BlockSpec((PEER_HEADS, 2, PEER_NKEYS, half), lambda i: (0, 0, 0, 0))
    mods = ROUTE_TILE // ROW_TILE
    return pl.pallas_call(
        _route_kernel,
        out_shape=(jax.ShapeDtypeStruct((rows, d), bf16), score, score, score, score),
        grid=(rows // ROUTE_TILE,),
        in_specs=[pl.BlockSpec((ROUTE_TILE, d), lambda i: (i, 0)),
                  pl.BlockSpec((mods, 1, d), lambda i: (i, 0, 0)),
                  pl.BlockSpec((mods, 1, d), lambda i: (i, 0, 0)),
                  pl.BlockSpec((d, nq), lambda i: (0, 0)),
                  key_spec],
        out_specs=(pl.BlockSpec((ROUTE_TILE, d), lambda i: (i, 0)),
                   score_spec, score_spec, score_spec, score_spec),
        scratch_shapes=[pltpu.VMEM((2, PEER_NKEYS, ROUTE_TILE), f32),
                        pltpu.VMEM((N_CAND_ROWS, ROUTE_TILE), f32)],
        compiler_params=_params("parallel"),
    )(x, shift_t, scale_t, wq, keys)


def _expert_kernel(xm_ref, u_ref, vt_ref, thr_ref, e1_ref, s2_ref, e2_ref, x_ref, g_ref, lg_ref, lb_ref,
                   o_ref, acc_ref, a_scr, w_scr):
    j = pl.program_id(1)

    @pl.when(j == 0)
    def _():
        acc_ref[...] = jnp.zeros(acc_ref.shape, f32)

    a_scr[...] = _dot(u_ref[...], xm_ref[...], NT)
    sqrt_half = math.sqrt(0.5)
    row_chunks = PEER_NKEYS // GATE_ROWS
    tiles_per_group = (EXPERT_TOK // LANE) * row_chunks

    def tile(r, carry, i1_0):
        lanes = pl.ds(pl.multiple_of((r // row_chunks) * LANE, LANE), LANE)
        r0 = pl.multiple_of((r % row_chunks) * GATE_ROWS, GATE_ROWS)
        i2 = pl.ds(r0, GATE_ROWS)
        gate = [jnp.zeros((GATE_ROWS, LANE), f32) for _ in range(I1_TILE)]
        for h in range(PEER_HEADS):
            s2 = s2_ref[h, i2, lanes]
            e2 = e2_ref[h, i2, lanes]
            for il in range(I1_TILE):
                i1 = i1_0 + il
                w = e1_ref[h, i1:i1 + 1, lanes] * e2
                gate[il] = gate[il] + jnp.where(s2 >= thr_ref[h, i1:i1 + 1, lanes], w, 0.0)
        for il in range(I1_TILE):
            rows = pl.ds((i1_0 + il) * PEER_NKEYS + r0, GATE_ROWS)
            a = a_scr[rows, lanes]
            act = 0.5 * a * (1.0 + lax.erf(a * sqrt_half))
            w_scr[rows, lanes] = (gate[il] * act).astype(bf16)
        return carry

    for i1_0 in range(0, EXPERT_BLK // PEER_NKEYS, I1_TILE):
        lax.fori_loop(0, tiles_per_group, functools.partial(tile, i1_0=i1_0), 0)
    acc_ref[...] += _dot(vt_ref[...], w_scr[...])

    @pl.when(j == pl.num_programs(1) - 1)
    def _():
        for t in range(EXPERT_TOK // ROW_TILE):
            rows = slice(t * ROW_TILE, (t + 1) * ROW_TILE)
            z = ALPHA * x_ref[rows, :] + g_ref[t] * acc_ref[:, rows].T
            o_ref[rows, :] = _layer_norm(z, lg_ref[...], lb_ref[...])


def _experts_residual(x, xm, u_bf, vt_bf, thr, e1, s2, e2, gate_t, ln_g, ln_b):
    rows, d = xm.shape
    n_exp = u_bf.shape[0]
    i1_per_blk = EXPERT_BLK // PEER_NKEYS
    sel = pl.BlockSpec((PEER_HEADS, i1_per_blk, EXPERT_TOK), lambda i, j: (0, j, i))
    full = pl.BlockSpec((PEER_HEADS, PEER_NKEYS, EXPERT_TOK), lambda i, j: (0, 0, i))
    tok = pl.BlockSpec((EXPERT_TOK, d), lambda i, j: (i, 0))
    vec = pl.BlockSpec((1, d), lambda i, j: (0, 0))
    return pl.pallas_call(
        _expert_kernel,
        out_shape=jax.ShapeDtypeStruct((rows, d), f32),
        grid=(rows // EXPERT_TOK, n_exp // EXPERT_BLK),
        in_specs=[tok,
                  pl.BlockSpec((EXPERT_BLK, d), lambda i, j: (j, 0)),
                  pl.BlockSpec((d, EXPERT_BLK), lambda i, j: (0, j)),
                  sel, sel, full, full,
                  tok, pl.BlockSpec((EXPERT_TOK // ROW_TILE, 1, d), lambda i, j: (i, 0, 0)), vec, vec],
        out_specs=tok,
        scratch_shapes=[pltpu.VMEM((d, EXPERT_TOK), f32),
                        pltpu.VMEM((EXPERT_BLK, EXPERT_TOK), f32),
                        pltpu.VMEM((EXPERT_BLK, EXPERT_TOK), bf16)],
        compiler_params=_params("parallel", "arbitrary"),
    )(xm, u_bf, vt_bf, thr, e1, s2, e2, x, gate_t, ln_g.reshape(1, -1), ln_b.reshape(1, -1))


def _peer_residual(x, shift_t, scale_t, gate_t, peer_q, peer_keys, peer_u, peer_v, ln_g, ln_b):
    xm, thr, s2, e1, e2 = _route(x, shift_t, scale_t, peer_q.astype(bf16), peer_keys.astype(bf16))
    return _experts_residual(x, xm, peer_u.astype(bf16), peer_v.T.astype(bf16), thr, e1, s2, e2, gate_t, ln_g, ln_b)


def _grid_sincos(n_tokens, d):
    n_rows = n_tokens // GRID_W
    n_freq = d // 4
    freq = jnp.exp(-math.log(POS_BASE) * jnp.arange(n_freq, dtype=f32) / n_freq)[None, :]
    row = jnp.arange(n_rows, dtype=f32)[:, None] * freq
    col = jnp.arange(GRID_W, dtype=f32)[:, None] * freq
    row_part = jnp.concatenate([jnp.sin(row), jnp.cos(row)], -1)[:, None, :]
    col_part = jnp.concatenate([jnp.sin(col), jnp.cos(col)], -1)[None, :, :]
    shape = (n_rows, GRID_W, 2 * n_freq)
    table = jnp.concatenate([jnp.broadcast_to(row_part, shape), jnp.broadcast_to(col_part, shape)], -1)
    return table.reshape(n_tokens, d)


def _tile_rows(mod, src):
    d = mod.shape[1] // 6
    rows = jnp.concatenate([jnp.broadcast_to(mod[r:r + 1], (n, mod.shape[1])) for r, n in src], axis=0)
    return [rows[:, None, i * d:(i + 1) * d] for i in range(6)]


def _pad_cols(w, n):
    return jnp.pad(w, ((0, 0), (0, n - w.shape[1])))


def kernel(x, c, ctx, c_ctx, l0_ada_w, l0_ada_b, l0_ln1_g, l0_ln1_b, l0_w_in, l0_conv_a, l0_a_log_a, l0_dt_bias_a, l0_norm_a, l0_conv_b, l0_conv_b_bias, l0_a_log_b, l0_dt_bias_b, l0_d_skip_b, l0_norm_b, l0_w_out, l0_ln2_g, l0_ln2_b, l0_peer_q, l0_peer_keys, l0_peer_u, l0_peer_v, l1_ada_w, l1_ada_b, l1_ln1_g, l1_ln1_b, l1_w_in, l1_conv_c, l1_i_bias, l1_f_bias, l1_norm_c, l1_w_out, l1_ln2_g, l1_ln2_b, l1_peer_q, l1_peer_keys, l1_peer_u, l1_peer_v):
    bsz, n_lat, d = x.shape
    n_ctx_tok = ctx.shape[1]
    rows_grid = n_lat // GRID_W
    seq = n_ctx_tok + n_lat
    n_ctx = n_ctx_tok // CHUNK
    tiles_b = seq // ROW_TILE
    ctx_tiles = n_ctx_tok // ROW_TILE
    lat_tiles = n_lat // ROW_TILE

    cond = jnp.concatenate([c, c_ctx[None, :], jnp.zeros((8 - bsz - 1, d), f32)], axis=0)
    src_all = [run for b in range(bsz) for run in ((bsz, ctx_tiles), (b, lat_tiles))]

    x_lat = x + _grid_sincos(n_lat, d).astype(x.dtype)
    xs_all = jnp.concatenate([ctx, x_lat], axis=1).reshape(bsz * seq, d)

    sh1, sc1, g1, sh2, sc2, g2 = _tile_rows(_ada(cond, l0_ada_w, l0_ada_b), src_all)
    qkv_w = A_HEADS * (2 * A_DK + A_DV)
    az = A_HEADS * A_DV
    ag = N_DIR * A_HEADS
    bd = B_HEADS * B_HEADDIM
    bx = bd + 2 * B_GROUPS * B_STATE
    bt = N_DIR * B_HEADS
    o0 = np.cumsum([0, qkv_w, az, ag, ag, bd, bx, bt])
    w_qkv, w_za, w_beta, w_dec, w_zb, w_xbc, w_dt = (l0_w_in[:, o0[i]:o0[i + 1]] for i in range(7))
    main_w = qkv_w + az + bd + bx
    w0 = _pad_cols(jnp.concatenate([w_qkv, w_za, w_zb, w_xbc, w_beta, w_dec, w_dt], axis=1), main_w + LANE)
    proj = _proj(xs_all, sh1, sc1, w0.astype(bf16)).reshape(bsz, seq, main_w + LANE)
    conv_w = jnp.concatenate([l0_conv_a, l0_conv_b], axis=1)
    conv_b = jnp.concatenate([jnp.zeros((qkv_w,), f32), l0_conv_b_bias]).reshape(1, -1)
    xbc_blk = (qkv_w + az + bd) // LANE
    col_blocks = list(range(qkv_w // LANE)) + list(range(xbc_blk, xbc_blk + bx // LANE))
    act = _conv_act(proj, conv_w, conv_b, col_blocks, seg0=n_ctx_tok,
                    n_norm=2 * A_HEADS, n_qscale=A_HEADS, qscale=A_DK ** -0.5)
    gates = proj[:, :, main_w:]
    o_a, o_b = _run_scans([_gdn(act, gates[..., :ag], gates[..., ag:2 * ag], l0_a_log_a, l0_dt_bias_a, n_ctx),
                           _ssd(act, gates[..., 2 * ag:2 * ag + bt], l0_a_log_b, l0_dt_bias_b, n_ctx)],
                          seq // CHUNK)
    x1 = _merge_even(xs_all, [o.reshape(bsz * seq, az) for o in o_a], [o.reshape(bsz * seq, bd) for o in o_b],
                     proj.reshape(bsz * seq, -1), act.reshape(bsz * seq, -1),
                     l0_norm_a, l0_norm_b, l0_d_skip_b, l0_w_out.astype(bf16), g1, l0_ln1_g, l0_ln1_b)
    x2 = _peer_residual(x1, sh2, sc2, g2, l0_peer_q, l0_peer_keys, l0_peer_u, l0_peer_v, l0_ln2_g, l0_ln2_b)

    sh1, sc1, g1, sh2, sc2, g2 = _tile_rows(_ada(cond, l1_ada_w, l1_ada_b), src_all)
    x2 = x2.reshape(bsz, seq, d)
    x2_ctx, x2_lat = x2[:, :n_ctx_tok], x2[:, n_ctx_tok:]
    lat_cm = x2_lat.reshape(bsz, rows_grid, GRID_W, d).transpose(0, 2, 1, 3).reshape(bsz, n_lat, d)
    xs1 = jnp.concatenate([x2_ctx, lat_cm], axis=1).reshape(bsz * seq, d)
    qk_w = 2 * C_HEADS * C_DK
    vw = C_HEADS * C_DV
    w1 = _pad_cols(l1_w_in, qk_w + 2 * vw + LANE)
    proj1 = _proj(xs1, sh1, sc1, w1.astype(bf16)).reshape(bsz, seq, -1)
    act1 = _conv_act(proj1, l1_conv_c, jnp.zeros((1, qk_w), f32), list(range(qk_w // LANE)), seg0=n_ctx_tok,
                     n_norm=0, n_qscale=C_HEADS, qscale=C_DK ** -0.5)
    gates1 = proj1[:, :, qk_w + 2 * vw:]
    cg = N_DIR * C_HEADS
    (h1,) = _run_scans([_mlstm(act1, proj1, gates1[..., :cg], gates1[..., cg:2 * cg], l1_i_bias, l1_f_bias, n_ctx)],
                       seq // CHUNK)
    y_cm = _merge_odd([h.reshape(bsz * seq, vw) for h in h1], proj1.reshape(bsz * seq, -1), l1_norm_c,
                      l1_w_out.astype(bf16), tiles_b, ctx_tiles, bsz)
    y_lat = y_cm.reshape(bsz, GRID_W, rows_grid, d).transpose(0, 2, 1, 3).reshape(bsz * n_lat, d)
    lat_sel = lambda t: t.reshape(bsz, tiles_b, 1, d)[:, ctx_tiles:].reshape(bsz * lat_tiles, 1, d)
    x3 = _resid_ln(x2_lat.reshape(bsz * n_lat, d), y_lat, lat_sel(g1), l1_ln1_g, l1_ln1_b)
    x4 = _peer_residual(x3, lat_sel(sh2), lat_sel(sc2), lat_sel(g2), l1_peer_q, l1_peer_keys, l1_peer_u,
                        l1_peer_v, l1_ln2_g, l1_ln2_b)
    return x4.reshape(bsz, n_lat, d)
```

```python
import functools
import math
from typing import Callable, NamedTuple

import numpy as np
import jax
import jax.numpy as jnp
from jax import lax
from jax.experimental import pallas as pl
from jax.experimental.pallas import tpu as pltpu

f32 = jnp.float32
bf16 = jnp.bfloat16

D_MODEL = 1024
GRID_W = 64
CHUNK = 64
CONV_W = 5
POS_BASE = 10000.0
EPS = 1e-6
N_DIR = 2
DEPTH = 2
ALPHA = (2 * DEPTH) ** 0.25

A_HEADS, A_DK, A_DV = 4, 128, 128
B_HEADS, B_HEADDIM, B_GROUPS, B_STATE = 8, 64, 2, 128
C_HEADS, C_DK, C_DV = 4, 128, 256
PEER_HEADS, PEER_NKEYS, PEER_DKEY, PEER_TOPK = 8, 128, 256, 16

LANE = 128
ROW_TILE = 256
ROUTE_TILE = 512
EXPERT_TOK = 512
EXPERT_BLK = 2048
I1_TILE = 4
GATE_ROWS = 64
VMEM_LIMIT = 56 * 1024 * 1024
N_CAND_ROWS = -(-sum((PEER_TOPK + 1) // (p + 1) for p in range(PEER_TOPK + 1)) // 8) * 8

NN = (((1,), (0,)), ((), ()))
NT = (((1,), (1,)), ((), ()))


def _params(*sem):
    return pltpu.CompilerParams(dimension_semantics=sem, vmem_limit_bytes=VMEM_LIMIT)


def _dot(a, b, dims=NN):
    return lax.dot_general(a, b, dims, preferred_element_type=f32)


def _split(a):
    hi = a.astype(bf16)
    return hi, (a - hi.astype(f32)).astype(bf16)


def _dot3(a, b, dims=NN):
    ah, al = _split(a)
    bh, bl = _split(b)
    return _dot(ah, bh, dims) + (_dot(ah, bl, dims) + _dot(al, bh, dims))


def _sigmoid(x):
    return 1.0 / (1.0 + jnp.exp(-x))


def _silu(x):
    return x * _sigmoid(x)


def _softplus(x):
    return jnp.maximum(x, 0.0) + jnp.log1p(jnp.exp(-jnp.abs(x)))


def _layer_norm(z, g, b):
    mu = jnp.mean(z, -1, keepdims=True)
    zc = z - mu
    var = jnp.mean(zc * zc, -1, keepdims=True)
    return zc * lax.rsqrt(var + EPS) * g + b


def _rms(x, g):
    return x * lax.rsqrt(jnp.mean(x * x, -1, keepdims=True) + EPS) * g


def _chunk_masks(d):
    row = lax.broadcasted_iota(jnp.int32, (CHUNK, CHUNK), 0)
    col = lax.broadcasted_iota(jnp.int32, (CHUNK, CHUNK), 1)
    diff = row - col if d == 0 else col - row
    return diff >= 0, diff > 0, diff <= 0, row == col


def _cum(mask_incl, mask_incl_t, x_col, x_row):
    c_col = jnp.sum(jnp.where(mask_incl, x_row, 0.0), axis=1, keepdims=True)
    c_row = jnp.sum(jnp.where(mask_incl_t, x_col, 0.0), axis=0, keepdims=True)
    tot = jnp.sum(x_col, axis=0, keepdims=True)
    return c_col, c_row, tot


def _ada_kernel(c_ref, w_ref, b_ref, o_ref):
    o_ref[...] = _dot3(_silu(c_ref[...]), w_ref[...]) + b_ref[...]


def _ada(cond, w, b):
    rows, d = cond.shape
    n = w.shape[1]
    return pl.pallas_call(
        _ada_kernel,
        out_shape=jax.ShapeDtypeStruct((rows, n), f32),
        grid=(n // d,),
        in_specs=[pl.BlockSpec((rows, d), lambda j: (0, 0)),
                  pl.BlockSpec((d, d), lambda j: (0, j)),
                  pl.BlockSpec((1, d), lambda j: (0, j))],
        out_specs=pl.BlockSpec((rows, d), lambda j: (0, j)),
        compiler_params=_params("parallel"),
    )(cond, w, b.reshape(1, n))


def _proj_kernel(x_ref, sh_ref, sc_ref, w_ref, o_ref, g_ref):
    xm = x_ref[...] * (1.0 + sc_ref[0]) + sh_ref[0]
    res = _dot(xm.astype(bf16), w_ref[...])
    o_ref[...] = res
    g_ref[...] = res[:, res.shape[1] - LANE:]


def _proj(x, shift_t, scale_t, w):
    rows, d = x.shape
    n = w.shape[1]
    return pl.pallas_call(
        _proj_kernel,
        out_shape=(jax.ShapeDtypeStruct((rows, n), f32), jax.ShapeDtypeStruct((rows, LANE), f32)),
        grid=(rows // ROW_TILE,),
        in_specs=[pl.BlockSpec((ROW_TILE, d), lambda i: (i, 0)),
                  pl.BlockSpec((1, 1, d), lambda i: (i, 0, 0)),
                  pl.BlockSpec((1, 1, d), lambda i: (i, 0, 0)),
                  pl.BlockSpec((d, n), lambda i: (0, 0))],
        out_specs=(pl.BlockSpec((ROW_TILE, n), lambda i: (i, 0)), pl.BlockSpec((ROW_TILE, LANE), lambda i: (i, 0))),
        compiler_params=_params("parallel"),
    )(x, shift_t, scale_t, w)


def _conv_kernel(x_ref, w_ref, b_ref, o_ref, pad_ref, *, seq, seg0, n_norm, n_qscale, qscale):
    j = pl.program_id(1)
    halo = 8
    zeros = jnp.zeros((halo, LANE), f32)
    pad_ref[0:halo, :] = zeros
    pad_ref[halo:halo + seg0, :] = x_ref[0, 0:seg0, :]
    pad_ref[halo + seg0:2 * halo + seg0, :] = zeros
    pad_ref[2 * halo + seg0:2 * halo + seq, :] = x_ref[0, seg0:seq, :]
    pad_ref[2 * halo + seq:3 * halo + seq, :] = zeros
    w = w_ref[...]
    bias = b_ref[...]
    scale = jnp.where(j < n_qscale, qscale, 1.0).astype(f32)
    rows = ROW_TILE

    def body(c, carry, normalize):
        r0 = pl.multiple_of(c * rows, rows)
        start = pl.multiple_of(r0 + jnp.where(c >= seg0 // rows, halo, 0), halo)
        win = pad_ref[pl.ds(start, rows + 2 * halo), :]
        acc = bias + w[2:3] * win[halo:halo + rows]
        for k in (0, 1, 3, 4):
            off = k - CONV_W // 2
            acc = acc + w[k:k + 1] * win[halo + off:halo + off + rows]
        y = _silu(acc)
        if normalize:
            y = y * lax.rsqrt(jnp.sum(y * y, -1, keepdims=True) + EPS)
        o_ref[0, pl.ds(r0, rows), :] = y * scale
        return carry

    if n_norm > 0:
        @pl.when(j < n_norm)
        def _():
            lax.fori_loop(0, seq // rows, functools.partial(body, normalize=True), 0)

    @pl.when(j >= n_norm)
    def _():
        lax.fori_loop(0, seq // rows, functools.partial(body, normalize=False), 0)


def _conv_act(x, w, b, col_blocks, *, seg0, n_norm, n_qscale, qscale):
    bsz, seq, _ = x.shape
    n_out = len(col_blocks)
    first_gap = next((i for i, cb in enumerate(col_blocks) if cb != i), n_out)
    gap = col_blocks[first_gap] - first_gap if first_gap < n_out else 0
    in_map = lambda bi, j: (bi, 0, jnp.where(j < first_gap, j, j + gap))
    kern = functools.partial(_conv_kernel, seq=seq, seg0=seg0, n_norm=n_norm, n_qscale=n_qscale, qscale=qscale)
    return pl.pallas_call(
        kern,
        out_shape=jax.ShapeDtypeStruct((bsz, seq, n_out * LANE), f32),
        grid=(bsz, n_out),
        in_specs=[pl.BlockSpec((1, seq, LANE), in_map),
                  pl.BlockSpec((CONV_W, LANE), lambda bi, j: (0, j)),
                  pl.BlockSpec((1, LANE), lambda bi, j: (0, j))],
        out_specs=pl.BlockSpec((1, seq, LANE), lambda bi, j: (bi, 0, j)),
        scratch_shapes=[pltpu.VMEM((seq + 24, LANE), f32)],
        compiler_params=_params("parallel", "parallel"),
    )(x, w, b)


def _chunk_index(d, n, n_ctx, n_all):
    return n if d == 0 else jnp.where(n < n_ctx, n_ctx - 1 - n, n_all + n_ctx - 1 - n)


def _both_directions(make_specs):
    return [spec for d in range(N_DIR) for spec in make_specs(d)]


class _ScanPart(NamedTuple):
    kernel: Callable
    operands: list
    in_specs: list
    out_shapes: list
    out_specs: list
    scratch_shapes: list


def _run_scans(parts, n_steps):
    def fused(*refs):
        n_in = sum(len(p.operands) for p in parts)
        n_out = sum(len(p.out_shapes) for p in parts)
        ins, outs, scr = refs[:n_in], refs[n_in:n_in + n_out], refs[n_in + n_out:]

        @pl.when(pl.program_id(0) == 0)
        def _():
            for state in scr:
                state[...] = jnp.zeros(state.shape, f32)

        for p in parts:
            p.kernel(*ins[:len(p.operands)], *outs[:len(p.out_shapes)], *scr[:len(p.scratch_shapes)])
            ins, outs, scr = ins[len(p.operands):], outs[len(p.out_shapes):], scr[len(p.scratch_shapes):]

    flat = pl.pallas_call(
        fused,
        out_shape=tuple(s for p in parts for s in p.out_shapes),
        grid=(n_steps,),
        in_specs=[s for p in parts for s in p.in_specs],
        out_specs=tuple(s for p in parts for s in p.out_specs),
        scratch_shapes=[s for p in parts for s in p.scratch_shapes],
        compiler_params=_params("arbitrary"),
    )(*[o for p in parts for o in p.operands])
    results = []
    for p in parts:
        results.append(flat[:len(p.out_shapes)])
        flat = flat[len(p.out_shapes):]
    return results


def _gate_layouts(raw, heads):
    bsz, seq, _ = raw.shape
    r = raw.reshape(bsz, seq // CHUNK, CHUNK, N_DIR, heads)
    return r.transpose(3, 0, 1, 2, 4), r.transpose(3, 0, 1, 4, 2)


def _param_layouts(*ps):
    return jnp.stack(ps, axis=1).astype(f32), jnp.stack(ps, axis=2).astype(f32)


def _gdn_kernel(*refs, bsz):
    n_in = 7
    ins = [refs[d * n_in:(d + 1) * n_in] for d in range(N_DIR)]
    pcol_ref, prow_ref = refs[N_DIR * n_in:N_DIR * n_in + 2]
    o_refs = refs[N_DIR * n_in + 2:N_DIR * n_in + 2 + N_DIR]
    st_ref = refs[-1]
    masks = [_chunk_masks(d) for d in range(N_DIR)]
    eye = masks[0][3].astype(f32)
    row = lax.broadcasted_iota(jnp.int32, (CHUNK, CHUNK), 0)
    col = lax.broadcasted_iota(jnp.int32, (CHUNK, CHUNK), 1)
    blocks = [(row >> sh) == (col >> sh) for sh in (3, 4, 5, 6)]
    off_masks = [blocks[lvl + 1] & ~blocks[lvl] for lvl in range(3)]
    chains = [(d, b, h) for d in range(N_DIR) for b in range(bsz) for h in range(A_HEADS)]
    idx = range(len(chains))
    pc = [pcol_ref[d] for d in range(N_DIR)]
    pr = [prow_ref[d] for d in range(N_DIR)]
    beta_col = [[_sigmoid(ins[d][3][0, b, 0]) for b in range(bsz)] for d in range(N_DIR)]
    g_col = [[-jnp.exp(pc[d][0:1, :]) * _softplus(ins[d][5][0, b, 0] + pc[d][1:2, :]) for b in range(bsz)]
             for d in range(N_DIR)]
    g_row = [[-jnp.exp(pr[d][:, 0:1]) * _softplus(ins[d][6][0, b, 0] + pr[d][:, 1:2]) for b in range(bsz)]
             for d in range(N_DIR)]
    sl = [slice(h * A_DK, (h + 1) * A_DK) for _, _, h in chains]
    q = [ins[d][0][b, :, sl[c]] for c, (d, b, _) in enumerate(chains)]
    k = [ins[d][1][b, :, sl[c]] for c, (d, b, _) in enumerate(chains)]
    v = [ins[d][2][b, :, sl[c]] for c, (d, b, _) in enumerate(chains)]
    cums = [_cum(masks[d][0], masks[d][2], g_col[d][b][:, h:h + 1], g_row[d][b][h:h + 1, :]) for d, b, h in chains]
    decay = [jnp.exp(jnp.where(masks[d][0], cums[c][0] - cums[c][1], -jnp.inf)) for c, (d, _, _) in enumerate(chains)]
    b_col = [beta_col[d][b][:, h:h + 1] for d, b, h in chains]
    kb = [k[c] * b_col[c] for c in idx]
    l_mat = [_dot(kb[c], k[c], NT) * jnp.where(masks[d][1], decay[c], 0.0) for c, (d, _, _) in enumerate(chains)]
    nil = [-jnp.where(blocks[0], l_mat[c], 0.0) for c in idx]
    t_inv = [eye + nil[c] for c in idx]
    for _ in range(2):
        nil = [_dot(nil[c], nil[c]) for c in idx]
        t_inv = [t_inv[c] + _dot(t_inv[c], nil[c]) for c in idx]
    for lvl in range(3):
        left = [_dot(t_inv[c], jnp.where(off_masks[lvl], l_mat[c], 0.0)) for c in idx]
        t_inv = [t_inv[c] - _dot(left[c], t_inv[c]) for c in idx]
    e_col = [jnp.exp(gc) for gc, _, _ in cums]
    uw = [_dot(t_inv[c], jnp.concatenate([v[c] * b_col[c], kb[c] * e_col[c]], axis=1)) for c in idx]
    qk = [_dot(q[c], k[c], NT) * decay[c] for c in idx]
    k_dec_t = [(k[c] * jnp.exp(cums[c][2] - cums[c][0])).T for c in idx]
    s = [st_ref[c] for c in idx]
    wq = [_dot(jnp.concatenate([uw[c][:, A_DV:], q[c] * e_col[c]], axis=0), s[c]) for c in idx]
    v_new = [uw[c][:, :A_DV] - wq[c][:CHUNK] for c in idx]
    out = [wq[c][CHUNK:] + _dot(qk[c], v_new[c]) for c in idx]
    s_new = [jnp.exp(cums[c][2]) * s[c] + _dot(k_dec_t[c], v_new[c]) for c in idx]
    for c, (d, b, _) in enumerate(chains):
        o_refs[d][b, :, sl[c]] = out[c]
        st_ref[c] = s_new[c]


def _gdn(act, beta_raw, decay_raw, a_log, dt_bias, n_ctx):
    bsz, seq, _ = act.shape
    n_all = seq // CHUNK
    hw = A_HEADS * A_DK
    bcol, brow = _gate_layouts(beta_raw, A_HEADS)
    dcol, drow = _gate_layouts(decay_raw, A_HEADS)
    pcol, prow = _param_layouts(a_log, dt_bias)
    ci = functools.partial(_chunk_index, n_ctx=n_ctx, n_all=n_all)

    def specs(d):
        tok = lambda blk: pl.BlockSpec((bsz, CHUNK, hw), lambda n: (0, ci(d, n), blk))
        gcol = pl.BlockSpec((1, bsz, 1, CHUNK, A_HEADS), lambda n: (d, 0, ci(d, n), 0, 0))
        grow = pl.BlockSpec((1, bsz, 1, A_HEADS, CHUNK), lambda n: (d, 0, ci(d, n), 0, 0))
        return [tok(0), tok(1), tok(2), gcol, grow, gcol, grow]

    out = jax.ShapeDtypeStruct((bsz, seq, hw), f32)
    return _ScanPart(
        kernel=functools.partial(_gdn_kernel, bsz=bsz),
        operands=[act, act, act, bcol, brow, dcol, drow] * N_DIR + [pcol, prow],
        in_specs=_both_directions(specs) + [pl.BlockSpec((N_DIR, 2, A_HEADS), lambda n: (0, 0, 0)),
                                            pl.BlockSpec((N_DIR, A_HEADS, 2), lambda n: (0, 0, 0))],
        out_shapes=[out] * N_DIR,
        out_specs=[pl.BlockSpec((bsz, CHUNK, hw), lambda n, d=d: (0, ci(d, n), 0)) for d in range(N_DIR)],
        scratch_shapes=[pltpu.VMEM((N_DIR * bsz * A_HEADS, A_DK, A_DV), f32)])


def _ssd_kernel(*refs, bsz):
    n_in = 5
    ins = [refs[d * n_in:(d + 1) * n_in] for d in range(N_DIR)]
    pcol_ref, prow_ref = refs[N_DIR * n_in:N_DIR * n_in + 2]
    o_refs = refs[N_DIR * n_in + 2:N_DIR * n_in + 2 + N_DIR]
    st_ref = refs[-1]
    masks = [_chunk_masks(d) for d in range(N_DIR)]
    rep = B_HEADS // B_GROUPS
    streams = [(d, b) for d in range(N_DIR) for b in range(bsz)]
    groups = [(s, g) for s in range(len(streams)) for g in range(B_GROUPS)]
    heads = [(s, h) for s in range(len(streams)) for h in range(B_HEADS)]
    gsl = [slice(g * B_STATE, (g + 1) * B_STATE) for _, g in groups]
    hsl = [slice(h * B_HEADDIM, (h + 1) * B_HEADDIM) for _, h in heads]
    pc = [pcol_ref[d] for d, _ in streams]
    pr = [prow_ref[d] for d, _ in streams]
    dt_col = [_softplus(ins[d][3][0, b, 0] + pc[s][1:2, :]) for s, (d, b) in enumerate(streams)]
    dt_row = [_softplus(ins[d][4][0, b, 0] + pr[s][:, 1:2]) for s, (d, b) in enumerate(streams)]
    da_col = [dt_col[s] * (-jnp.exp(pc[s][0:1, :])) for s in range(len(streams))]
    da_row = [dt_row[s] * (-jnp.exp(pr[s][:, 0:1])) for s in range(len(streams))]
    bm = [ins[streams[s][0]][1][streams[s][1], :, gsl[i]] for i, (s, _) in enumerate(groups)]
    cm = [ins[streams[s][0]][2][streams[s][1], :, gsl[i]] for i, (s, _) in enumerate(groups)]
    hs = [st_ref[i] for i in range(len(groups))]
    cb = [_dot(cm[i], bm[i], NT) for i in range(len(groups))]
    y_off = [_dot(cm[i], hs[i]) for i in range(len(groups))]
    bm_t = [bm[i].T for i in range(len(groups))]
    m_incl = [masks[streams[s][0]][0] for s, _ in heads]
    m_incl_t = [masks[streams[s][0]][2] for s, _ in heads]
    cums = [_cum(m_incl[j], m_incl_t[j], da_col[s][:, h:h + 1], da_row[s][h:h + 1, :]) for j, (s, h) in enumerate(heads)]
    seg = [jnp.exp(jnp.where(m_incl[j], cums[j][0] - cums[j][1], -jnp.inf)) for j in range(len(heads))]
    xdt = [ins[streams[s][0]][0][streams[s][1], :, hsl[j]] * dt_col[s][:, h:h + 1] for j, (s, h) in enumerate(heads)]
    y_diag = [_dot(cb[j // rep] * seg[j], xdt[j]) for j in range(len(heads))]
    for j, (s, h) in enumerate(heads):
        r = h % rep
        d, b = streams[s]
        o_refs[d][b, :, hsl[j]] = y_diag[j] + jnp.exp(cums[j][0]) * y_off[j // rep][:, r * B_HEADDIM:(r + 1) * B_HEADDIM]
    xdt_end = [xdt[j] * jnp.exp(cums[j][2] - cums[j][0]) for j in range(len(heads))]
    keep = [jnp.broadcast_to(jnp.exp(cums[j][2]), (1, B_HEADDIM)) for j in range(len(heads))]
    upd = [_dot(bm_t[i], jnp.concatenate(xdt_end[i * rep:(i + 1) * rep], axis=1)) for i in range(len(groups))]
    for i in range(len(groups)):
        st_ref[i] = jnp.concatenate(keep[i * rep:(i + 1) * rep], axis=1) * hs[i] + upd[i]


def _ssd(act, dt_raw, a_log, dt_bias, n_ctx):
    bsz, seq, _ = act.shape
    n_all = seq // CHUNK
    dinner = B_HEADS * B_HEADDIM
    gw = B_GROUPS * B_STATE
    tcol, trow = _gate_layouts(dt_raw, B_HEADS)
    pcol, prow = _param_layouts(a_log, dt_bias)
    ci = functools.partial(_chunk_index, n_ctx=n_ctx, n_all=n_all)
    x_off = (A_HEADS * (2 * A_DK + A_DV)) // dinner
    bm_off = (A_HEADS * (2 * A_DK + A_DV) + dinner) // gw

    def specs(d):
        return [pl.BlockSpec((bsz, CHUNK, dinner), lambda n: (0, ci(d, n), x_off)),
                pl.BlockSpec((bsz, CHUNK, gw), lambda n: (0, ci(d, n), bm_off)),
                pl.BlockSpec((bsz, CHUNK, gw), lambda n: (0, ci(d, n), bm_off + 1)),
                pl.BlockSpec((1, bsz, 1, CHUNK, B_HEADS), lambda n: (d, 0, ci(d, n), 0, 0)),
                pl.BlockSpec((1, bsz, 1, B_HEADS, CHUNK), lambda n: (d, 0, ci(d, n), 0, 0))]

    out = jax.ShapeDtypeStruct((bsz, seq, dinner), f32)
    return _ScanPart(
        kernel=functools.partial(_ssd_kernel, bsz=bsz),
        operands=[act, act, act, tcol, trow] * N_DIR + [pcol, prow],
        in_specs=_both_directions(specs) + [pl.BlockSpec((N_DIR, 2, B_HEADS), lambda n: (0, 0, 0)),
                                            pl.BlockSpec((N_DIR, B_HEADS, 2), lambda n: (0, 0, 0))],
        out_shapes=[out] * N_DIR,
        out_specs=[pl.BlockSpec((bsz, CHUNK, dinner), lambda n, d=d: (0, ci(d, n), 0)) for d in range(N_DIR)],
        scratch_shapes=[pltpu.VMEM((N_DIR * bsz * B_GROUPS, B_STATE, (B_HEADS // B_GROUPS) * B_HEADDIM), f32)])


def _mlstm_kernel(*refs, bsz):
    n_in = 7
    ins = [refs[d * n_in:(d + 1) * n_in] for d in range(N_DIR)]
    pcol_ref, prow_ref = refs[N_DIR * n_in:N_DIR * n_in + 2]
    o_refs = refs[N_DIR * n_in + 2:N_DIR * n_in + 2 + N_DIR]
    c_ref, n_ref, m_ref = refs[-3:]
    masks = [_chunk_masks(d) for d in range(N_DIR)]
    streams = [(d, b) for d in range(N_DIR) for b in range(bsz)]
    chains = [(s, h) for s in range(len(streams)) for h in range(C_HEADS)]
    idx = range(len(chains))
    pc = [pcol_ref[d] for d, _ in streams]
    pr = [prow_ref[d] for d, _ in streams]
    li_col = [ins[d][3][0, b, 0] + pc[s][0:1, :] for s, (d, b) in enumerate(streams)]
    li_row = [ins[d][4][0, b, 0] + pr[s][:, 0:1] for s, (d, b) in enumerate(streams)]
    lf_col = [-_softplus(-(ins[d][5][0, b, 0] + pc[s][1:2, :])) for s, (d, b) in enumerate(streams)]
    lf_row = [-_softplus(-(ins[d][6][0, b, 0] + pr[s][:, 1:2])) for s, (d, b) in enumerate(streams)]
    q = [ins[streams[s][0]][0][streams[s][1], :, h * C_DK:(h + 1) * C_DK] for s, h in chains]
    k = [ins[streams[s][0]][1][streams[s][1], :, h * C_DK:(h + 1) * C_DK] for s, h in chains]
    v = [ins[streams[s][0]][2][streams[s][1], :, h * C_DV:(h + 1) * C_DV] for s, h in chains]
    m_incl = [masks[streams[s][0]][0] for s, _ in chains]
    m_incl_t = [masks[streams[s][0]][2] for s, _ in chains]
    cums = [_cum(m_incl[c], m_incl_t[c], lf_col[s][:, h:h + 1], lf_row[s][h:h + 1, :]) for c, (s, h) in enumerate(chains)]
    logw_intra = [jnp.where(m_incl[c], cums[c][0] - cums[c][1] + li_row[s][h:h + 1, :], -jnp.inf)
                  for c, (s, h) in enumerate(chains)]
    m_prev = [m_ref[c] for c in idx]
    logw_inter = [cums[c][0] + m_prev[c] for c in idx]
    m_t = [jnp.maximum(logw_inter[c], jnp.max(logw_intra[c], axis=1, keepdims=True)) for c in idx]
    qk = [_dot(q[c], k[c], NT) for c in idx]
    c_mem = [c_ref[c] for c in idx]
    n_mem = [n_ref[c] for c in idx]
    qc = [_dot(q[c], c_mem[c]) for c in idx]
    s = [qk[c] * jnp.exp(logw_intra[c] - m_t[c]) for c in idx]
    w_inter = [jnp.exp(logw_inter[c] - m_t[c]) for c in idx]
    num = [_dot(s[c], v[c]) + w_inter[c] * qc[c] for c in idx]
    den = [jnp.sum(s[c], axis=1, keepdims=True) + w_inter[c] * jnp.sum(q[c] * n_mem[c], axis=1, keepdims=True)
           for c in idx]
    for c, (s, h) in enumerate(chains):
        d, b = streams[s]
        o_refs[d][b, :, h * C_DV:(h + 1) * C_DV] = num[c] / jnp.maximum(jnp.abs(den[c]), jnp.exp(-m_t[c]))
    logw_end = [cums[c][2] - cums[c][0] + li_col[s][:, h:h + 1] for c, (s, h) in enumerate(chains)]
    m_new = [jnp.maximum(cums[c][2] + m_prev[c], jnp.max(logw_end[c], axis=0, keepdims=True)) for c in idx]
    kw = [k[c] * jnp.exp(logw_end[c] - m_new[c]) for c in idx]
    keep = [jnp.exp(cums[c][2] + m_prev[c] - m_new[c]) for c in idx]
    kv = [_dot(kw[c].T, v[c]) for c in idx]
    for c in idx:
        c_ref[c] = keep[c] * c_mem[c] + kv[c]
        n_ref[c] = keep[c] * n_mem[c] + jnp.sum(kw[c], axis=0, keepdims=True)
        m_ref[c] = m_new[c]


def _mlstm(act, proj, i_raw, f_raw, i_bias, f_bias, n_ctx):
    bsz, seq, _ = act.shape
    n_all = seq // CHUNK
    qw = C_HEADS * C_DK
    vw = C_HEADS * C_DV
    icol, irow = _gate_layouts(i_raw, C_HEADS)
    fcol, frow = _gate_layouts(f_raw, C_HEADS)
    pcol, prow = _param_layouts(i_bias, f_bias)
    ci = functools.partial(_chunk_index, n_ctx=n_ctx, n_all=n_all)

    def specs(d):
        gcol = pl.BlockSpec((1, bsz, 1, CHUNK, C_HEADS), lambda n: (d, 0, ci(d, n), 0, 0))
        grow = pl.BlockSpec((1, bsz, 1, C_HEADS, CHUNK), lambda n: (d, 0, ci(d, n), 0, 0))
        return [pl.BlockSpec((bsz, CHUNK, qw), lambda n: (0, ci(d, n), 0)),
                pl.BlockSpec((bsz, CHUNK, qw), lambda n: (0, ci(d, n), 1)),
                pl.BlockSpec((bsz, CHUNK, vw), lambda n: (0, ci(d, n), 2 * qw // vw)),
                gcol, grow, gcol, grow]

    out = jax.ShapeDtypeStruct((bsz, seq, vw), f32)
    n_chain = N_DIR * bsz * C_HEADS
    return _ScanPart(
        kernel=functools.partial(_mlstm_kernel, bsz=bsz),
        operands=[act, act, proj, icol, irow, fcol, frow] * N_DIR + [pcol, prow],
        in_specs=_both_directions(specs) + [pl.BlockSpec((N_DIR, 2, C_HEADS), lambda n: (0, 0, 0)),
                                            pl.BlockSpec((N_DIR, C_HEADS, 2), lambda n: (0, 0, 0))],
        out_shapes=[out] * N_DIR,
        out_specs=[pl.BlockSpec((bsz, CHUNK, vw), lambda n, d=d: (0, ci(d, n), 0)) for d in range(N_DIR)],
        scratch_shapes=[pltpu.VMEM((n_chain, C_DK, C_DV), f32),
                        pltpu.VMEM((n_chain, 1, C_DK), f32),
                        pltpu.VMEM((n_chain, 1, 1), f32)])


def _merge_even_kernel(x_ref, af_ref, ar_ref, bf_ref, br_ref, za_ref, zb_ref, xs_ref, na_ref, nb_ref, dsk_ref,
                       w_ref, g_ref, lg_ref, lb_ref, o_ref):
    a = af_ref[...] + ar_ref[...]
    za = za_ref[...]
    na = na_ref[...]
    parts = []
    for h in range(A_HEADS):
        sl = slice(h * A_DV, (h + 1) * A_DV)
        parts.append(_rms(a[:, sl], na) * _silu(za[:, sl]))
    yb = (bf_ref[...] + br_ref[...] + dsk_ref[...] * xs_ref[...]) * _silu(zb_ref[...])
    gw = (B_HEADS * B_HEADDIM) // B_GROUPS
    nb = nb_ref[...]
    for g in range(B_GROUPS):
        sl = slice(g * gw, (g + 1) * gw)
        parts.append(_rms(yb[:, sl], nb[:, sl]))
    y = jnp.concatenate(parts, axis=1).astype(bf16)
    z = ALPHA * x_ref[...] + g_ref[0] * _dot(y, w_ref[...])
    o_ref[...] = _layer_norm(z, lg_ref[...], lb_ref[...])


def _merge_even(x, o_a, o_b, proj, act, norm_a, norm_b, d_skip, w_out, gate_t, ln_g, ln_b):
    rows, d = x.shape
    aw = A_HEADS * A_DV
    bw = B_HEADS * B_HEADDIM
    row = lambda w_, blk: pl.BlockSpec((ROW_TILE, w_), lambda i: (i, blk))
    vec = lambda w_: pl.BlockSpec((1, w_), lambda i: (0, 0))
    qkv = A_HEADS * (2 * A_DK + A_DV)
    return pl.pallas_call(
        _merge_even_kernel,
        out_shape=jax.ShapeDtypeStruct((rows, d), f32),
        grid=(rows // ROW_TILE,),
        in_specs=[row(d, 0), row(aw, 0), row(aw, 0), row(bw, 0), row(bw, 0),
                  row(aw, qkv // aw), row(bw, (qkv + aw) // bw), row(bw, qkv // bw),
                  vec(A_DV), vec(bw), vec(bw),
                  pl.BlockSpec((aw + bw, d), lambda i: (0, 0)),
                  pl.BlockSpec((1, 1, d), lambda i: (i, 0, 0)),
                  vec(d), vec(d)],
        out_specs=row(d, 0),
        compiler_params=_params("parallel"),
    )(x, *o_a, *o_b, proj, proj, act, norm_a.reshape(1, -1), norm_b.reshape(1, -1),
      jnp.repeat(d_skip, B_HEADDIM).reshape(1, -1), w_out, gate_t, ln_g.reshape(1, -1), ln_b.reshape(1, -1))


def _merge_odd_kernel(hf_ref, hr_ref, o_ref_in, nc_ref, w_ref, out_ref):
    hsum = hf_ref[...] + hr_ref[...]
    o = o_ref_in[...]
    nc = nc_ref[...]
    parts = []
    for h in range(C_HEADS):
        sl = slice(h * C_DV, (h + 1) * C_DV)
        parts.append(_rms(hsum[:, sl], nc) * _sigmoid(o[:, sl]))
    y = jnp.concatenate(parts, axis=1).astype(bf16)
    out_ref[...] = _dot(y, w_ref[...])


def _merge_odd(h, proj, norm_c, w_out, tiles_per_batch, ctx_tiles, bsz):
    vw = C_HEADS * C_DV
    lat_tiles = tiles_per_batch - ctx_tiles
    src = lambda i: i + ctx_tiles * (i // lat_tiles + 1)
    d = w_out.shape[1]
    return pl.pallas_call(
        _merge_odd_kernel,
        out_shape=jax.ShapeDtypeStruct((bsz * lat_tiles * ROW_TILE, d), f32),
        grid=(bsz * lat_tiles,),
        in_specs=[pl.BlockSpec((ROW_TILE, vw), lambda i: (src(i), 0)),
                  pl.BlockSpec((ROW_TILE, vw), lambda i: (src(i), 0)),
                  pl.BlockSpec((ROW_TILE, vw), lambda i: (src(i), 2 * C_HEADS * C_DK // vw + 1)),
                  pl.BlockSpec((1, C_DV), lambda i: (0, 0)),
                  pl.BlockSpec((vw, d), lambda i: (0, 0))],
        out_specs=pl.BlockSpec((ROW_TILE, d), lambda i: (i, 0)),
        compiler_params=_params("parallel"),
    )(*h, proj, norm_c.reshape(1, -1), w_out)


def _resid_ln_kernel(x_ref, y_ref, g_ref, lg_ref, lb_ref, o_ref):
    o_ref[...] = _layer_norm(ALPHA * x_ref[...] + g_ref[0] * y_ref[...], lg_ref[...], lb_ref[...])


def _resid_ln(x, y, gate_t, ln_g, ln_b):
    rows, d = x.shape
    return pl.pallas_call(
        _resid_ln_kernel,
        out_shape=jax.ShapeDtypeStruct((rows, d), f32),
        grid=(rows // ROW_TILE,),
        in_specs=[pl.BlockSpec((ROW_TILE, d), lambda i: (i, 0)), pl.BlockSpec((ROW_TILE, d), lambda i: (i, 0)),
                  pl.BlockSpec((1, 1, d), lambda i: (i, 0, 0)),
                  pl.BlockSpec((1, d), lambda i: (0, 0)),
                  pl.BlockSpec((1, d), lambda i: (0, 0))],
        out_specs=pl.BlockSpec((ROW_TILE, d), lambda i: (i, 0)),
        compiler_params=_params("parallel"),
    )(x, y, gate_t, ln_g.reshape(1, -1), ln_b.reshape(1, -1))


def _oddeven_merge(lo, hi, r):
    step = r * 2
    if step < hi - lo:
        yield from _oddeven_merge(lo, hi, step)
        yield from _oddeven_merge(lo + r, hi, step)
        yield from [(i, i + r) for i in range(lo + r, hi - r, step)]
    else:
        yield (lo, lo + r)


def _oddeven_sort(lo, hi):
    if hi - lo >= 1:
        mid = lo + (hi - lo) // 2
        yield from _oddeven_sort(lo, mid)
        yield from _oddeven_sort(mid + 1, hi)
        yield from _oddeven_merge(lo, hi, 1)


def _exchange(a, i, j):
    a[i], a[j] = jnp.maximum(a[i], a[j]), jnp.minimum(a[i], a[j])


def _top_sorted(slabs):
    a = list(slabs)
    for i, j in _oddeven_sort(0, len(a) - 1):
        _exchange(a, i, j)
    for shift in (4, 2, 1):
        other = [pltpu.roll(x, shift, axis=0) for x in a]
        if len(a) < PEER_TOPK:
            a = a + other[::-1]
        else:
            a = [jnp.maximum(a[k], other[PEER_TOPK - 1 - k]) for k in range(PEER_TOPK)]
        dist = PEER_TOPK // 2
        while dist >= 1:
            for i in range(PEER_TOPK):
                if i & dist == 0:
                    _exchange(a, i, i + dist)
            dist //= 2
    return a


def _next_largest(slabs, kth):
    count = sum(jnp.where(s >= kth, 1.0, 0.0) for s in slabs)
    below = functools.reduce(jnp.maximum, [jnp.where(s < kth, s, -jnp.inf) for s in slabs])
    count = jnp.sum(count, axis=0, keepdims=True)
    below = jnp.max(below, axis=0, keepdims=True)
    return jnp.where(count > PEER_TOPK, kth, below)


def _top_values(s):
    slabs = [s[i:i + 8] for i in range(0, s.shape[0], 8)]
    if len(slabs) < 8:
        slabs = slabs + [jnp.full_like(slabs[0], -jnp.inf)] * (8 - len(slabs))
    top = [t[0:1] for t in _top_sorted(slabs)]
    return top + [_next_largest(slabs, top[-1])]


def _route_kernel(x_ref, sh_ref, sc_ref, w_ref, k_ref, xm_ref, thr_ref, s2_ref, e1_ref, e2_ref, sc_scr, cand_ref):
    for t in range(ROUTE_TILE // ROW_TILE):
        rows = slice(t * ROW_TILE, (t + 1) * ROW_TILE)
        xm_ref[rows, :] = (x_ref[rows, :] * (1.0 + sc_ref[t]) + sh_ref[t]).astype(bf16)
    half = PEER_DKEY // 2
    n_top = PEER_TOPK + 1
    pairs = [(p, r) for p in range(n_top) for r in range(n_top) if (p + 1) * (r + 1) <= n_top]
    cand_ref[...] = jnp.full(cand_ref.shape, -jnp.inf, f32)

    def scores(h, sc_scr):
        wsl = pl.ds(pl.multiple_of(h * PEER_DKEY, PEER_DKEY), PEER_DKEY)
        q = _dot(xm_ref[...], w_ref[:, wsl])
        for c in range(2):
            sc_scr[c] = _dot(k_ref[h, c], q[:, c * half:(c + 1) * half].astype(bf16), NT)

    def select(h, sc_scr):
        for lb in range(ROUTE_TILE // LANE):
            lanes = slice(lb * LANE, (lb + 1) * LANE)
            s1 = sc_scr[0, :, lanes]
            s2 = sc_scr[1, :, lanes]
            t1 = _top_values(s1)
            t2 = _top_values(s2)
            for i, (p, r) in enumerate(pairs):
                cand_ref[i:i + 1, lanes] = t1[p] + t2[r]
            best = _top_values(cand_ref[:, lanes])
            z = sum(jnp.exp(bv - best[0]) for bv in best[:PEER_TOPK])
            tau = 0.5 * (best[PEER_TOPK - 1] + best[PEER_TOPK])
            thr_ref[h, :, lanes] = tau - s1
            s2_ref[h, :, lanes] = s2
            e1_ref[h, :, lanes] = jnp.exp(s1 - t1[0]) / z
            e2_ref[h, :, lanes] = jnp.exp(s2 - t2[0])

    def head(h, carry):
        scores(h, sc_scr)
        select(h, sc_scr)
        return carry

    lax.fori_loop(0, PEER_HEADS, head, 0)


def _route(x, shift_t, scale_t, wq, keys):
    rows, d = x.shape
    nq = wq.shape[1]
    half = PEER_DKEY // 2
    score = jax.ShapeDtypeStruct((PEER_HEADS, PEER_NKEYS, rows), f32)
    score_spec = pl.BlockSpec((PEER_HEADS, PEER_NKEYS, ROUTE_TILE), lambda i: (0, 0, i))
    key_spec = pl.BlockSpec((PEER_HEADS, 2, PEER_NKEYS, half), lambda i: (0, 0, 0, 0))
    mods = ROUTE_TILE // ROW_TILE
    return pl.pallas_call(
        _route_kernel,
        out_shape=(jax.ShapeDtypeStruct((rows, d), bf16), score, score, score, score),
        grid=(rows // ROUTE_TILE,),
        in_specs=[pl.BlockSpec((ROUTE_TILE, d), lambda i: (i, 0)),
                  pl.BlockSpec((mods, 1, d), lambda i: (i, 0, 0)),
                  pl.BlockSpec((mods, 1, d), lambda i: (i, 0, 0)),
                  pl.BlockSpec((d, nq), lambda i: (0, 0)),
                  key_spec],
        out_specs=(pl.BlockSpec((ROUTE_TILE, d), lambda i: (i, 0)),
                   score_spec, score_spec, score_spec, score_spec),
        scratch_shapes=[pltpu.VMEM((2, PEER_NKEYS, ROUTE_TILE), f32),
                        pltpu.VMEM((N_CAND_ROWS, ROUTE_TILE), f32)],
        compiler_params=_params("parallel"),
    )(x, shift_t, scale_t, wq, keys)


def _expert_kernel(xm_ref, u_ref, vt_ref, thr_ref, e1_ref, s2_ref, e2_ref, x_ref, g_ref, lg_ref, lb_ref,
                   o_ref, acc_ref, a_scr, w_scr):
    j = pl.program_id(1)

    @pl.when(j == 0)
    def _():
        acc_ref[...] = jnp.zeros(acc_ref.shape, f32)

    a_scr[...] = _dot(u_ref[...], xm_ref[...], NT)
    sqrt_half = math.sqrt(0.5)
    row_chunks = PEER_NKEYS // GATE_ROWS
    tiles_per_group = (EXPERT_TOK // LANE) * row_chunks

    def tile(r, carry, i1_0):
        lanes = pl.ds(pl.multiple_of((r // row_chunks) * LANE, LANE), LANE)
        r0 = pl.multiple_of((r % row_chunks) * GATE_ROWS, GATE_ROWS)
        i2 = pl.ds(r0, GATE_ROWS)
        gate = [jnp.zeros((GATE_ROWS, LANE), f32) for _ in range(I1_TILE)]
        for h in range(PEER_HEADS):
            s2 = s2_ref[h, i2, lanes]
            e2 = e2_ref[h, i2, lanes]
            for il in range(I1_TILE):
                i1 = i1_0 + il
                w = e1_ref[h, i1:i1 + 1, lanes] * e2
                gate[il] = gate[il] + jnp.where(s2 >= thr_ref[h, i1:i1 + 1, lanes], w, 0.0)
        for il in range(I1_TILE):
            rows = pl.ds((i1_0 + il) * PEER_NKEYS + r0, GATE_ROWS)
            a = a_scr[rows, lanes]
            act = 0.5 * a * (1.0 + lax.erf(a * sqrt_half))
            w_scr[rows, lanes] = (gate[il] * act).astype(bf16)
        return carry

    for i1_0 in range(0, EXPERT_BLK // PEER_NKEYS, I1_TILE):
        lax.fori_loop(0, tiles_per_group, functools.partial(tile, i1_0=i1_0), 0)
    acc_ref[...] += _dot(vt_ref[...], w_scr[...])

    @pl.when(j == pl.num_programs(1) - 1)
    def _():
        for t in range(EXPERT_TOK // ROW_TILE):
            rows = slice(t * ROW_TILE, (t + 1) * ROW_TILE)
            z = ALPHA * x_ref[rows, :] + g_ref[t] * acc_ref[:, rows].T
            o_ref[rows, :] = _layer_norm(z, lg_ref[...], lb_ref[...])


def _experts_residual(x, xm, u_bf, vt_bf, thr, e1, s2, e2, gate_t, ln_g, ln_b):
    rows, d = xm.shape
    n_exp = u_bf.shape[0]
    i1_per_blk = EXPERT_BLK // PEER_NKEYS
    sel = pl.BlockSpec((PEER_HEADS, i1_per_blk, EXPERT_TOK), lambda i, j: (0, j, i))
    full = pl.BlockSpec((PEER_HEADS, PEER_NKEYS, EXPERT_TOK), lambda i, j: (0, 0, i))
    tok = pl.BlockSpec((EXPERT_TOK, d), lambda i, j: (i, 0))
    vec = pl.BlockSpec((1, d), lambda i, j: (0, 0))
    return pl.pallas_call(
        _expert_kernel,
        out_shape=jax.ShapeDtypeStruct((rows, d), f32),
        grid=(rows // EXPERT_TOK, n_exp // EXPERT_BLK),
        in_specs=[tok,
                  pl.BlockSpec((EXPERT_BLK, d), lambda i, j: (j, 0)),
                  pl.BlockSpec((d, EXPERT_BLK), lambda i, j: (0, j)),
                  sel, sel, full, full,
                  tok, pl.BlockSpec((EXPERT_TOK // ROW_TILE, 1, d), lambda i, j: (i, 0, 0)), vec, vec],
        out_specs=tok,
        scratch_shapes=[pltpu.VMEM((d, EXPERT_TOK), f32),
                        pltpu.VMEM((EXPERT_BLK, EXPERT_TOK), f32),
                        pltpu.VMEM((EXPERT_BLK, EXPERT_TOK), bf16)],
        compiler_params=_params("parallel", "arbitrary"),
    )(xm, u_bf, vt_bf, thr, e1, s2, e2, x, gate_t, ln_g.reshape(1, -1), ln_b.reshape(1, -1))


def _peer_residual(x, shift_t, scale_t, gate_t, peer_q, peer_keys, peer_u, peer_v, ln_g, ln_b):
    xm, thr, s2, e1, e2 = _route(x, shift_t, scale_t, peer_q.astype(bf16), peer_keys.astype(bf16))
    return _experts_residual(x, xm, peer_u.astype(bf16), peer_v.T.astype(bf16), thr, e1, s2, e2, gate_t, ln_g, ln_b)


def _grid_sincos(n_tokens, d):
    n_rows = n_tokens // GRID_W
    n_freq = d // 4
    freq = jnp.exp(-math.log(POS_BASE) * jnp.arange(n_freq, dtype=f32) / n_freq)[None, :]
    row = jnp.arange(n_rows, dtype=f32)[:, None] * freq
    col = jnp.arange(GRID_W, dtype=f32)[:, None] * freq
    row_part = jnp.concatenate([jnp.sin(row), jnp.cos(row)], -1)[:, None, :]
    col_part = jnp.concatenate([jnp.sin(col), jnp.cos(col)], -1)[None, :, :]
    shape = (n_rows, GRID_W, 2 * n_freq)
    table = jnp.concatenate([jnp.broadcast_to(row_part, shape), jnp.broadcast_to(col_part, shape)], -1)
    return table.reshape(n_tokens, d)


def _tile_rows(mod, src):
    d = mod.shape[1] // 6
    rows = jnp.concatenate([jnp.broadcast_to(mod[r:r + 1], (n, mod.shape[1])) for r, n in src], axis=0)
    return [rows[:, None, i * d:(i + 1) * d] for i in range(6)]


def _pad_cols(w, n):
    return jnp.pad(w, ((0, 0), (0, n - w.shape[1])))


def kernel(x, c, ctx, c_ctx, l0_ada_w, l0_ada_b, l0_ln1_g, l0_ln1_b, l0_w_in, l0_conv_a, l0_a_log_a, l0_dt_bias_a, l0_norm_a, l0_conv_b, l0_conv_b_bias, l0_a_log_b, l0_dt_bias_b, l0_d_skip_b, l0_norm_b, l0_w_out, l0_ln2_g, l0_ln2_b, l0_peer_q, l0_peer_keys, l0_peer_u, l0_peer_v, l1_ada_w, l1_ada_b, l1_ln1_g, l1_ln1_b, l1_w_in, l1_conv_c, l1_i_bias, l1_f_bias, l1_norm_c, l1_w_out, l1_ln2_g, l1_ln2_b, l1_peer_q, l1_peer_keys, l1_peer_u, l1_peer_v):
    bsz, n_lat, d = x.shape
    n_ctx_tok = ctx.shape[1]
    rows_grid = n_lat // GRID_W
    seq = n_ctx_tok + n_lat
    n_ctx = n_ctx_tok // CHUNK
    tiles_b = seq // ROW_TILE
    ctx_tiles = n_ctx_tok // ROW_TILE
    lat_tiles = n_lat // ROW_TILE

    cond = jnp.concatenate([c, c_ctx[None, :], jnp.zeros((8 - bsz - 1, d), f32)], axis=0)
    src_all = [run for b in range(bsz) for run in ((bsz, ctx_tiles), (b, lat_tiles))]

    x_lat = x + _grid_sincos(n_lat, d).astype(x.dtype)
    xs_all = jnp.concatenate([ctx, x_lat], axis=1).reshape(bsz * seq, d)

    sh1, sc1, g1, sh2, sc2, g2 = _tile_rows(_ada(cond, l0_ada_w, l0_ada_b), src_all)
    qkv_w = A_HEADS * (2 * A_DK + A_DV)
    az = A_HEADS * A_DV
    ag = N_DIR * A_HEADS
    bd = B_HEADS * B_HEADDIM
    bx = bd + 2 * B_GROUPS * B_STATE
    bt = N_DIR * B_HEADS
    o0 = np.cumsum([0, qkv_w, az, ag, ag, bd, bx, bt])
    w_qkv, w_za, w_beta, w_dec, w_zb, w_xbc, w_dt = (l0_w_in[:, o0[i]:o0[i + 1]] for i in range(7))
    main_w = qkv_w + az + bd + bx
    w0 = _pad_cols(jnp.concatenate([w_qkv, w_za, w_zb, w_xbc, w_beta, w_dec, w_dt], axis=1), main_w + LANE)
    proj, gates = _proj(xs_all, sh1, sc1, w0.astype(bf16))
    proj = proj.reshape(bsz, seq, main_w + LANE)
    gates = gates.reshape(bsz, seq, LANE)
    conv_w = jnp.concatenate([l0_conv_a, l0_conv_b], axis=1)
    conv_b = jnp.concatenate([jnp.zeros((qkv_w,), f32), l0_conv_b_bias]).reshape(1, -1)
    xbc_blk = (qkv_w + az + bd) // LANE
    col_blocks = list(range(qkv_w // LANE)) + list(range(xbc_blk, xbc_blk + bx // LANE))
    act = _conv_act(proj, conv_w, conv_b, col_blocks, seg0=n_ctx_tok,
                    n_norm=2 * A_HEADS, n_qscale=A_HEADS, qscale=A_DK ** -0.5)
    o_a, o_b = _run_scans([_gdn(act, gates[..., :ag], gates[..., ag:2 * ag], l0_a_log_a, l0_dt_bias_a, n_ctx),
                           _ssd(act, gates[..., 2 * ag:2 * ag + bt], l0_a_log_b, l0_dt_bias_b, n_ctx)],
                          seq // CHUNK)
    x1 = _merge_even(xs_all, [o.reshape(bsz * seq, az) for o in o_a], [o.reshape(bsz * seq, bd) for o in o_b],
                     proj.reshape(bsz * seq, -1), act.reshape(bsz * seq, -1),
                     l0_norm_a, l0_norm_b, l0_d_skip_b, l0_w_out.astype(bf16), g1, l0_ln1_g, l0_ln1_b)
    x2 = _peer_residual(x1, sh2, sc2, g2, l0_peer_q, l0_peer_keys, l0_peer_u, l0_peer_v, l0_ln2_g, l0_ln2_b)

    sh1, sc1, g1, sh2, sc2, g2 = _tile_rows(_ada(cond, l1_ada_w, l1_ada_b), src_all)
    x2 = x2.reshape(bsz, seq, d)
    x2_ctx, x2_lat = x2[:, :n_ctx_tok], x2[:, n_ctx_tok:]
    lat_cm = x2_lat.reshape(bsz, rows_grid, GRID_W, d).transpose(0, 2, 1, 3).reshape(bsz, n_lat, d)
    xs1 = jnp.concatenate([x2_ctx, lat_cm], axis=1).reshape(bsz * seq, d)
    qk_w = 2 * C_HEADS * C_DK
    vw = C_HEADS * C_DV
    w1 = _pad_cols(l1_w_in, qk_w + 2 * vw + LANE)
    proj1, gates1 = _proj(xs1, sh1, sc1, w1.astype(bf16))
    proj1 = proj1.reshape(bsz, seq, -1)
    gates1 = gates1.reshape(bsz, seq, LANE)
    act1 = _conv_act(proj1, l1_conv_c, jnp.zeros((1, qk_w), f32), list(range(qk_w // LANE)), seg0=n_ctx_tok,
                     n_norm=0, n_qscale=C_HEADS, qscale=C_DK ** -0.5)
    cg = N_DIR * C_HEADS
    (h1,) = _run_scans([_mlstm(act1, proj1, gates1[..., :cg], gates1[..., cg:2 * cg], l1_i_bias, l1_f_bias, n_ctx)],
                       seq // CHUNK)
    y_cm = _merge_odd([h.reshape(bsz * seq, vw) for h in h1], proj1.reshape(bsz * seq, -1), l1_norm_c,
                      l1_w_out.astype(bf16), tiles_b, ctx_tiles, bsz)
    y_lat = y_cm.reshape(bsz, GRID_W, rows_grid, d).transpose(0, 2, 1, 3).reshape(bsz * n_lat, d)
    lat_sel = lambda t: t.reshape(bsz, tiles_b, 1, d)[:, ctx_tiles:].reshape(bsz * lat_tiles, 1, d)
    x3 = _resid_ln(x2_lat.reshape(bsz * n_lat, d), y_lat, lat_sel(g1), l1_ln1_g, l1_ln1_b)
    x4 = _peer_residual(x3, lat_sel(sh2), lat_sel(sc2), lat_sel(g2), l1_peer_q, l1_peer_keys, l1_peer_u,
                        l1_peer_v, l1_ln2_g, l1_ln2_b)
    return x4.reshape(bsz, n_lat, d)
```
